```python
import jax, jax.numpy as jnp
from jax import lax
import numpy as np

D_MODEL = 1024
BATCH = 2
SEQ = 8192
DEPTH = 1

A_HEADS = 8
A_HEAD_DIM = 64
A_WIDTH = A_HEADS * A_HEAD_DIM
Q_LORA_RANK = 256
IDX_HEADS = 4
IDX_DIM = 64
TOPK_MAX = 256
Q_BLOCK = 128
B_WIDTH = 512
B_EXPAND = 128
B_HEADS = B_WIDTH // B_EXPAND
B_HEAD_V = B_WIDTH // B_HEADS
CHUNK = 16
D_FF = 4 * D_MODEL
EPS = 1e-6

COL_SIZES = (Q_LORA_RANK, A_WIDTH, A_WIDTH, IDX_DIM, IDX_HEADS,
             B_WIDTH, B_WIDTH, B_WIDTH, B_WIDTH, D_MODEL, D_MODEL)
D_IN = sum(COL_SIZES)

kernel_name = 'hybrid_dsa_hgrn2_gated_block'


def rms_norm(x, g):
    xf = x.astype(jnp.float32)
    y = xf * lax.rsqrt(jnp.mean(xf * xf, axis=-1, keepdims=True) + EPS)
    return (y * g.astype(jnp.float32)).astype(x.dtype)


def split_cols(t, sizes):
    points, acc = [], 0
    for s in sizes[:-1]:
        acc += s
        points.append(acc)
    return jnp.split(t, points, axis=-1)


def dsa_attention(q, k, v, q_idx, k_idx, w_idx):
    B, L = q.shape[0], q.shape[1]
    topk = min(TOPK_MAX, L // 4)
    nblk = L // Q_BLOCK
    key_pos = jnp.arange(L)
    scale = A_HEAD_DIM ** -0.5
    idx_scale = (IDX_DIM ** -0.5) * (IDX_HEADS ** -0.5)
    k_idx_f = k_idx.astype(jnp.float32)

    def to_blocks(t):
        return t.reshape((B, nblk, Q_BLOCK) + t.shape[2:]).swapaxes(0, 1)

    def block(args):
        qb, qib, wb, start = args
        q_pos = start + jnp.arange(Q_BLOCK)
        causal = key_pos[None, :] <= q_pos[:, None]
        rel = jax.nn.relu(jnp.einsum('bthd,bsd->bths', qib.astype(jnp.float32), k_idx_f))
        score = jnp.einsum('bth,bths->bts', wb.astype(jnp.float32) * idx_scale, rel)
        score = jnp.where(causal[None], score, -jnp.inf)
        top_val, top_idx = lax.top_k(score, topk)
        valid = top_val > -jnp.inf
        kg = jax.vmap(lambda kb, ib: kb[ib])(k, top_idx)
        vg = jax.vmap(lambda vb, ib: vb[ib])(v, top_idx)
        logits = jnp.einsum('bthd,btkhd->bhtk', qb.astype(jnp.float32), kg.astype(jnp.float32)) * scale
        logits = jnp.where(valid[:, None], logits, -jnp.inf)
        p = jax.nn.softmax(logits, axis=-1)
        o = jnp.einsum('bhtk,btkhd->bthd', p, vg.astype(jnp.float32))
        return o.astype(q.dtype)

    starts = jnp.arange(nblk) * Q_BLOCK
    out = lax.map(block, (to_blocks(q), to_blocks(q_idx), to_blocks(w_idx), starts))
    return out.swapaxes(0, 1).reshape(B, L, A_WIDTH)


def hgrn2_chunkwise(q, f_logit, i, lb):
    B, L, H, dk = q.shape
    dv = i.shape[-1]
    n = L // CHUNK
    lbh = lb.reshape(H, dk).astype(jnp.float32)
    f = lbh + (1.0 - lbh) * jax.nn.sigmoid(f_logit.astype(jnp.float32))
    k = 1.0 - f
    log_f = jnp.log(f)

    def chunks(t):
        return t.reshape(B, n, CHUNK, H, t.shape[-1])

    qc, kc, vc, gc = chunks(q.astype(jnp.float32)), chunks(k), chunks(i.astype(jnp.float32)), chunks(log_f)
    b = jnp.cumsum(gc, axis=2)
    b_end = b[:, :, -1]
    causal = jnp.tril(jnp.ones((CHUNK, CHUNK), dtype=bool))
    diff = b[:, :, :, None] - b[:, :, None, :]
    decay = jnp.exp(jnp.where(causal[None, None, :, :, None, None], diff, -jnp.inf))
    attn = jnp.einsum('bnthd,bnshd,bntshd->bnhts', qc, kc, decay)
    o_intra = jnp.einsum('bnhts,bnshv->bnthv', attn, vc)
    kv = jnp.einsum('bnshd,bnshv->bnhdv', kc * jnp.exp(b_end[:, :, None] - b), vc)

    def step(S, inp):
        dec, upd = inp
        return dec[..., None] * S + upd, S

    S0 = jnp.zeros((B, H, dk, dv), jnp.float32)
    _, S_prev = lax.scan(step, S0, (jnp.exp(b_end).swapaxes(0, 1), kv.swapaxes(0, 1)))
    S_prev = S_prev.swapaxes(0, 1)
    o_inter = jnp.einsum('bnthd,bnhdv->bnthv', qc * jnp.exp(b), S_prev)
    return (o_intra + o_inter).reshape(B, L, H, dv)


def setup_inputs(seed: int = 0) -> dict:
    key = jax.random.key(seed)
    ks = jax.random.split(key, 24)
    f32 = jnp.float32

    def nrm(k, shape, fan_in):
        return jax.random.normal(k, shape, f32) * (fan_in ** -0.5)

    def gain(k, shape):
        return 1.0 + 0.05 * jax.random.normal(k, shape, f32)

    def bias(k, shape, s=0.01):
        return s * jax.random.normal(k, shape, f32)

    return {
        'x': jax.random.normal(ks[0], (BATCH, SEQ, D_MODEL), f32),
        'c': jax.random.normal(ks[1], (BATCH, D_MODEL), f32),
        'w_ada': nrm(ks[2], (DEPTH, D_MODEL, 6 * D_MODEL), D_MODEL),
        'b_ada': bias(ks[3], (DEPTH, 6 * D_MODEL), 0.02),
        'norm1_g': gain(ks[4], (DEPTH, D_MODEL)),
        'w_in': nrm(ks[5], (DEPTH, D_MODEL, D_IN), D_MODEL),
        'q_lat_norm_g': gain(ks[6], (DEPTH, Q_LORA_RANK)),
        'w_uq': nrm(ks[7], (DEPTH, Q_LORA_RANK, A_WIDTH), Q_LORA_RANK),
        'w_uq_idx': nrm(ks[8], (DEPTH, Q_LORA_RANK, IDX_HEADS * IDX_DIM), Q_LORA_RANK),
        'q_norm_g': gain(ks[9], (DEPTH, A_HEAD_DIM)),
        'k_norm_g': gain(ks[10], (DEPTH, A_HEAD_DIM)),
        'k_idx_norm_g': gain(ks[11], (DEPTH, IDX_DIM)),
        'hgrn_lb': 0.5 * jax.random.normal(ks[12], (DEPTH + 1, B_WIDTH), f32),
        'hgrn_o_norm_g': gain(ks[13], (DEPTH, B_WIDTH)),
        'w_proj_a': nrm(ks[14], (DEPTH, A_WIDTH, D_MODEL), A_WIDTH),
        'w_proj_b': nrm(ks[15], (DEPTH, B_WIDTH, D_MODEL), B_WIDTH),
        'w_out': nrm(ks[16], (DEPTH, D_MODEL, D_MODEL), D_MODEL),
        'norm2_g': gain(ks[17], (DEPTH, D_MODEL)),
        'w_mlp1': nrm(ks[18], (DEPTH, D_MODEL, D_FF), D_MODEL),
        'b_mlp1': bias(ks[19], (DEPTH, D_FF)),
        'w_mlp2': nrm(ks[20], (DEPTH, D_FF, D_MODEL), D_FF),
        'b_mlp2': bias(ks[21], (DEPTH, D_MODEL)),
    }


def reference(x, c, w_ada, b_ada, norm1_g, w_in, q_lat_norm_g, w_uq, w_uq_idx, q_norm_g, k_norm_g,
              k_idx_norm_g, hgrn_lb, hgrn_o_norm_g, w_proj_a, w_proj_b, w_out, norm2_g,
              w_mlp1, b_mlp1, w_mlp2, b_mlp2):
    B, L = x.shape[0], x.shape[1]
    lb_all = jnp.cumsum(jax.nn.softmax(hgrn_lb.astype(jnp.float32), axis=0), axis=0)
    c_act = jax.nn.silu(c)
    for l in range(DEPTH):
        mod = c_act @ w_ada[l] + b_ada[l]
        shift1, scale1, gate1, shift2, scale2, gate2 = jnp.split(mod, 6, axis=-1)

        h = rms_norm(x, norm1_g[l]) * (1.0 + scale1[:, None]) + shift1[:, None]
        proj = h @ w_in[l]
        (q_lat, a_k, a_v, k_idx, w_idx, b_q, b_f, b_i, b_g, gate_a, gate_b) = split_cols(proj, COL_SIZES)

        q_lat = rms_norm(q_lat, q_lat_norm_g[l])
        a_q = rms_norm((q_lat @ w_uq[l]).reshape(B, L, A_HEADS, A_HEAD_DIM), q_norm_g[l])
        a_k = rms_norm(a_k.reshape(B, L, A_HEADS, A_HEAD_DIM), k_norm_g[l])
        a_v = a_v.reshape(B, L, A_HEADS, A_HEAD_DIM)
        q_idx = (q_lat @ w_uq_idx[l]).reshape(B, L, IDX_HEADS, IDX_DIM)
        k_idx = rms_norm(k_idx, k_idx_norm_g[l])
        out_a = dsa_attention(a_q, a_k, a_v, q_idx, k_idx, w_idx)

        hq = jax.nn.silu(b_q).reshape(B, L, B_HEADS, B_EXPAND)
        hf = b_f.reshape(B, L, B_HEADS, B_EXPAND)
        hi = b_i.reshape(B, L, B_HEADS, B_HEAD_V)
        o_b = hgrn2_chunkwise(hq, hf, hi, lb_all[l]).astype(x.dtype)
        o_b = rms_norm(o_b, hgrn_o_norm_g[l].reshape(B_HEADS, B_HEAD_V)).reshape(B, L, B_WIDTH)
        out_b = o_b * jax.nn.silu(b_g)

        merged = jax.nn.sigmoid(gate_a) * (out_a @ w_proj_a[l]) + jax.nn.sigmoid(gate_b) * (out_b @ w_proj_b[l])
        x = x + gate1[:, None] * (merged @ w_out[l])

        h2 = rms_norm(x, norm2_g[l]) * (1.0 + scale2[:, None]) + shift2[:, None]
        y = jnp.square(jax.nn.relu(h2 @ w_mlp1[l] + b_mlp1[l])) @ w_mlp2[l] + b_mlp2[l]
        x = x + gate2[:, None] * y
    return x
```

```python
import functools

import jax
import jax.numpy as jnp
from jax import lax
from jax.experimental import pallas as pl
from jax.experimental.pallas import tpu as pltpu

F32 = jnp.float32
BF16 = jnp.bfloat16
I32 = jnp.int32

EPS = 1e-6
A_HEADS = 8
A_HEAD_DIM = 64
A_WIDTH = A_HEADS * A_HEAD_DIM
Q_LORA_RANK = 256
IDX_HEADS = 4
IDX_DIM = 64
TOPK_MAX = 256
B_WIDTH = 512
B_HEADS = 4
B_HEAD_DIM = 128
HGRN_CHUNK = 16

LANES = 128
VMEM_LIMIT_BYTES = 56 * 1024 * 1024

COL_AK, COL_AV, COL_BQ, COL_BF, COL_BI, COL_BG = 0, 512, 1024, 1536, 2048, 2560
COL_GA, COL_GB, COL_QL, COL_KW = 3072, 4096, 5120, 5376
PROJ_COLS = 5504

INT_MIN = -2147483648
NEG_INF_KEY = -2139095041


def _sigmoid(x):
    return 1.0 / (1.0 + jnp.exp(-x))


def _split2(x):
    hi = x.astype(BF16)
    lo = (x - hi.astype(F32)).astype(BF16)
    return hi, lo


def _split3(x):
    a = x.astype(BF16)
    r = x - a.astype(F32)
    b = r.astype(BF16)
    c = (r - b.astype(F32)).astype(BF16)
    return a, b, c


def _dot(a, b):
    return jnp.dot(a, b, preferred_element_type=F32)


def _log2(n):
    assert n > 0 and n & (n - 1) == 0, n
    return n.bit_length() - 1


def _group_ones(n, group):
    r = lax.broadcasted_iota(I32, (n, n), 0) >> _log2(group)
    c = lax.broadcasted_iota(I32, (n, n), 1) >> _log2(group)
    return jnp.where(r == c, 1.0, 0.0).astype(BF16)


def _group_mean_sq(x, ones_bd, group):
    hi, lo = _split2(x * x)
    return (_dot(hi, ones_bd) + _dot(lo, ones_bd)) * (1.0 / group)


def _adaln_kernel(c_ref, w_ref, b_ref, o_ref):
    c = c_ref[...]
    a = c * _sigmoid(c)
    a1, a2, a3 = _split3(a)
    w1, w2, w3 = _split3(w_ref[...])
    acc = _dot(a1, w1) + (_dot(a1, w2) + _dot(a2, w1)) + (_dot(a2, w2) + _dot(a1, w3) + _dot(a3, w1))
    o_ref[...] = acc + b_ref[...]


def _adaln(c_pad, w, b):
    rows, d = c_pad.shape
    n = w.shape[1]
    tn = 1536
    return pl.pallas_call(
        _adaln_kernel,
        grid=(n // tn,),
        in_specs=[
            pl.BlockSpec((rows, d), lambda j: (0, 0)),
            pl.BlockSpec((d, tn), lambda j: (0, j)),
            pl.BlockSpec((1, tn), lambda j: (0, j)),
        ],
        out_specs=pl.BlockSpec((rows, tn), lambda j: (0, j)),
        out_shape=jax.ShapeDtypeStruct((rows, n), F32),
        compiler_params=pltpu.CompilerParams(vmem_limit_bytes=VMEM_LIMIT_BYTES),
        name="adaln",
    )(c_pad, w, b)


def _inproj_kernel(x_ref, g_ref, sc_ref, sh_ref, w_ref, o_ref):
    x = x_ref[...]
    ms = jnp.mean(x * x, axis=-1, keepdims=True)
    h = x * lax.rsqrt(ms + EPS) * g_ref[...]
    h = h * (1.0 + sc_ref[0]) + sh_ref[0]
    o_ref[...] = _dot(h.astype(BF16), w_ref[...])


def _inproj(x2, g, scale, shift, w, seq, tm):
    m, d = x2.shape
    n = w.shape[1]
    tpb = seq // tm
    return pl.pallas_call(
        _inproj_kernel,
        grid=(m // tm,),
        in_specs=[
            pl.BlockSpec((tm, d), lambda i: (i, 0)),
            pl.BlockSpec((1, d), lambda i: (0, 0)),
            pl.BlockSpec((1, 1, d), lambda i: (i // tpb, 0, 0)),
            pl.BlockSpec((1, 1, d), lambda i: (i // tpb, 0, 0)),
            pl.BlockSpec((d, n), lambda i: (0, 0)),
        ],
        out_specs=pl.BlockSpec((tm, n), lambda i: (i, 0)),
        out_shape=jax.ShapeDtypeStruct((m, n), F32),
        compiler_params=pltpu.CompilerParams(vmem_limit_bytes=VMEM_LIMIT_BYTES),
        name="inproj",
    )(x2, g, scale, shift, w)


def _qkprep_kernel(ak_ref, ql_ref, kw_ref, gql_ref, wuq_ref, gq_ref, gk_ref, gki_ref,
                   aq_out, ak_out, qi_out, ki_out, w_out):
    ones64 = _group_ones(A_WIDTH, A_HEAD_DIM)
    ql = ql_ref[...]
    ql = ql * lax.rsqrt(jnp.mean(ql * ql, axis=-1, keepdims=True) + EPS) * gql_ref[...]
    up = _dot(ql.astype(BF16), wuq_ref[...])
    aq = up[:, :A_WIDTH]
    aq = aq * lax.rsqrt(_group_mean_sq(aq, ones64, A_HEAD_DIM) + EPS) * gq_ref[...]
    aq_out[...] = (aq * (A_HEAD_DIM ** -0.5)).astype(BF16)
    qi_out[...] = up[:, A_WIDTH:]
    ak = ak_ref[...]
    ak = ak * lax.rsqrt(_group_mean_sq(ak, ones64, A_HEAD_DIM) + EPS) * gk_ref[...]
    ak_out[...] = ak.astype(BF16)
    kw = kw_ref[...]
    lane = lax.broadcasted_iota(I32, kw.shape, 1)
    ksq = jnp.where(lane < IDX_DIM, kw * kw, 0.0)
    kms = jnp.sum(ksq, axis=-1, keepdims=True) * (1.0 / IDX_DIM)
    kn = kw * lax.rsqrt(kms + EPS) * gki_ref[...]
    ki_out[...] = kn[:, :IDX_DIM]
    idx_scale = (IDX_DIM ** -0.5) * (IDX_HEADS ** -0.5)
    w_out[...] = jnp.where((lane >= IDX_DIM) & (lane < IDX_DIM + IDX_HEADS), kw * idx_scale, 0.0)


def _qkprep(proj, gql, wuq, gq, gk, gki, tm):
    m = proj.shape[0]
    row = lambda i: (i, 0)
    const = lambda i: (0, 0)
    return pl.pallas_call(
        _qkprep_kernel,
        grid=(m // tm,),
        in_specs=[
            pl.BlockSpec((tm, A_WIDTH), lambda i: (i, COL_AK // A_WIDTH)),
            pl.BlockSpec((tm, Q_LORA_RANK), lambda i: (i, COL_QL // Q_LORA_RANK)),
            pl.BlockSpec((tm, LANES), lambda i: (i, COL_KW // LANES)),
            pl.BlockSpec((1, Q_LORA_RANK), const),
            pl.BlockSpec(wuq.shape, const),
            pl.BlockSpec((1, A_WIDTH), const),
            pl.BlockSpec((1, A_WIDTH), const),
            pl.BlockSpec((1, LANES), const),
        ],
        out_specs=[
            pl.BlockSpec((tm, A_WIDTH), row),
            pl.BlockSpec((tm, A_WIDTH), row),
            pl.BlockSpec((tm, IDX_HEADS * IDX_DIM), row),
            pl.BlockSpec((tm, IDX_DIM), row),
            pl.BlockSpec((tm, LANES), row),
        ],
        out_shape=[
            jax.ShapeDtypeStruct((m, A_WIDTH), BF16),
            jax.ShapeDtypeStruct((m, A_WIDTH), BF16),
            jax.ShapeDtypeStruct((m, IDX_HEADS * IDX_DIM), F32),
            jax.ShapeDtypeStruct((m, IDX_DIM), F32),
            jax.ShapeDtypeStruct((m, LANES), F32),
        ],
        compiler_params=pltpu.CompilerParams(vmem_limit_bytes=VMEM_LIMIT_BYTES),
        name="qkprep",
    )(proj, proj, proj, gql, wuq, gq, gk, gki)


def _dsa_kernel(qi_ref, w_ref, kti_ref, q_ref, kt_ref, v_ref, tri_ref, o_ref,
                keys_ref, acc_ref, *, tq, ck, topk, rb):
    i = pl.program_id(1)
    n_chunks = ((i + 1) * tq + ck - 1) // ck
    nrb = tq // rb
    f_topk = float(topk)

    def score_body(jc, carry):
        kc = kti_ref[0, jc].astype(BF16)
        s_pos = jc * ck + lax.broadcasted_iota(I32, (rb, ck), 1)
        for r in range(nrb):
            rows = pl.ds(r * rb, rb)
            wv = w_ref[0, rows, :]
            score = jnp.zeros((rb, ck), F32)
            for h in range(IDX_HEADS):
                s = _dot(qi_ref[0, h, rows, :].astype(BF16), kc)
                score = score + wv[:, IDX_DIM + h:IDX_DIM + h + 1] * jnp.maximum(s, 0.0)
            t_pos = i * tq + r * rb + lax.broadcasted_iota(I32, (rb, ck), 0)
            score = jnp.where(s_pos <= t_pos, score + 0.0, -jnp.inf)
            bits = pltpu.bitcast(score, I32)
            keys_ref[jc, rows, :] = bits ^ ((bits >> 31) & 0x7FFFFFFF)
        return carry

    lax.fori_loop(0, n_chunks, score_body, 0)

    def count_ge(r, cand):
        cand_b = jnp.broadcast_to(cand, (rb, LANES))

        def body(jc, acc):
            blk = keys_ref[jc, pl.ds(r * rb, rb), :]
            for s in range(ck // LANES):
                acc = acc + jnp.where(blk[:, s * LANES:(s + 1) * LANES] >= cand_b, 1.0, 0.0)
            return acc

        acc = lax.fori_loop(0, n_chunks, body, jnp.zeros((rb, LANES), F32))
        return jnp.sum(acc, axis=-1, keepdims=True)

    for r in range(nrb):
        rows = pl.ds(r * rb, rb)
        zero = jnp.zeros((rb, 1), I32)
        thr = jnp.where(count_ge(r, zero) >= f_topk, zero, jnp.full((rb, 1), INT_MIN, I32))

        def bit_body(it, thr):
            cand = thr | (jnp.int32(1) << (30 - it))
            return jnp.where(count_ge(r, cand) >= f_topk, cand, thr)

        thr = lax.fori_loop(0, 31, bit_body, thr)
        cnt_gt = count_ge(r, thr + 1)
        cnt_ge = count_ge(r, thr)
        is_ninf = thr == NEG_INF_KEY
        need = jnp.where(is_ninf, 0.0, f_topk - cnt_gt)
        tie = (cnt_ge - cnt_gt) != need
        need = jnp.where(tie, need, 2.0 * ck * 65536.0)

        @pl.when(jnp.max(jnp.where(tie & jnp.logical_not(is_ninf), 1.0, 0.0)) > 0.0)
        def _():
            def tie_body(jc, carry):
                blk = keys_ref[jc, rows, :]
                eq = blk == thr
                pc = _dot(jnp.where(eq, 1.0, 0.0).astype(BF16), tri_ref[...]) + carry
                keys_ref[jc, rows, :] = jnp.where(eq & (pc > need), thr - 1, blk)
                return pc[:, ck - 1:ck]

            lax.fori_loop(0, n_chunks, tie_body, jnp.zeros((rb, 1), F32))

        keff = jnp.where(is_ninf, thr + 1, thr)

        keff_b = jnp.broadcast_to(keff, (rb, ck))
        lane = lax.broadcasted_iota(I32, (rb, LANES), 1)
        low = lane < A_HEAD_DIM
        acc_ref[:, rows, :] = jnp.zeros((A_HEADS // 2, rb, LANES), F32)

        def att_body(jc, carry):
            m_all, l_all = carry
            sel = keys_ref[jc, rows, :] >= keff_b
            krows = pl.ds(pl.multiple_of(jc * ck, ck), ck)
            m_out, l_out = [], []
            for hp in range(A_HEADS // 2):
                vc = v_ref[0, krows, hp * LANES:(hp + 1) * LANES]
                pv_pair, alpha_pair = [], []
                for e in range(2):
                    h = 2 * hp + e
                    s = _dot(q_ref[0, h, rows, :], kt_ref[0, jc, h])
                    s = jnp.where(sel, s, -jnp.inf)
                    m_new = jnp.maximum(m_all[h], jnp.max(s, axis=-1, keepdims=True))
                    m_safe = jnp.where(m_new == -jnp.inf, 0.0, m_new)
                    p = jnp.exp(s - m_safe)
                    alpha = jnp.exp(m_all[h] - m_safe)
                    l_out.append(l_all[h] * alpha + jnp.sum(p, axis=-1, keepdims=True))
                    m_out.append(m_new)
                    pv_pair.append(_dot(p.astype(BF16), vc))
                    alpha_pair.append(alpha)
                alpha_l = jnp.where(low, alpha_pair[0], alpha_pair[1])
                pv = jnp.where(low, pv_pair[0], pv_pair[1])
                acc_ref[hp, rows, :] = acc_ref[hp, rows, :] * alpha_l + pv
            return tuple(m_out), tuple(l_out)

        m0 = tuple(jnp.full((rb, 1), -jnp.inf, F32) for _ in range(A_HEADS))
        l0 = tuple(jnp.zeros((rb, 1), F32) for _ in range(A_HEADS))
        _, l_fin = lax.fori_loop(0, n_chunks, att_body, (m0, l0))
        for hp in range(A_HEADS // 2):
            inv = jnp.where(low, 1.0 / l_fin[2 * hp], 1.0 / l_fin[2 * hp + 1])
            o_ref[0, rows, hp * LANES:(hp + 1) * LANES] = (acc_ref[hp, rows, :] * inv).astype(o_ref.dtype)


def _dsa(qi, w, kti, q, kt, v, tri, tq, ck, topk):
    bsz, hi, seq, di = qi.shape
    nc = seq // ck
    kern = functools.partial(_dsa_kernel, tq=tq, ck=ck, topk=topk, rb=min(tq, 128))
    return pl.pallas_call(
        kern,
        grid=(bsz, seq // tq),
        in_specs=[
            pl.BlockSpec((1, hi, tq, di), lambda b, i: (b, 0, i, 0)),
            pl.BlockSpec((1, tq, LANES), lambda b, i: (b, i, 0)),
            pl.BlockSpec((1, nc, di, ck), lambda b, i: (b, 0, 0, 0)),
            pl.BlockSpec((1, A_HEADS, tq, A_HEAD_DIM), lambda b, i: (b, 0, i, 0)),
            pl.BlockSpec((1, nc, A_HEADS, A_HEAD_DIM, ck), lambda b, i: (b, 0, 0, 0, 0)),
            pl.BlockSpec((1, seq, A_WIDTH), lambda b, i: (b, 0, 0)),
            pl.BlockSpec((ck, ck), lambda b, i: (0, 0)),
        ],
        out_specs=pl.BlockSpec((1, tq, A_WIDTH), lambda b, i: (b, i, 0)),
        out_shape=jax.ShapeDtypeStruct((bsz, seq, A_WIDTH), BF16),
        scratch_shapes=[
            pltpu.VMEM((nc, tq, ck), I32),
            pltpu.VMEM((A_HEADS // 2, tq, LANES), F32),
        ],
        compiler_params=pltpu.CompilerParams(
            dimension_semantics=("arbitrary", "arbitrary"), vmem_limit_bytes=VMEM_LIMIT_BYTES),
        name="dsa",
    )(qi, w, kti, q, kt, v, tri)


def _hgrn_kernel(bq_ref, bf_ref, bi_ref, bg_ref, lb_ref, go_ref, o_ref, st_ref, oi_ref, *, tl, layer):
    cs = HGRN_CHUNK
    nch = tl // cs

    @pl.when(pl.program_id(1) == 0)
    def _():
        st_ref[...] = jnp.zeros(st_ref.shape, F32)

    lbr = lb_ref[...]
    slots = [lbr[k:k + 1] for k in range(lbr.shape[0])]
    mx = functools.reduce(jnp.maximum, slots)
    es = [jnp.exp(s - mx) for s in slots]
    lb = functools.reduce(jnp.add, es[:layer + 1]) / functools.reduce(jnp.add, es)

    bq = bq_ref[...]
    q = bq * _sigmoid(bq)
    f = lb + (1.0 - lb) * _sigmoid(bf_ref[...])
    kk = 1.0 - f
    g = jnp.log(f)
    v = bi_ref[...]

    r_i = lax.broadcasted_iota(I32, (tl, tl), 0)
    c_i = lax.broadcasted_iota(I32, (tl, tl), 1)
    same = (r_i >> _log2(cs)) == (c_i >> _log2(cs))
    tri = jnp.where(same & (c_i <= r_i), 1.0, 0.0).astype(BF16)
    blk = jnp.where(same, 1.0, 0.0).astype(BF16)
    g1, g2, g3 = _split3(g)
    b = _dot(tri, g1) + _dot(tri, g2) + _dot(tri, g3)
    bend = _dot(blk, g1) + _dot(blk, g2) + _dot(blk, g3)

    pos = lax.broadcasted_iota(I32, (tl, B_WIDTH), 0) & (cs - 1)
    o = jnp.zeros((tl, B_WIDTH), F32)
    for r in range(cs):
        if r == 0:
            kk_s, b_s, v_s = kk, b, v
        else:
            kk_s = pltpu.roll(kk, r, axis=0)
            b_s = pltpu.roll(b, r, axis=0)
            v_s = pltpu.roll(v, r, axis=0)
        e = q * kk_s * jnp.exp(jnp.where(pos >= r, b - b_s, -jnp.inf))
        parts = []
        for h in range(B_HEADS):
            sl = slice(h * B_HEAD_DIM, (h + 1) * B_HEAD_DIM)
            parts.append(jnp.sum(e[:, sl], axis=-1, keepdims=True) * v_s[:, sl])
        o = o + jnp.concatenate(parts, axis=1)

    qe = (q * jnp.exp(b)).astype(BF16)
    kd = (kk * jnp.exp(bend - b)).astype(BF16)
    vb = v.astype(BF16)
    dec = jnp.exp(bend)
    for c in range(nch):
        rs = slice(c * cs, (c + 1) * cs)
        for h in range(B_HEADS):
            sl = slice(h * B_HEAD_DIM, (h + 1) * B_HEAD_DIM)
            st = st_ref[h]
            oi_ref[rs, sl] = lax.dot_general(qe[rs, sl], st.astype(BF16), (((1,), (1,)), ((), ())),
                                             preferred_element_type=F32)
            upd = lax.dot_general(vb[rs, sl], kd[rs, sl], (((0,), (0,)), ((), ())),
                                  preferred_element_type=F32)
            st_ref[h] = st * dec[c * cs:c * cs + 1, sl] + upd
    o = o + oi_ref[...]

    parts = []
    for h in range(B_HEADS):
        sl = slice(h * B_HEAD_DIM, (h + 1) * B_HEAD_DIM)
        oh = o[:, sl]
        parts.append(oh * lax.rsqrt(jnp.mean(oh * oh, axis=-1, keepdims=True) + EPS))
    on = jnp.concatenate(parts, axis=1) * go_ref[...]
    bg = bg_ref[...]
    o_ref[...] = (on * (bg * _sigmoid(bg))).astype(o_ref.dtype)


def _hgrn(proj, lb_table, go, bsz, seq, tl, layer):
    m = proj.shape[0]
    tpb = seq // tl
    col = lambda c: (lambda b, t: (b * tpb + t, c // B_WIDTH))
    return pl.pallas_call(
        functools.partial(_hgrn_kernel, tl=tl, layer=layer),
        grid=(bsz, tpb),
        in_specs=[
            pl.BlockSpec((tl, B_WIDTH), col(COL_BQ)),
            pl.BlockSpec((tl, B_WIDTH), col(COL_BF)),
            pl.BlockSpec((tl, B_WIDTH), col(COL_BI)),
            pl.BlockSpec((tl, B_WIDTH), col(COL_BG)),
            pl.BlockSpec(lb_table.shape, lambda b, t: (0, 0)),
            pl.BlockSpec((1, B_WIDTH), lambda b, t: (0, 0)),
        ],
        out_specs=pl.BlockSpec((tl, B_WIDTH), lambda b, t: (b * tpb + t, 0)),
        out_shape=jax.ShapeDtypeStruct((m, B_WIDTH), BF16),
        scratch_shapes=[
            pltpu.VMEM((B_HEADS, B_HEAD_DIM, B_HEAD_DIM), F32),
            pltpu.VMEM((tl, B_WIDTH), F32),
        ],
        compiler_params=pltpu.CompilerParams(
            dimension_semantics=("arbitrary", "arbitrary"), vmem_limit_bytes=VMEM_LIMIT_BYTES),
        name="hgrn",
    )(proj, proj, proj, proj, lb_table, go)


def _merge_kernel(oa_ref, ob_ref, ga_ref, gb_ref, x_ref, g1_ref, wa_ref, wb_ref, wo_ref, o_ref):
    pa = _dot(oa_ref[...], wa_ref[...])
    pb = _dot(ob_ref[...], wb_ref[...])
    merged = _sigmoid(ga_ref[...]) * pa + _sigmoid(gb_ref[...]) * pb
    y = _dot(merged.astype(BF16), wo_ref[...])
    o_ref[...] = x_ref[...] + g1_ref[0] * y


def _merge(oa, ob, proj, x2, gate1, wa, wb, wo, seq, tm):
    m, d = x2.shape
    tpb = seq // tm
    row = lambda i: (i, 0)
    const = lambda i: (0, 0)
    return pl.pallas_call(
        _merge_kernel,
        grid=(m // tm,),
        in_specs=[
            pl.BlockSpec((tm, A_WIDTH), row),
            pl.BlockSpec((tm, B_WIDTH), row),
            pl.BlockSpec((tm, d), lambda i: (i, COL_GA // d)),
            pl.BlockSpec((tm, d), lambda i: (i, COL_GB // d)),
            pl.BlockSpec((tm, d), row),
            pl.BlockSpec((1, 1, d), lambda i: (i // tpb, 0, 0)),
            pl.BlockSpec(wa.shape, const),
            pl.BlockSpec(wb.shape, const),
            pl.BlockSpec(wo.shape, const),
        ],
        out_specs=pl.BlockSpec((tm, d), row),
        out_shape=jax.ShapeDtypeStruct((m, d), F32),
        compiler_params=pltpu.CompilerParams(vmem_limit_bytes=VMEM_LIMIT_BYTES),
        name="merge",
    )(oa, ob, proj, proj, x2, gate1, wa, wb, wo)


def _mlp_kernel(x_ref, g_ref, sc_ref, sh_ref, g2_ref, w1_ref, b1_ref, w2_ref, b2_ref, o_ref, *, tf):
    x = x_ref[...]
    ms = jnp.mean(x * x, axis=-1, keepdims=True)
    h = x * lax.rsqrt(ms + EPS) * g_ref[...]
    h = (h * (1.0 + sc_ref[0]) + sh_ref[0]).astype(BF16)
    dff = w1_ref.shape[1]
    y = jnp.zeros(x.shape, F32)
    for c in range(dff // tf):
        cs = slice(c * tf, (c + 1) * tf)
        a = jnp.maximum(_dot(h, w1_ref[:, cs]) + b1_ref[:, cs], 0.0)
        y = y + _dot((a * a).astype(BF16), w2_ref[cs, :])
    o_ref[...] = x + g2_ref[0] * (y + b2_ref[...])


def _mlp(x1, g, scale, shift, gate2, w1, b1, w2, b2, seq, tm, tf):
    m, d = x1.shape
    tpb = seq // tm
    row = lambda i: (i, 0)
    const = lambda i: (0, 0)
    bat = lambda i: (i // tpb, 0, 0)
    return pl.pallas_call(
        functools.partial(_mlp_kernel, tf=tf),
        grid=(m // tm,),
        in_specs=[
            pl.BlockSpec((tm, d), row),
            pl.BlockSpec((1, d), const),
            pl.BlockSpec((1, 1, d), bat),
            pl.BlockSpec((1, 1, d), bat),
            pl.BlockSpec((1, 1, d), bat),
            pl.BlockSpec(w1.shape, const),
            pl.BlockSpec(b1.shape, const),
            pl.BlockSpec(w2.shape, const),
            pl.BlockSpec(b2.shape, const),
        ],
        out_specs=pl.BlockSpec((tm, d), row),
        out_shape=jax.ShapeDtypeStruct((m, d), F32),
        compiler_params=pltpu.CompilerParams(vmem_limit_bytes=VMEM_LIMIT_BYTES),
        name="mlp",
    )(x1, g, scale, shift, gate2, w1, b1, w2, b2)


def _regroup_w_in(w):
    d = w.shape[0]
    sizes = (Q_LORA_RANK, A_WIDTH, A_WIDTH, IDX_DIM, IDX_HEADS, B_WIDTH, B_WIDTH, B_WIDTH, B_WIDTH, d, d)
    offs = [0]
    for s in sizes:
        offs.append(offs[-1] + s)
    part = lambda k: w[:, offs[k]:offs[k + 1]]
    q_lat, a_k, a_v, k_idx, w_idx, b_q, b_f, b_i, b_g, gate_a, gate_b = (part(k) for k in range(11))
    pad = jnp.zeros((d, LANES - IDX_DIM - IDX_HEADS), w.dtype)
    out = jnp.concatenate([a_k, a_v, b_q, b_f, b_i, b_g, gate_a, gate_b, q_lat, k_idx, w_idx, pad], axis=1)
    assert out.shape[1] == PROJ_COLS
    return out.astype(BF16)


def _layer(x, mod, l, p, tiles):
    bsz, seq, d = x.shape
    m = bsz * seq
    shift1, scale1, gate1, shift2, scale2, gate2 = (mod[:, k * d:(k + 1) * d].reshape(bsz, 1, d) for k in range(6))
    x2 = x.reshape(m, d)

    proj = _inproj(x2, p['norm1_g'][l][None], scale1, shift1, _regroup_w_in(p['w_in'][l]), seq, tiles['tm_in'])

    wuq = jnp.concatenate([p['w_uq'][l], p['w_uq_idx'][l]], axis=1).astype(BF16)
    gki = jnp.concatenate([p['k_idx_norm_g'][l], jnp.zeros((LANES - IDX_DIM,), F32)])[None]
    aq, ak, qi, ki, wi = _qkprep(
        proj, p['q_lat_norm_g'][l][None], wuq,
        jnp.tile(p['q_norm_g'][l], A_HEADS)[None], jnp.tile(p['k_norm_g'][l], A_HEADS)[None], gki, tiles['tm'])

    tq, ck = tiles['tq'], tiles['ck']
    nc = seq // ck
    topk = min(TOPK_MAX, seq // 4)
    qi4 = qi.reshape(bsz, seq, IDX_HEADS, IDX_DIM).transpose(0, 2, 1, 3)
    kti = ki.reshape(bsz, nc, ck, IDX_DIM).transpose(0, 1, 3, 2)
    q4 = aq.reshape(bsz, seq, A_HEADS, A_HEAD_DIM).transpose(0, 2, 1, 3)
    kt = ak.reshape(bsz, nc, ck, A_HEADS, A_HEAD_DIM).transpose(0, 1, 3, 4, 2)
    av = lax.slice_in_dim(proj, COL_AV, COL_AV + A_WIDTH, axis=1).astype(BF16).reshape(bsz, seq, A_WIDTH)
    tri = jnp.triu(jnp.ones((ck, ck), BF16))
    out_a = _dsa(qi4, wi.reshape(bsz, seq, LANES), kti, q4, kt, av, tri, tq, ck, topk).reshape(m, A_WIDTH)

    out_b = _hgrn(proj, p['hgrn_lb'], p['hgrn_o_norm_g'][l][None], bsz, seq, tiles['tl'], l)

    x1 = _merge(out_a, out_b, proj, x2, gate1, p['w_proj_a'][l].astype(BF16), p['w_proj_b'][l].astype(BF16),
                p['w_out'][l].astype(BF16), seq, tiles['tm'])
    out = _mlp(x1, p['norm2_g'][l][None], scale2, shift2, gate2, p['w_mlp1'][l].astype(BF16), p['b_mlp1'][l][None],
               p['w_mlp2'][l].astype(BF16), p['b_mlp2'][l][None], seq, tiles['tm'], tiles['tf'])
    return out.reshape(bsz, seq, d)


def _tiles(seq):
    pick = lambda want: min(want, seq)
    return dict(tm_in=pick(256), tm=pick(512), tq=pick(256), ck=pick(512), tl=pick(256), tf=1024)


def kernel(x, c, w_ada, b_ada, norm1_g, w_in, q_lat_norm_g, w_uq, w_uq_idx, q_norm_g, k_norm_g, k_idx_norm_g,
           hgrn_lb, hgrn_o_norm_g, w_proj_a, w_proj_b, w_out, norm2_g, w_mlp1, b_mlp1, w_mlp2, b_mlp2):
    p = dict(norm1_g=norm1_g, w_in=w_in, q_lat_norm_g=q_lat_norm_g, w_uq=w_uq, w_uq_idx=w_uq_idx,
             q_norm_g=q_norm_g, k_norm_g=k_norm_g, k_idx_norm_g=k_idx_norm_g, hgrn_lb=hgrn_lb,
             hgrn_o_norm_g=hgrn_o_norm_g, w_proj_a=w_proj_a, w_proj_b=w_proj_b, w_out=w_out, norm2_g=norm2_g,
             w_mlp1=w_mlp1, b_mlp1=b_mlp1, w_mlp2=w_mlp2, b_mlp2=b_mlp2)
    bsz, seq, d = x.shape
    depth = w_ada.shape[0]
    tiles = _tiles(seq)
    c_pad = jnp.zeros((8, d), F32).at[:bsz].set(c)
    for l in range(depth):
        mod = _adaln(c_pad, w_ada[l], b_ada[l][None])[:bsz]
        x = _layer(x, mod, l, p, tiles)
    return x
```

```python
import functools

import jax
import jax.numpy as jnp
from jax import lax
from jax.experimental import pallas as pl
from jax.experimental.pallas import tpu as pltpu

F32 = jnp.float32
BF16 = jnp.bfloat16
I32 = jnp.int32

EPS = 1e-6
A_HEADS = 8
A_HEAD_DIM = 64
A_WIDTH = A_HEADS * A_HEAD_DIM
Q_LORA_RANK = 256
IDX_HEADS = 4
IDX_DIM = 64
TOPK_MAX = 256
B_WIDTH = 512
B_HEADS = 4
B_HEAD_DIM = 128
HGRN_CHUNK = 16

LANES = 128
VMEM_LIMIT_BYTES = 56 * 1024 * 1024

COL_AK, COL_AV, COL_BQ, COL_BF, COL_BI, COL_BG = 0, 512, 1024, 1536, 2048, 2560
COL_GA, COL_GB, COL_QL, COL_KW = 3072, 4096, 5120, 5376
PROJ_COLS = 5504

LOG2E = 1.4426950408889634
BISECT_STEPS = 12
LOGIT_BOUND_LIMIT = 60.0


def _sigmoid(x):
    return 1.0 / (1.0 + jnp.exp(-x))


def _split2(x):
    hi = x.astype(BF16)
    lo = (x - hi.astype(F32)).astype(BF16)
    return hi, lo


def _split3(x):
    a = x.astype(BF16)
    r = x - a.astype(F32)
    b = r.astype(BF16)
    c = (r - b.astype(F32)).astype(BF16)
    return a, b, c


def _dot(a, b):
    return jnp.dot(a, b, preferred_element_type=F32)


def _log2(n):
    assert n > 0 and n & (n - 1) == 0, n
    return n.bit_length() - 1


def _group_ones(n, group):
    r = lax.broadcasted_iota(I32, (n, n), 0) >> _log2(group)
    c = lax.broadcasted_iota(I32, (n, n), 1) >> _log2(group)
    return jnp.where(r == c, 1.0, 0.0).astype(BF16)


def _group_mean_sq(x, ones_bd, group):
    hi, lo = _split2(x * x)
    return (_dot(hi, ones_bd) + _dot(lo, ones_bd)) * (1.0 / group)


def _adaln_kernel(c_ref, w_ref, b_ref, o_ref):
    c = c_ref[...]
    a = c * _sigmoid(c)
    a1, a2, a3 = _split3(a)
    w1, w2, w3 = _split3(w_ref[...])
    acc = _dot(a1, w1) + (_dot(a1, w2) + _dot(a2, w1)) + (_dot(a2, w2) + _dot(a1, w3) + _dot(a3, w1))
    o_ref[...] = acc + b_ref[...]


def _adaln(c_pad, w, b):
    rows, d = c_pad.shape
    n = w.shape[1]
    tn = 1536
    return pl.pallas_call(
        _adaln_kernel,
        grid=(n // tn,),
        in_specs=[
            pl.BlockSpec((rows, d), lambda j: (0, 0)),
            pl.BlockSpec((d, tn), lambda j: (0, j)),
            pl.BlockSpec((1, tn), lambda j: (0, j)),
        ],
        out_specs=pl.BlockSpec((rows, tn), lambda j: (0, j)),
        out_shape=jax.ShapeDtypeStruct((rows, n), F32),
        compiler_params=pltpu.CompilerParams(vmem_limit_bytes=VMEM_LIMIT_BYTES),
        name="adaln",
    )(c_pad, w, b)


def _inproj_kernel(x_ref, g_ref, sc_ref, sh_ref, w_ref, o_ref):
    x = x_ref[...]
    ms = jnp.mean(x * x, axis=-1, keepdims=True)
    h = x * lax.rsqrt(ms + EPS) * g_ref[...]
    h = h * (1.0 + sc_ref[0]) + sh_ref[0]
    o_ref[...] = _dot(h.astype(BF16), w_ref[...])


def _inproj(x2, g, scale, shift, w, seq, tm):
    m, d = x2.shape
    n = w.shape[1]
    tpb = seq // tm
    return pl.pallas_call(
        _inproj_kernel,
        grid=(m // tm,),
        in_specs=[
            pl.BlockSpec((tm, d), lambda i: (i, 0)),
            pl.BlockSpec((1, d), lambda i: (0, 0)),
            pl.BlockSpec((1, 1, d), lambda i: (i // tpb, 0, 0)),
            pl.BlockSpec((1, 1, d), lambda i: (i // tpb, 0, 0)),
            pl.BlockSpec((d, n), lambda i: (0, 0)),
        ],
        out_specs=pl.BlockSpec((tm, n), lambda i: (i, 0)),
        out_shape=jax.ShapeDtypeStruct((m, n), F32),
        compiler_params=pltpu.CompilerParams(vmem_limit_bytes=VMEM_LIMIT_BYTES),
        name="inproj",
    )(x2, g, scale, shift, w)


def _qkprep_kernel(ak_ref, ql_ref, kw_ref, gql_ref, wuq_ref, gq_ref, gk_ref, gki_ref,
                   aq_out, ak_out, qi_out, ki_out, w_out):
    ones64 = _group_ones(A_WIDTH, A_HEAD_DIM)
    ql = ql_ref[...]
    ql = ql * lax.rsqrt(jnp.mean(ql * ql, axis=-1, keepdims=True) + EPS) * gql_ref[...]
    up = _dot(ql.astype(BF16), wuq_ref[...])
    aq = up[:, :A_WIDTH]
    aq = aq * lax.rsqrt(_group_mean_sq(aq, ones64, A_HEAD_DIM) + EPS) * gq_ref[...]
    aq_out[...] = (aq * (A_HEAD_DIM ** -0.5 * LOG2E)).astype(BF16)
    qi_out[...] = up[:, A_WIDTH:]
    ak = ak_ref[...]
    ak = ak * lax.rsqrt(_group_mean_sq(ak, ones64, A_HEAD_DIM) + EPS) * gk_ref[...]
    ak_out[...] = ak.astype(BF16)
    kw = kw_ref[...]
    lane = lax.broadcasted_iota(I32, kw.shape, 1)
    ksq = jnp.where(lane < IDX_DIM, kw * kw, 0.0)
    kms = jnp.sum(ksq, axis=-1, keepdims=True) * (1.0 / IDX_DIM)
    kn = kw * lax.rsqrt(kms + EPS) * gki_ref[...]
    ki_out[...] = kn[:, :IDX_DIM]
    idx_scale = (IDX_DIM ** -0.5) * (IDX_HEADS ** -0.5)
    w_out[...] = jnp.where((lane >= IDX_DIM) & (lane < IDX_DIM + IDX_HEADS), kw * idx_scale, 0.0)


def _qkprep(proj, gql, wuq, gq, gk, gki, tm):
    m = proj.shape[0]
    row = lambda i: (i, 0)
    const = lambda i: (0, 0)
    return pl.pallas_call(
        _qkprep_kernel,
        grid=(m // tm,),
        in_specs=[
            pl.BlockSpec((tm, A_WIDTH), lambda i: (i, COL_AK // A_WIDTH)),
            pl.BlockSpec((tm, Q_LORA_RANK), lambda i: (i, COL_QL // Q_LORA_RANK)),
            pl.BlockSpec((tm, LANES), lambda i: (i, COL_KW // LANES)),
            pl.BlockSpec((1, Q_LORA_RANK), const),
            pl.BlockSpec(wuq.shape, const),
            pl.BlockSpec((1, A_WIDTH), const),
            pl.BlockSpec((1, A_WIDTH), const),
            pl.BlockSpec((1, LANES), const),
        ],
        out_specs=[
            pl.BlockSpec((tm, A_WIDTH), row),
            pl.BlockSpec((tm, A_WIDTH), row),
            pl.BlockSpec((tm, IDX_HEADS * IDX_DIM), row),
            pl.BlockSpec((tm, IDX_DIM), row),
            pl.BlockSpec((tm, LANES), row),
        ],
        out_shape=[
            jax.ShapeDtypeStruct((m, A_WIDTH), BF16),
            jax.ShapeDtypeStruct((m, A_WIDTH), BF16),
            jax.ShapeDtypeStruct((m, IDX_HEADS * IDX_DIM), F32),
            jax.ShapeDtypeStruct((m, IDX_DIM), F32),
            jax.ShapeDtypeStruct((m, LANES), F32),
        ],
        compiler_params=pltpu.CompilerParams(vmem_limit_bytes=VMEM_LIMIT_BYTES),
        name="qkprep",
    )(proj, proj, proj, gql, wuq, gq, gk, gki)


def _dsa_kernel(qi_ref, w_ref, kti_ref, q_ref, kt_ref, v_ref, tri_ref, o_ref,
                sc_ref, acc_ref, m_ref, tau_ref, kn_ref, *, tq, ck, topk, rb, nbis):
    i = pl.program_id(1)
    n_chunks = ((i + 1) * tq + ck - 1) // ck
    nrb = tq // rb
    nsl = ck // LANES
    kf = float(topk)
    neg_inf = float("-inf")
    pos_inf = float("inf")
    slab = lambda x, s: x[:, s * LANES:(s + 1) * LANES]

    @pl.when(i == 0)
    def _():
        def kn_body(jc, mx):
            k = kt_ref[0, jc].astype(F32)
            return jnp.maximum(mx, jnp.sum(k * k, axis=1, keepdims=True))

        mx = lax.fori_loop(0, kt_ref.shape[1], kn_body, jnp.zeros((A_HEADS, 1, ck), F32))
        kn_ref[...] = jnp.broadcast_to(jnp.max(mx, axis=2, keepdims=True), kn_ref.shape)

    def score_body(jc, carry):
        kc = kti_ref[0, jc].astype(BF16)
        s_pos = jc * ck + lax.broadcasted_iota(I32, (rb, ck), 1)
        out = []
        for r in range(nrb):
            mn, mx = carry[r]
            rows = pl.ds(r * rb, rb)
            wv = w_ref[0, rows, :]
            score = jnp.zeros((rb, ck), F32)
            for h in range(IDX_HEADS):
                s = _dot(qi_ref[0, h, rows, :].astype(BF16), kc)
                score = score + wv[:, IDX_DIM + h:IDX_DIM + h + 1] * jnp.maximum(s, 0.0)
            t_pos = i * tq + r * rb + lax.broadcasted_iota(I32, (rb, ck), 0)
            causal = s_pos <= t_pos
            sc_ref[jc, rows, :] = jnp.where(causal, score, neg_inf)
            hi_fill = jnp.where(causal, score, pos_inf)
            lo_fill = jnp.where(causal, score, neg_inf)
            for s in range(nsl):
                mn = jnp.minimum(mn, slab(hi_fill, s))
                mx = jnp.maximum(mx, slab(lo_fill, s))
            out.append((mn, mx))
        return tuple(out)

    init = tuple((jnp.full((rb, LANES), pos_inf, F32), jnp.full((rb, LANES), neg_inf, F32)) for _ in range(nrb))
    minmax = lax.fori_loop(0, n_chunks, score_body, init)

    def count_ge(r, cand):
        cand_b = jnp.broadcast_to(cand, (rb, LANES))

        def body(jc, acc):
            blk = sc_ref[jc, pl.ds(r * rb, rb), :]
            for s in range(nsl):
                acc = acc + jnp.where(slab(blk, s) >= cand_b, 1.0, 0.0)
            return acc

        acc = lax.fori_loop(0, n_chunks, body, jnp.zeros((rb, LANES), F32))
        return jnp.sum(acc, axis=-1, keepdims=True)

    def probe(r, cand):
        cand_b = jnp.broadcast_to(cand, (rb, LANES))

        def body(jc, carry):
            cnt, a, b = carry
            blk = sc_ref[jc, pl.ds(r * rb, rb), :]
            for s in range(nsl):
                x = slab(blk, s)
                ge = x >= cand_b
                cnt = cnt + jnp.where(ge, 1.0, 0.0)
                a = jnp.minimum(a, jnp.where(ge, x, pos_inf))
                b = jnp.maximum(b, jnp.where(ge, neg_inf, x))
            return cnt, a, b

        cnt, a, b = lax.fori_loop(0, n_chunks, body, (jnp.zeros((rb, LANES), F32),
                                                      jnp.full((rb, LANES), pos_inf, F32),
                                                      jnp.full((rb, LANES), neg_inf, F32)))
        return (jnp.sum(cnt, axis=-1, keepdims=True), jnp.min(a, axis=-1, keepdims=True),
                jnp.max(b, axis=-1, keepdims=True))

    def snap(r, lo, hi):
        lo_b = jnp.broadcast_to(lo, (rb, LANES))
        hi_b = jnp.broadcast_to(hi, (rb, LANES))

        def body(jc, carry):
            a, b = carry
            blk = sc_ref[jc, pl.ds(r * rb, rb), :]
            for s in range(nsl):
                x = slab(blk, s)
                a = jnp.minimum(a, jnp.where(x >= lo_b, x, pos_inf))
                b = jnp.maximum(b, jnp.where(x < hi_b, x, neg_inf))
            return a, b

        a, b = lax.fori_loop(0, n_chunks, body, (jnp.full((rb, LANES), pos_inf, F32),
                                                 jnp.full((rb, LANES), neg_inf, F32)))
        return jnp.min(a, axis=-1, keepdims=True), jnp.max(b, axis=-1, keepdims=True)

    for r in range(nrb):
        rows = pl.ds(r * rb, rb)
        rmin = jnp.min(minmax[r][0], axis=-1, keepdims=True)
        rmax = jnp.max(minmax[r][1], axis=-1, keepdims=True)
        n_valid = (i * tq + r * rb + 1 + lax.broadcasted_iota(I32, (rb, 1), 0)).astype(F32)
        small = n_valid <= kf

        def bis_body(_, st):
            lo, hi, c_lo, c_hi = st
            mid = lo + (hi - lo) * 0.5
            ok = (mid > lo) & (mid < hi)
            c = count_ge(r, mid)
            up = ok & (c >= kf)
            dn = ok & (c < kf)
            return (jnp.where(up, mid, lo), jnp.where(dn, mid, hi), jnp.where(up, c, c_lo), jnp.where(dn, c, c_hi))

        hi0 = rmax + jnp.maximum(jnp.abs(rmax) * 1e-6, 1e-30)
        lo, hi, c_lo, c_hi = lax.fori_loop(0, nbis, bis_body, (rmin, hi0, n_valid, jnp.zeros((rb, 1), F32)))

        lo, hi = snap(r, lo, hi)

        def active_of(lo, hi, c_lo):
            return jnp.logical_not(small) & (lo < hi) & (c_lo != kf)

        def snap_cond(st):
            return st[4] > 0.0

        def snap_body(st):
            lo, hi, c_lo, c_hi, _ = st
            act = active_of(lo, hi, c_lo)
            frac = jnp.clip((c_lo - kf + 0.5) / jnp.maximum(c_lo - c_hi, 1.0), 0.1, 0.9)
            mid = lo + (hi - lo) * frac
            mid = jnp.where((c_hi == kf - 1.0) | (mid <= lo) | (mid > hi), hi, mid)
            c, a, b = probe(r, mid)
            up = act & (c >= kf)
            dn = act & (c < kf)
            lo, c_lo = jnp.where(up, a, lo), jnp.where(up, c, c_lo)
            hi, c_hi = jnp.where(dn, b, hi), jnp.where(dn, c, c_hi)
            return lo, hi, c_lo, c_hi, jnp.max(jnp.where(active_of(lo, hi, c_lo), 1.0, 0.0))

        flag0 = jnp.max(jnp.where(active_of(lo, hi, c_lo), 1.0, 0.0))
        lo, hi, c_lo, c_hi, _ = lax.while_loop(snap_cond, snap_body, (lo, hi, c_lo, c_hi, flag0))

        tau = jnp.where(small, rmin, lo)
        excess = jnp.logical_not(small) & (lo == hi) & (c_lo > kf)
        need = jnp.where(excess, kf - c_hi, 4.0 * ck * nsl * 65536.0)

        @pl.when(jnp.max(jnp.where(excess, 1.0, 0.0)) > 0.0)
        def _():
            def tie_body(jc, carry):
                blk = sc_ref[jc, rows, :]
                eq = blk == tau
                pc = _dot(jnp.where(eq, 1.0, 0.0).astype(BF16), tri_ref[...]) + carry
                sc_ref[jc, rows, :] = jnp.where(eq & (pc > need), neg_inf, blk)
                return pc[:, ck - 1:ck]

            lax.fori_loop(0, n_chunks, tie_body, jnp.zeros((rb, 1), F32))

        tau_ref[rows, :] = jnp.broadcast_to(tau, (rb, LANES))

    bmax = jnp.zeros((1, 1), F32)
    for h in range(A_HEADS):
        qf = q_ref[0, h].astype(F32)
        bound = jnp.sqrt(jnp.sum(qf * qf, axis=-1, keepdims=True) * kn_ref[h, 0:1, 0:1])
        m_ref[h] = jnp.broadcast_to(bound, (tq, LANES))
        bmax = jnp.maximum(bmax, jnp.max(bound, axis=0, keepdims=True))

    def selected(jc):
        return sc_ref[jc] >= pltpu.repeat(tau_ref[...], nsl, axis=1)

    @pl.when(bmax[0, 0] > LOGIT_BOUND_LIMIT)
    def _():
        def max_body(jc, mx):
            sel = selected(jc)
            out = []
            for h in range(A_HEADS):
                s = jnp.where(sel, _dot(q_ref[0, h], kt_ref[0, jc, h]), neg_inf)
                m = mx[h]
                for sl in range(nsl):
                    m = jnp.maximum(m, slab(s, sl))
                out.append(m)
            return tuple(out)

        mx = lax.fori_loop(0, n_chunks, max_body, tuple(jnp.full((tq, LANES), neg_inf, F32) for _ in range(A_HEADS)))
        for h in range(A_HEADS):
            m_ref[h] = jnp.broadcast_to(jnp.max(mx[h], axis=-1, keepdims=True), (tq, LANES))

    lane_k = lax.broadcasted_iota(I32, (ck, LANES), 1)
    low_k = lane_k < A_HEAD_DIM
    one_even = jnp.where(lane_k == 0, 1.0, 0.0).astype(BF16)
    one_odd = jnp.where(lane_k == 1, 1.0, 0.0).astype(BF16)
    acc_ref[...] = jnp.zeros(acc_ref.shape, F32)

    def att_body(jc, carry):
        sel = selected(jc)
        krows = pl.ds(pl.multiple_of(jc * ck, ck), ck)
        for hp in range(A_HEADS // 2):
            v2 = v_ref[0, krows, hp * LANES:(hp + 1) * LANES]
            zero = jnp.zeros_like(v2)
            rhs = jnp.concatenate([
                jnp.concatenate([jnp.where(low_k, v2, zero), one_even], axis=1),
                jnp.concatenate([jnp.where(low_k, zero, v2), one_odd], axis=1)], axis=0)
            ps = []
            for e in range(2):
                h = 2 * hp + e
                s = _dot(q_ref[0, h], kt_ref[0, jc, h])
                p = jnp.where(sel, jnp.exp2(s - pltpu.repeat(m_ref[h], nsl, axis=1)), 0.0)
                ps.append(p.astype(BF16))
            acc_ref[hp] += _dot(jnp.concatenate(ps, axis=1), rhs)
        return carry

    lax.fori_loop(0, n_chunks, att_body, 0)
    low_q = lax.broadcasted_iota(I32, (tq, LANES), 1) < A_HEAD_DIM
    for hp in range(A_HEADS // 2):
        a = acc_ref[hp]
        inv = jnp.where(low_q, 1.0 / a[:, LANES:LANES + 1], 1.0 / a[:, LANES + 1:LANES + 2])
        o_ref[0, :, hp * LANES:(hp + 1) * LANES] = (a[:, :LANES] * inv).astype(o_ref.dtype)


def _dsa(qi, w, kti, q, kt, v, tri, tq, ck, topk):
    bsz, hi, seq, di = qi.shape
    nc = seq // ck
    kern = functools.partial(_dsa_kernel, tq=tq, ck=ck, topk=topk, rb=min(tq, 128), nbis=BISECT_STEPS)
    return pl.pallas_call(
        kern,
        grid=(bsz, seq // tq),
        in_specs=[
            pl.BlockSpec((1, hi, tq, di), lambda b, i: (b, 0, i, 0)),
            pl.BlockSpec((1, tq, LANES), lambda b, i: (b, i, 0)),
            pl.BlockSpec((1, nc, di, ck), lambda b, i: (b, 0, 0, 0)),
            pl.BlockSpec((1, A_HEADS, tq, A_HEAD_DIM), lambda b, i: (b, 0, i, 0)),
            pl.BlockSpec((1, nc, A_HEADS, A_HEAD_DIM, ck), lambda b, i: (b, 0, 0, 0, 0)),
            pl.BlockSpec((1, seq, A_WIDTH), lambda b, i: (b, 0, 0)),
            pl.BlockSpec((ck, ck), lambda b, i: (0, 0)),
        ],
        out_specs=pl.BlockSpec((1, tq, A_WIDTH), lambda b, i: (b, i, 0)),
        out_shape=jax.ShapeDtypeStruct((bsz, seq, A_WIDTH), BF16),
        scratch_shapes=[
            pltpu.VMEM((nc, tq, ck), F32),
            pltpu.VMEM((A_HEADS // 2, tq, 2 * LANES), F32),
            pltpu.VMEM((A_HEADS, tq, LANES), F32),
            pltpu.VMEM((tq, LANES), F32),
            pltpu.VMEM((A_HEADS, 8, LANES), F32),
        ],
        compiler_params=pltpu.CompilerParams(
            dimension_semantics=("arbitrary", "arbitrary"), vmem_limit_bytes=VMEM_LIMIT_BYTES),
        name="dsa",
    )(qi, w, kti, q, kt, v, tri)


def _hgrn_kernel(bq_ref, bf_ref, bi_ref, bg_ref, lb_ref, go_ref, o_ref, st_ref, oi_ref, *, tl, layer):
    cs = HGRN_CHUNK
    nch = tl // cs

    @pl.when(pl.program_id(1) == 0)
    def _():
        st_ref[...] = jnp.zeros(st_ref.shape, F32)

    lbr = lb_ref[...]
    slots = [lbr[k:k + 1] for k in range(lbr.shape[0])]
    mx = functools.reduce(jnp.maximum, slots)
    es = [jnp.exp(s - mx) for s in slots]
    lb = functools.reduce(jnp.add, es[:layer + 1]) / functools.reduce(jnp.add, es)

    bq = bq_ref[...]
    q = bq * _sigmoid(bq)
    f = lb + (1.0 - lb) * _sigmoid(bf_ref[...])
    kk = 1.0 - f
    g = jnp.log(f)
    v = bi_ref[...]

    r_i = lax.broadcasted_iota(I32, (tl, tl), 0)
    c_i = lax.broadcasted_iota(I32, (tl, tl), 1)
    same = (r_i >> _log2(cs)) == (c_i >> _log2(cs))
    tri = jnp.where(same & (c_i <= r_i), 1.0, 0.0).astype(BF16)
    blk = jnp.where(same, 1.0, 0.0).astype(BF16)
    g1, g2, g3 = _split3(g)
    b = _dot(tri, g1) + _dot(tri, g2) + _dot(tri, g3)
    bend = _dot(blk, g1) + _dot(blk, g2) + _dot(blk, g3)

    pos = lax.broadcasted_iota(I32, (tl, B_WIDTH), 0) & (cs - 1)
    o = jnp.zeros((tl, B_WIDTH), F32)
    for r in range(cs):
        if r == 0:
            kk_s, b_s, v_s = kk, b, v
        else:
            kk_s = pltpu.roll(kk, r, axis=0)
            b_s = pltpu.roll(b, r, axis=0)
            v_s = pltpu.roll(v, r, axis=0)
        e = q * kk_s * jnp.exp(jnp.where(pos >= r, b - b_s, -jnp.inf))
        parts = []
        for h in range(B_HEADS):
            sl = slice(h * B_HEAD_DIM, (h + 1) * B_HEAD_DIM)
            parts.append(jnp.sum(e[:, sl], axis=-1, keepdims=True) * v_s[:, sl])
        o = o + jnp.concatenate(parts, axis=1)

    qe = (q * jnp.exp(b)).astype(BF16)
    kd = (kk * jnp.exp(bend - b)).astype(BF16)
    vb = v.astype(BF16)
    dec = jnp.exp(bend)
    for c in range(nch):
        rs = slice(c * cs, (c + 1) * cs)
        for h in range(B_HEADS):
            sl = slice(h * B_HEAD_DIM, (h + 1) * B_HEAD_DIM)
            st = st_ref[h]
            oi_ref[rs, sl] = lax.dot_general(qe[rs, sl], st.astype(BF16), (((1,), (1,)), ((), ())),
                                             preferred_element_type=F32)
            upd = lax.dot_general(vb[rs, sl], kd[rs, sl], (((0,), (0,)), ((), ())),
                                  preferred_element_type=F32)
            st_ref[h] = st * dec[c * cs:c * cs + 1, sl] + upd
    o = o + oi_ref[...]

    parts = []
    for h in range(B_HEADS):
        sl = slice(h * B_HEAD_DIM, (h + 1) * B_HEAD_DIM)
        oh = o[:, sl]
        parts.append(oh * lax.rsqrt(jnp.mean(oh * oh, axis=-1, keepdims=True) + EPS))
    on = jnp.concatenate(parts, axis=1) * go_ref[...]
    bg = bg_ref[...]
    o_ref[...] = (on * (bg * _sigmoid(bg))).astype(o_ref.dtype)


def _hgrn(proj, lb_table, go, bsz, seq, tl, layer):
    m = proj.shape[0]
    tpb = seq // tl
    col = lambda c: (lambda b, t: (b * tpb + t, c // B_WIDTH))
    return pl.pallas_call(
        functools.partial(_hgrn_kernel, tl=tl, layer=layer),
        grid=(bsz, tpb),
        in_specs=[
            pl.BlockSpec((tl, B_WIDTH), col(COL_BQ)),
            pl.BlockSpec((tl, B_WIDTH), col(COL_BF)),
            pl.BlockSpec((tl, B_WIDTH), col(COL_BI)),
            pl.BlockSpec((tl, B_WIDTH), col(COL_BG)),
            pl.BlockSpec(lb_table.shape, lambda b, t: (0, 0)),
            pl.BlockSpec((1, B_WIDTH), lambda b, t: (0, 0)),
        ],
        out_specs=pl.BlockSpec((tl, B_WIDTH), lambda b, t: (b * tpb + t, 0)),
        out_shape=jax.ShapeDtypeStruct((m, B_WIDTH), BF16),
        scratch_shapes=[
            pltpu.VMEM((B_HEADS, B_HEAD_DIM, B_HEAD_DIM), F32),
            pltpu.VMEM((tl, B_WIDTH), F32),
        ],
        compiler_params=pltpu.CompilerParams(
            dimension_semantics=("arbitrary", "arbitrary"), vmem_limit_bytes=VMEM_LIMIT_BYTES),
        name="hgrn",
    )(proj, proj, proj, proj, lb_table, go)


def _merge_kernel(oa_ref, ob_ref, ga_ref, gb_ref, x_ref, g1_ref, wa_ref, wb_ref, wo_ref, o_ref):
    pa = _dot(oa_ref[...], wa_ref[...])
    pb = _dot(ob_ref[...], wb_ref[...])
    merged = _sigmoid(ga_ref[...]) * pa + _sigmoid(gb_ref[...]) * pb
    y = _dot(merged.astype(BF16), wo_ref[...])
    o_ref[...] = x_ref[...] + g1_ref[0] * y


def _merge(oa, ob, proj, x2, gate1, wa, wb, wo, seq, tm):
    m, d = x2.shape
    tpb = seq // tm
    row = lambda i: (i, 0)
    const = lambda i: (0, 0)
    return pl.pallas_call(
        _merge_kernel,
        grid=(m // tm,),
        in_specs=[
            pl.BlockSpec((tm, A_WIDTH), row),
            pl.BlockSpec((tm, B_WIDTH), row),
            pl.BlockSpec((tm, d), lambda i: (i, COL_GA // d)),
            pl.BlockSpec((tm, d), lambda i: (i, COL_GB // d)),
            pl.BlockSpec((tm, d), row),
            pl.BlockSpec((1, 1, d), lambda i: (i // tpb, 0, 0)),
            pl.BlockSpec(wa.shape, const),
            pl.BlockSpec(wb.shape, const),
            pl.BlockSpec(wo.shape, const),
        ],
        out_specs=pl.BlockSpec((tm, d), row),
        out_shape=jax.ShapeDtypeStruct((m, d), F32),
        compiler_params=pltpu.CompilerParams(vmem_limit_bytes=VMEM_LIMIT_BYTES),
        name="merge",
    )(oa, ob, proj, proj, x2, gate1, wa, wb, wo)


def _mlp_kernel(x_ref, g_ref, sc_ref, sh_ref, g2_ref, w1_ref, b1_ref, w2_ref, b2_ref, o_ref, *, tf):
    x = x_ref[...]
    ms = jnp.mean(x * x, axis=-1, keepdims=True)
    h = x * lax.rsqrt(ms + EPS) * g_ref[...]
    h = (h * (1.0 + sc_ref[0]) + sh_ref[0]).astype(BF16)
    dff = w1_ref.shape[1]
    y = jnp.zeros(x.shape, F32)
    for c in range(dff // tf):
        cs = slice(c * tf, (c + 1) * tf)
        a = jnp.maximum(_dot(h, w1_ref[:, cs]) + b1_ref[:, cs], 0.0)
        y = y + _dot((a * a).astype(BF16), w2_ref[cs, :])
    o_ref[...] = x + g2_ref[0] * (y + b2_ref[...])


def _mlp(x1, g, scale, shift, gate2, w1, b1, w2, b2, seq, tm, tf):
    m, d = x1.shape
    tpb = seq // tm
    row = lambda i: (i, 0)
    const = lambda i: (0, 0)
    bat = lambda i: (i // tpb, 0, 0)
    return pl.pallas_call(
        functools.partial(_mlp_kernel, tf=tf),
        grid=(m // tm,),
        in_specs=[
            pl.BlockSpec((tm, d), row),
            pl.BlockSpec((1, d), const),
            pl.BlockSpec((1, 1, d), bat),
            pl.BlockSpec((1, 1, d), bat),
            pl.BlockSpec((1, 1, d), bat),
            pl.BlockSpec(w1.shape, const),
            pl.BlockSpec(b1.shape, const),
            pl.BlockSpec(w2.shape, const),
            pl.BlockSpec(b2.shape, const),
        ],
        out_specs=pl.BlockSpec((tm, d), row),
        out_shape=jax.ShapeDtypeStruct((m, d), F32),
        compiler_params=pltpu.CompilerParams(vmem_limit_bytes=VMEM_LIMIT_BYTES),
        name="mlp",
    )(x1, g, scale, shift, gate2, w1, b1, w2, b2)


def _regroup_w_in(w):
    d = w.shape[0]
    sizes = (Q_LORA_RANK, A_WIDTH, A_WIDTH, IDX_DIM, IDX_HEADS, B_WIDTH, B_WIDTH, B_WIDTH, B_WIDTH, d, d)
    offs = [0]
    for s in sizes:
        offs.append(offs[-1] + s)
    part = lambda k: w[:, offs[k]:offs[k + 1]]
    q_lat, a_k, a_v, k_idx, w_idx, b_q, b_f, b_i, b_g, gate_a, gate_b = (part(k) for k in range(11))
    pad = jnp.zeros((d, LANES - IDX_DIM - IDX_HEADS), w.dtype)
    out = jnp.concatenate([a_k, a_v, b_q, b_f, b_i, b_g, gate_a, gate_b, q_lat, k_idx, w_idx, pad], axis=1)
    assert out.shape[1] == PROJ_COLS
    return out.astype(BF16)


def _layer(x, mod, l, p, tiles):
    bsz, seq, d = x.shape
    m = bsz * seq
    shift1, scale1, gate1, shift2, scale2, gate2 = (mod[:, k * d:(k + 1) * d].reshape(bsz, 1, d) for k in range(6))
    x2 = x.reshape(m, d)

    proj = _inproj(x2, p['norm1_g'][l][None], scale1, shift1, _regroup_w_in(p['w_in'][l]), seq, tiles['tm_in'])

    wuq = jnp.concatenate([p['w_uq'][l], p['w_uq_idx'][l]], axis=1).astype(BF16)
    gki = jnp.concatenate([p['k_idx_norm_g'][l], jnp.zeros((LANES - IDX_DIM,), F32)])[None]
    aq, ak, qi, ki, wi = _qkprep(
        proj, p['q_lat_norm_g'][l][None], wuq,
        jnp.tile(p['q_norm_g'][l], A_HEADS)[None], jnp.tile(p['k_norm_g'][l], A_HEADS)[None], gki, tiles['tm'])

    tq, ck = tiles['tq'], tiles['ck']
    nc = seq // ck
    topk = min(TOPK_MAX, seq // 4)
    qi4 = qi.reshape(bsz, seq, IDX_HEADS, IDX_DIM).transpose(0, 2, 1, 3)
    kti = ki.reshape(bsz, nc, ck, IDX_DIM).transpose(0, 1, 3, 2)
    q4 = aq.reshape(bsz, seq, A_HEADS, A_HEAD_DIM).transpose(0, 2, 1, 3)
    kt = ak.reshape(bsz, nc, ck, A_HEADS, A_HEAD_DIM).transpose(0, 1, 3, 4, 2)
    av = lax.slice_in_dim(proj, COL_AV, COL_AV + A_WIDTH, axis=1).astype(BF16).reshape(bsz, seq, A_WIDTH)
    tri = jnp.triu(jnp.ones((ck, ck), BF16))
    out_a = _dsa(qi4, wi.reshape(bsz, seq, LANES), kti, q4, kt, av, tri, tq, ck, topk).reshape(m, A_WIDTH)

    out_b = _hgrn(proj, p['hgrn_lb'], p['hgrn_o_norm_g'][l][None], bsz, seq, tiles['tl'], l)

    x1 = _merge(out_a, out_b, proj, x2, gate1, p['w_proj_a'][l].astype(BF16), p['w_proj_b'][l].astype(BF16),
                p['w_out'][l].astype(BF16), seq, tiles['tm'])
    out = _mlp(x1, p['norm2_g'][l][None], scale2, shift2, gate2, p['w_mlp1'][l].astype(BF16), p['b_mlp1'][l][None],
               p['w_mlp2'][l].astype(BF16), p['b_mlp2'][l][None], seq, tiles['tm'], tiles['tf'])
    return out.reshape(bsz, seq, d)


def _tiles(seq):
    pick = lambda want: min(want, seq)
    return dict(tm_in=pick(256), tm=pick(512), tq=pick(256), ck=pick(512), tl=pick(256), tf=1024)


def kernel(x, c, w_ada, b_ada, norm1_g, w_in, q_lat_norm_g, w_uq, w_uq_idx, q_norm_g, k_norm_g, k_idx_norm_g,
           hgrn_lb, hgrn_o_norm_g, w_proj_a, w_proj_b, w_out, norm2_g, w_mlp1, b_mlp1, w_mlp2, b_mlp2):
    p = dict(norm1_g=norm1_g, w_in=w_in, q_lat_norm_g=q_lat_norm_g, w_uq=w_uq, w_uq_idx=w_uq_idx,
             q_norm_g=q_norm_g, k_norm_g=k_norm_g, k_idx_norm_g=k_idx_norm_g, hgrn_lb=hgrn_lb,
             hgrn_o_norm_g=hgrn_o_norm_g, w_proj_a=w_proj_a, w_proj_b=w_proj_b, w_out=w_out, norm2_g=norm2_g,
             w_mlp1=w_mlp1, b_mlp1=b_mlp1, w_mlp2=w_mlp2, b_mlp2=b_mlp2)
    bsz, seq, d = x.shape
    depth = w_ada.shape[0]
    tiles = _tiles(seq)
    c_pad = jnp.zeros((8, d), F32).at[:bsz].set(c)
    for l in range(depth):
        mod = _adaln(c_pad, w_ada[l], b_ada[l][None])[:bsz]
        x = _layer(x, mod, l, p, tiles)
    return x
```

```python
import functools

import jax
import jax.numpy as jnp
from jax import lax
from jax.experimental import pallas as pl
from jax.experimental.pallas import tpu as pltpu

F32 = jnp.float32
BF16 = jnp.bfloat16
I32 = jnp.int32

EPS = 1e-6
A_HEADS = 8
A_HEAD_DIM = 64
A_WIDTH = A_HEADS * A_HEAD_DIM
Q_LORA_RANK = 256
IDX_HEADS = 4
IDX_DIM = 64
TOPK_MAX = 256
B_WIDTH = 512
B_HEADS = 4
B_HEAD_DIM = 128
HGRN_CHUNK = 16

LANES = 128
VMEM_LIMIT_BYTES = 56 * 1024 * 1024

COL_AK, COL_AV, COL_BQ, COL_BF, COL_BI, COL_BG = 0, 512, 1024, 1536, 2048, 2560
COL_GA, COL_GB, COL_QL, COL_KW = 3072, 4096, 5120, 5376
PROJ_COLS = 5504

LOG2E = 1.4426950408889634
BISECT_STEPS = 15
SELECT_ACC_ROWS = 32
LOGIT_BOUND_LIMIT = 60.0


def _sigmoid(x):
    return 1.0 / (1.0 + jnp.exp(-x))


def _split2(x):
    hi = x.astype(BF16)
    lo = (x - hi.astype(F32)).astype(BF16)
    return hi, lo


def _split3(x):
    a = x.astype(BF16)
    r = x - a.astype(F32)
    b = r.astype(BF16)
    c = (r - b.astype(F32)).astype(BF16)
    return a, b, c


def _dot(a, b):
    return jnp.dot(a, b, preferred_element_type=F32)


def _log2(n):
    assert n > 0 and n & (n - 1) == 0, n
    return n.bit_length() - 1


def _group_ones(n, group):
    r = lax.broadcasted_iota(I32, (n, n), 0) >> _log2(group)
    c = lax.broadcasted_iota(I32, (n, n), 1) >> _log2(group)
    return jnp.where(r == c, 1.0, 0.0).astype(BF16)


def _group_mean_sq(x, ones_bd, group):
    hi, lo = _split2(x * x)
    return (_dot(hi, ones_bd) + _dot(lo, ones_bd)) * (1.0 / group)


def _adaln_kernel(c_ref, w_ref, b_ref, o_ref):
    c = c_ref[...]
    a = c * _sigmoid(c)
    a1, a2, a3 = _split3(a)
    w1, w2, w3 = _split3(w_ref[...])
    acc = _dot(a1, w1) + (_dot(a1, w2) + _dot(a2, w1)) + (_dot(a2, w2) + _dot(a1, w3) + _dot(a3, w1))
    o_ref[...] = acc + b_ref[...]


def _adaln(c_pad, w, b):
    rows, d = c_pad.shape
    n = w.shape[1]
    tn = 1536
    return pl.pallas_call(
        _adaln_kernel,
        grid=(n // tn,),
        in_specs=[
            pl.BlockSpec((rows, d), lambda j: (0, 0)),
            pl.BlockSpec((d, tn), lambda j: (0, j)),
            pl.BlockSpec((1, tn), lambda j: (0, j)),
        ],
        out_specs=pl.BlockSpec((rows, tn), lambda j: (0, j)),
        out_shape=jax.ShapeDtypeStruct((rows, n), F32),
        compiler_params=pltpu.CompilerParams(vmem_limit_bytes=VMEM_LIMIT_BYTES),
        name="adaln",
    )(c_pad, w, b)


def _inproj_kernel(x_ref, g_ref, sc_ref, sh_ref, w_ref, o_ref):
    x = x_ref[...]
    ms = jnp.mean(x * x, axis=-1, keepdims=True)
    h = x * lax.rsqrt(ms + EPS) * g_ref[...]
    h = h * (1.0 + sc_ref[0]) + sh_ref[0]
    o_ref[...] = _dot(h.astype(BF16), w_ref[...])


def _inproj(x2, g, scale, shift, w, seq, tm):
    m, d = x2.shape
    n = w.shape[1]
    tpb = seq // tm
    return pl.pallas_call(
        _inproj_kernel,
        grid=(m // tm,),
        in_specs=[
            pl.BlockSpec((tm, d), lambda i: (i, 0)),
            pl.BlockSpec((1, d), lambda i: (0, 0)),
            pl.BlockSpec((1, 1, d), lambda i: (i // tpb, 0, 0)),
            pl.BlockSpec((1, 1, d), lambda i: (i // tpb, 0, 0)),
            pl.BlockSpec((d, n), lambda i: (0, 0)),
        ],
        out_specs=pl.BlockSpec((tm, n), lambda i: (i, 0)),
        out_shape=jax.ShapeDtypeStruct((m, n), F32),
        compiler_params=pltpu.CompilerParams(vmem_limit_bytes=VMEM_LIMIT_BYTES),
        name="inproj",
    )(x2, g, scale, shift, w)


def _qkprep_kernel(ak_ref, ql_ref, kw_ref, gql_ref, wuq_ref, gq_ref, gk_ref, gki_ref,
                   aq_out, ak_out, qi_out, ki_out, w_out):
    ones64 = _group_ones(A_WIDTH, A_HEAD_DIM)
    ql = ql_ref[...]
    ql = ql * lax.rsqrt(jnp.mean(ql * ql, axis=-1, keepdims=True) + EPS) * gql_ref[...]
    up = _dot(ql.astype(BF16), wuq_ref[...])
    aq = up[:, :A_WIDTH]
    aq = aq * lax.rsqrt(_group_mean_sq(aq, ones64, A_HEAD_DIM) + EPS) * gq_ref[...]
    aq_out[...] = (aq * (A_HEAD_DIM ** -0.5 * LOG2E)).astype(BF16)
    qi_out[...] = up[:, A_WIDTH:]
    ak = ak_ref[...]
    ak = ak * lax.rsqrt(_group_mean_sq(ak, ones64, A_HEAD_DIM) + EPS) * gk_ref[...]
    ak_out[...] = ak.astype(BF16)
    kw = kw_ref[...]
    lane = lax.broadcasted_iota(I32, kw.shape, 1)
    ksq = jnp.where(lane < IDX_DIM, kw * kw, 0.0)
    kms = jnp.sum(ksq, axis=-1, keepdims=True) * (1.0 / IDX_DIM)
    kn = kw * lax.rsqrt(kms + EPS) * gki_ref[...]
    ki_out[...] = kn[:, :IDX_DIM].astype(BF16)
    idx_scale = (IDX_DIM ** -0.5) * (IDX_HEADS ** -0.5)
    w_out[...] = jnp.where((lane >= IDX_DIM) & (lane < IDX_DIM + IDX_HEADS), kw * idx_scale, 0.0)


def _qkprep(proj, gql, wuq, gq, gk, gki, tm):
    m = proj.shape[0]
    row = lambda i: (i, 0)
    const = lambda i: (0, 0)
    return pl.pallas_call(
        _qkprep_kernel,
        grid=(m // tm,),
        in_specs=[
            pl.BlockSpec((tm, A_WIDTH), lambda i: (i, COL_AK // A_WIDTH)),
            pl.BlockSpec((tm, Q_LORA_RANK), lambda i: (i, COL_QL // Q_LORA_RANK)),
            pl.BlockSpec((tm, LANES), lambda i: (i, COL_KW // LANES)),
            pl.BlockSpec((1, Q_LORA_RANK), const),
            pl.BlockSpec(wuq.shape, const),
            pl.BlockSpec((1, A_WIDTH), const),
            pl.BlockSpec((1, A_WIDTH), const),
            pl.BlockSpec((1, LANES), const),
        ],
        out_specs=[
            pl.BlockSpec((tm, A_WIDTH), row),
            pl.BlockSpec((tm, A_WIDTH), row),
            pl.BlockSpec((tm, IDX_HEADS * IDX_DIM), row),
            pl.BlockSpec((tm, IDX_DIM), row),
            pl.BlockSpec((tm, LANES), row),
        ],
        out_shape=[
            jax.ShapeDtypeStruct((m, A_WIDTH), BF16),
            jax.ShapeDtypeStruct((m, A_WIDTH), BF16),
            jax.ShapeDtypeStruct((m, IDX_HEADS * IDX_DIM), F32),
            jax.ShapeDtypeStruct((m, IDX_DIM), BF16),
            jax.ShapeDtypeStruct((m, LANES), F32),
        ],
        compiler_params=pltpu.CompilerParams(vmem_limit_bytes=VMEM_LIMIT_BYTES),
        name="qkprep",
    )(proj, proj, proj, gql, wuq, gq, gk, gki)


def _dsa_kernel(qi_ref, w_ref, kti_ref, q_ref, kt_ref, v_ref, tri_ref, o_ref,
                sc_ref, acc_ref, m_ref, tau_ref, kn_ref, *, tq, ck, topk, rb, nbis):
    i = pl.program_id(1)
    n_chunks = ((i + 1) * tq + ck - 1) // ck
    nrb = tq // rb
    nsl = ck // LANES
    kf = float(topk)
    neg_inf = float("-inf")
    pos_inf = float("inf")
    slab = lambda x, s: x[:, s * LANES:(s + 1) * LANES]

    @pl.when(i == 0)
    def _():
        def kn_body(jc, mx):
            k = kt_ref[0, jc].astype(F32)
            return jnp.maximum(mx, jnp.sum(k * k, axis=1, keepdims=True))

        mx = lax.fori_loop(0, kt_ref.shape[1], kn_body, jnp.zeros((A_HEADS, 1, ck), F32))
        kn_ref[...] = jnp.broadcast_to(jnp.max(mx, axis=2, keepdims=True), kn_ref.shape)

    def score_body(jc, carry):
        kc = kti_ref[0, jc].astype(BF16)
        s_pos = jc * ck + lax.broadcasted_iota(I32, (rb, ck), 1)
        out = []
        for r in range(nrb):
            mn, mx = carry[r]
            rows = pl.ds(r * rb, rb)
            wv = w_ref[0, rows, :]
            score = jnp.zeros((rb, ck), F32)
            for h in range(IDX_HEADS):
                s = _dot(qi_ref[0, h, rows, :].astype(BF16), kc)
                score = score + wv[:, IDX_DIM + h:IDX_DIM + h + 1] * jnp.maximum(s, 0.0)
            t_pos = i * tq + r * rb + lax.broadcasted_iota(I32, (rb, ck), 0)
            causal = s_pos <= t_pos
            sc_ref[jc, rows, :] = jnp.where(causal, score, neg_inf)
            hi_fill = jnp.where(causal, score, pos_inf)
            lo_fill = jnp.where(causal, score, neg_inf)
            for s in range(nsl):
                mn = jnp.minimum(mn, slab(hi_fill, s))
                mx = jnp.maximum(mx, slab(lo_fill, s))
            out.append((mn, mx))
        return tuple(out)

    init = tuple((jnp.full((rb, LANES), pos_inf, F32), jnp.full((rb, LANES), neg_inf, F32)) for _ in range(nrb))
    minmax = lax.fori_loop(0, n_chunks, score_body, init)

    def count_ge(r, cand):
        cand_b = jnp.broadcast_to(cand, (rb, LANES))

        def body(jc, acc):
            blk = sc_ref[jc, pl.ds(r * rb, rb), :]
            for s in range(nsl):
                acc = acc + jnp.where(slab(blk, s) >= cand_b, 1.0, 0.0)
            return acc

        acc = lax.fori_loop(0, n_chunks, body, jnp.zeros((rb, LANES), F32))
        return jnp.sum(acc, axis=-1, keepdims=True)

    def probe(r, cand):
        cand_b = jnp.broadcast_to(cand, (rb, LANES))

        def body(jc, carry):
            cnt, a, b = carry
            blk = sc_ref[jc, pl.ds(r * rb, rb), :]
            for s in range(nsl):
                x = slab(blk, s)
                ge = x >= cand_b
                cnt = cnt + jnp.where(ge, 1.0, 0.0)
                a = jnp.minimum(a, jnp.where(ge, x, pos_inf))
                b = jnp.maximum(b, jnp.where(ge, neg_inf, x))
            return cnt, a, b

        cnt, a, b = lax.fori_loop(0, n_chunks, body, (jnp.zeros((rb, LANES), F32),
                                                      jnp.full((rb, LANES), pos_inf, F32),
                                                      jnp.full((rb, LANES), neg_inf, F32)))
        return (jnp.sum(cnt, axis=-1, keepdims=True), jnp.min(a, axis=-1, keepdims=True),
                jnp.max(b, axis=-1, keepdims=True))

    def snap(r, lo, hi):
        lo_b = jnp.broadcast_to(lo, (rb, LANES))
        hi_b = jnp.broadcast_to(hi, (rb, LANES))

        def body(jc, carry):
            a, b = carry
            blk = sc_ref[jc, pl.ds(r * rb, rb), :]
            for s in range(nsl):
                x = slab(blk, s)
                a = jnp.minimum(a, jnp.where(x >= lo_b, x, pos_inf))
                b = jnp.maximum(b, jnp.where(x < hi_b, x, neg_inf))
            return a, b

        a, b = lax.fori_loop(0, n_chunks, body, (jnp.full((rb, LANES), pos_inf, F32),
                                                 jnp.full((rb, LANES), neg_inf, F32)))
        return jnp.min(a, axis=-1, keepdims=True), jnp.max(b, axis=-1, keepdims=True)

    for r in range(nrb):
        rows = pl.ds(r * rb, rb)
        rmin = jnp.min(minmax[r][0], axis=-1, keepdims=True)
        rmax = jnp.max(minmax[r][1], axis=-1, keepdims=True)
        n_valid = (i * tq + r * rb + 1 + lax.broadcasted_iota(I32, (rb, 1), 0)).astype(F32)
        small = n_valid <= kf

        def bis_body(_, st):
            lo, hi, c_lo, c_hi = st
            mid = lo + (hi - lo) * 0.5
            ok = (mid > lo) & (mid < hi)
            c = count_ge(r, mid)
            up = ok & (c >= kf)
            dn = ok & (c < kf)
            return (jnp.where(up, mid, lo), jnp.where(dn, mid, hi), jnp.where(up, c, c_lo), jnp.where(dn, c, c_hi))

        hi0 = rmax + jnp.maximum(jnp.abs(rmax) * 1e-6, 1e-30)
        lo, hi, c_lo, c_hi = lax.fori_loop(0, nbis, bis_body, (rmin, hi0, n_valid, jnp.zeros((rb, 1), F32)))

        lo, hi = snap(r, lo, hi)

        def active_of(lo, hi, c_lo):
            return jnp.logical_not(small) & (lo < hi) & (c_lo != kf)

        def snap_cond(st):
            return st[4] > 0.0

        def snap_body(st):
            lo, hi, c_lo, c_hi, _ = st
            act = active_of(lo, hi, c_lo)
            frac = jnp.clip((c_lo - kf + 0.5) / jnp.maximum(c_lo - c_hi, 1.0), 0.1, 0.9)
            mid = lo + (hi - lo) * frac
            mid = jnp.where((c_hi == kf - 1.0) | (mid <= lo) | (mid > hi), hi, mid)
            c, a, b = probe(r, mid)
            up = act & (c >= kf)
            dn = act & (c < kf)
            lo, c_lo = jnp.where(up, a, lo), jnp.where(up, c, c_lo)
            hi, c_hi = jnp.where(dn, b, hi), jnp.where(dn, c, c_hi)
            return lo, hi, c_lo, c_hi, jnp.max(jnp.where(active_of(lo, hi, c_lo), 1.0, 0.0))

        flag0 = jnp.max(jnp.where(active_of(lo, hi, c_lo), 1.0, 0.0))
        lo, hi, c_lo, c_hi, _ = lax.while_loop(snap_cond, snap_body, (lo, hi, c_lo, c_hi, flag0))

        tau = jnp.where(small, rmin, lo)
        excess = jnp.logical_not(small) & (lo == hi) & (c_lo > kf)
        need = jnp.where(excess, kf - c_hi, 4.0 * ck * nsl * 65536.0)

        @pl.when(jnp.max(jnp.where(excess, 1.0, 0.0)) > 0.0)
        def _():
            def tie_body(jc, carry):
                blk = sc_ref[jc, rows, :]
                eq = blk == tau
                pc = _dot(jnp.where(eq, 1.0, 0.0).astype(BF16), tri_ref[...]) + carry
                sc_ref[jc, rows, :] = jnp.where(eq & (pc > need), neg_inf, blk)
                return pc[:, ck - 1:ck]

            lax.fori_loop(0, n_chunks, tie_body, jnp.zeros((rb, 1), F32))

        tau_ref[rows, :] = jnp.broadcast_to(tau, (rb, LANES))

    bmax = jnp.zeros((1, 1), F32)
    for h in range(A_HEADS):
        qf = q_ref[0, h].astype(F32)
        bound = jnp.sqrt(jnp.sum(qf * qf, axis=-1, keepdims=True) * kn_ref[h, 0:1, 0:1])
        m_ref[h] = jnp.broadcast_to(bound, (tq, LANES))
        bmax = jnp.maximum(bmax, jnp.max(bound, axis=0, keepdims=True))

    def selected(jc):
        return sc_ref[jc] >= pltpu.repeat(tau_ref[...], nsl, axis=1)

    @pl.when(bmax[0, 0] > LOGIT_BOUND_LIMIT)
    def _():
        def max_body(jc, mx):
            sel = selected(jc)
            out = []
            for h in range(A_HEADS):
                s = jnp.where(sel, _dot(q_ref[0, h], kt_ref[0, jc, h]), neg_inf)
                m = mx[h]
                for sl in range(nsl):
                    m = jnp.maximum(m, slab(s, sl))
                out.append(m)
            return tuple(out)

        mx = lax.fori_loop(0, n_chunks, max_body, tuple(jnp.full((tq, LANES), neg_inf, F32) for _ in range(A_HEADS)))
        for h in range(A_HEADS):
            m_ref[h] = jnp.broadcast_to(jnp.max(mx[h], axis=-1, keepdims=True), (tq, LANES))

    lane_k = lax.broadcasted_iota(I32, (ck, LANES), 1)
    low_k = lane_k < A_HEAD_DIM
    one_even = jnp.where(lane_k == 0, 1.0, 0.0).astype(BF16)
    one_odd = jnp.where(lane_k == 1, 1.0, 0.0).astype(BF16)
    acc_ref[...] = jnp.zeros(acc_ref.shape, F32)

    def att_body(jc, carry):
        sel = selected(jc)
        krows = pl.ds(pl.multiple_of(jc * ck, ck), ck)
        for hp in range(A_HEADS // 2):
            v2 = v_ref[0, krows, hp * LANES:(hp + 1) * LANES]
            zero = jnp.zeros_like(v2)
            rhs = jnp.concatenate([
                jnp.concatenate([jnp.where(low_k, v2, zero), one_even], axis=1),
                jnp.concatenate([jnp.where(low_k, zero, v2), one_odd], axis=1)], axis=0)
            ps = []
            for e in range(2):
                h = 2 * hp + e
                s = _dot(q_ref[0, h], kt_ref[0, jc, h])
                p = jnp.where(sel, jnp.exp2(s - pltpu.repeat(m_ref[h], nsl, axis=1)), 0.0)
                ps.append(p.astype(BF16))
            acc_ref[hp] += _dot(jnp.concatenate(ps, axis=1), rhs)
        return carry

    lax.fori_loop(0, n_chunks, att_body, 0)
    low_q = lax.broadcasted_iota(I32, (tq, LANES), 1) < A_HEAD_DIM
    for hp in range(A_HEADS // 2):
        a = acc_ref[hp]
        inv = jnp.where(low_q, 1.0 / a[:, LANES:LANES + 1], 1.0 / a[:, LANES + 1:LANES + 2])
        o_ref[0, :, hp * LANES:(hp + 1) * LANES] = (a[:, :LANES] * inv).astype(o_ref.dtype)


def _dsa(qi, w, kti, q, kt, v, tri, tq, ck, topk):
    bsz, hi, seq, di = qi.shape
    nc = seq // ck
    kern = functools.partial(_dsa_kernel, tq=tq, ck=ck, topk=topk, rb=min(tq, 128), nbis=BISECT_STEPS)
    return pl.pallas_call(
        kern,
        grid=(bsz, seq // tq),
        in_specs=[
            pl.BlockSpec((1, hi, tq, di), lambda b, i: (b, 0, i, 0)),
            pl.BlockSpec((1, tq, LANES), lambda b, i: (b, i, 0)),
            pl.BlockSpec((1, nc, di, ck), lambda b, i: (b, 0, 0, 0)),
            pl.BlockSpec((1, A_HEADS, tq, A_HEAD_DIM), lambda b, i: (b, 0, i, 0)),
            pl.BlockSpec((1, nc, A_HEADS, A_HEAD_DIM, ck), lambda b, i: (b, 0, 0, 0, 0)),
            pl.BlockSpec((1, seq, A_WIDTH), lambda b, i: (b, 0, 0)),
            pl.BlockSpec((ck, ck), lambda b, i: (0, 0)),
        ],
        out_specs=pl.BlockSpec((1, tq, A_WIDTH), lambda b, i: (b, i, 0)),
        out_shape=jax.ShapeDtypeStruct((bsz, seq, A_WIDTH), BF16),
        scratch_shapes=[
            pltpu.VMEM((nc, tq, ck), F32),
            pltpu.VMEM((A_HEADS // 2, tq, 2 * LANES), F32),
            pltpu.VMEM((A_HEADS, tq, LANES), F32),
            pltpu.VMEM((tq, LANES), F32),
            pltpu.VMEM((A_HEADS, 8, LANES), F32),
        ],
        compiler_params=pltpu.CompilerParams(
            dimension_semantics=("arbitrary", "arbitrary"), vmem_limit_bytes=VMEM_LIMIT_BYTES),
        name="dsa",
    )(qi, w, kti, q, kt, v, tri)


def _fold8(x, op, init):
    r = init.shape[0]
    for s in range(x.shape[0] // r):
        init = op(init, x[s * r:(s + 1) * r, :])
    return init


def _dsat_kernel(qit_ref, wt_ref, ki_ref, qt_ref, k_ref, vt_ref, tril_ref, o_ref,
                 sc_ref, acc_ref, p_ref, kn_ref, *, tq, ck, topk, nbis):
    i = pl.program_id(1)
    n_chunks = ((i + 1) * tq + ck - 1) // ck
    kf = float(topk)
    neg_inf = float("-inf")
    pos_inf = float("inf")
    full8 = lambda val: jnp.full((8, tq), val, F32)
    ar = SELECT_ACC_ROWS
    full_acc = lambda val: jnp.full((ar, tq), val, F32)
    row_min = lambda x: jnp.min(x, axis=0, keepdims=True)
    row_max = lambda x: jnp.max(x, axis=0, keepdims=True)
    row_sum = lambda x: jnp.sum(x, axis=0, keepdims=True)

    @pl.when(i == 0)
    def _():
        head_of_lane = lax.broadcasted_iota(I32, (A_WIDTH, LANES), 0) >> _log2(A_HEAD_DIM)
        pick = jnp.where(head_of_lane == lax.broadcasted_iota(I32, (A_WIDTH, LANES), 1), 1.0, 0.0).astype(BF16)

        def kn_body(jc, mx):
            k = k_ref[0, pl.ds(pl.multiple_of(jc * ck, ck), ck), :].astype(F32)
            return _fold8(_dot((k * k).astype(BF16), pick), jnp.maximum, mx)

        mx = lax.fori_loop(0, k_ref.shape[1] // ck, kn_body, jnp.zeros((8, LANES), F32))
        kn_ref[...] = jnp.broadcast_to(row_max(mx), (8, LANES))

    t_pos = i * tq + lax.broadcasted_iota(I32, (ck, tq), 1)
    s_iota = lax.broadcasted_iota(I32, (ck, tq), 0)

    def score_body(masked, jc, carry):
        mn, mx = carry
        kc = ki_ref[0, pl.ds(pl.multiple_of(jc * ck, ck), ck), :]
        score = jnp.zeros((ck, tq), F32)
        for h in range(IDX_HEADS):
            score = score + wt_ref[0, h:h + 1, :] * jnp.maximum(_dot(kc, qit_ref[0, h]), 0.0)
        if masked:
            causal = (s_iota + jc * ck) <= t_pos
            sc_ref[jc] = jnp.where(causal, score, neg_inf)
            mn = _fold8(jnp.where(causal, score, pos_inf), jnp.minimum, mn)
            mx = _fold8(jnp.where(causal, score, neg_inf), jnp.maximum, mx)
        else:
            sc_ref[jc] = score
            mn = _fold8(score, jnp.minimum, mn)
            mx = _fold8(score, jnp.maximum, mx)
        return mn, mx

    n_full = (i * tq + 1) // ck
    carry = lax.fori_loop(0, n_full, functools.partial(score_body, False), (full_acc(pos_inf), full_acc(neg_inf)))
    mn_acc, mx_acc = lax.fori_loop(n_full, n_chunks, functools.partial(score_body, True), carry)
    rmin, rmax = row_min(mn_acc), row_max(mx_acc)

    def count_ge(cand):
        cand_b = jnp.broadcast_to(cand, (ar, tq))

        def body(jc, acc):
            return _fold8(sc_ref[jc], lambda a, x: a + jnp.where(x >= cand_b, 1.0, 0.0), acc)

        return row_sum(lax.fori_loop(0, n_chunks, body, full_acc(0.0)))

    def snap(lo, hi):
        lo_b = jnp.broadcast_to(lo, (ar, tq))
        hi_b = jnp.broadcast_to(hi, (ar, tq))

        def body(jc, carry):
            x = sc_ref[jc]
            a = _fold8(x, lambda a, x: jnp.minimum(a, jnp.where(x >= lo_b, x, pos_inf)), carry[0])
            b = _fold8(x, lambda b, x: jnp.maximum(b, jnp.where(x < hi_b, x, neg_inf)), carry[1])
            return a, b

        a, b = lax.fori_loop(0, n_chunks, body, (full_acc(pos_inf), full_acc(neg_inf)))
        return row_min(a), row_max(b)

    def probe(cand):
        cand_b = jnp.broadcast_to(cand, (ar, tq))

        def body(jc, carry):
            cnt, a, b = carry
            blk = sc_ref[jc]
            for s in range(ck // ar):
                x = blk[s * ar:(s + 1) * ar, :]
                ge = x >= cand_b
                cnt = cnt + jnp.where(ge, 1.0, 0.0)
                a = jnp.minimum(a, jnp.where(ge, x, pos_inf))
                b = jnp.maximum(b, jnp.where(ge, neg_inf, x))
            return cnt, a, b

        cnt, a, b = lax.fori_loop(0, n_chunks, body, (full_acc(0.0), full_acc(pos_inf), full_acc(neg_inf)))
        return row_sum(cnt), row_min(a), row_max(b)

    n_valid = (i * tq + 1 + lax.broadcasted_iota(I32, (1, tq), 1)).astype(F32)
    small = n_valid <= kf

    def bis_body(_, st):
        lo, hi, c_lo, c_hi = st
        mid = lo + (hi - lo) * 0.5
        ok = (mid > lo) & (mid < hi)
        c = count_ge(mid)
        up = ok & (c >= kf)
        dn = ok & (c < kf)
        return jnp.where(up, mid, lo), jnp.where(dn, mid, hi), jnp.where(up, c, c_lo), jnp.where(dn, c, c_hi)

    hi0 = rmax + jnp.maximum(jnp.abs(rmax) * 1e-6, 1e-30)
    lo, hi, c_lo, c_hi = lax.fori_loop(0, nbis, bis_body, (rmin, hi0, n_valid, jnp.zeros((1, tq), F32)))

    lo, hi = snap(lo, hi)

    def active_of(lo, hi, c_lo):
        return jnp.logical_not(small) & (lo < hi) & (c_lo != kf)

    def snap_body(st):
        lo, hi, c_lo, c_hi, _ = st
        act = active_of(lo, hi, c_lo)
        frac = jnp.clip((c_lo - kf + 0.5) / jnp.maximum(c_lo - c_hi, 1.0), 0.1, 0.9)
        mid = lo + (hi - lo) * frac
        mid = jnp.where((c_hi == kf - 1.0) | (mid <= lo) | (mid > hi), hi, mid)
        c, a, b = probe(mid)
        up = act & (c >= kf)
        dn = act & (c < kf)
        lo, c_lo = jnp.where(up, a, lo), jnp.where(up, c, c_lo)
        hi, c_hi = jnp.where(dn, b, hi), jnp.where(dn, c, c_hi)
        return lo, hi, c_lo, c_hi, jnp.max(jnp.where(active_of(lo, hi, c_lo), 1.0, 0.0))

    flag0 = jnp.max(jnp.where(active_of(lo, hi, c_lo), 1.0, 0.0))
    lo, hi, c_lo, c_hi, _ = lax.while_loop(lambda st: st[4] > 0.0, snap_body, (lo, hi, c_lo, c_hi, flag0))

    tau = jnp.where(small, rmin, lo)
    excess = jnp.logical_not(small) & (lo == hi) & (c_lo > kf)
    need = jnp.where(excess, kf - c_hi, 4.0 * 65536.0 * 65536.0)

    @pl.when(jnp.max(jnp.where(excess, 1.0, 0.0)) > 0.0)
    def _():
        def tie_body(jc, carry):
            blk = sc_ref[jc]
            eq = blk == tau
            pc = _dot(tril_ref[...], jnp.where(eq, 1.0, 0.0).astype(BF16)) + carry
            sc_ref[jc] = jnp.where(eq & (pc > need), neg_inf, blk)
            return pc[ck - 1:ck, :]

        lax.fori_loop(0, n_chunks, tie_body, jnp.zeros((1, tq), F32))

    def logits_t(jc, h):
        kc = k_ref[0, pl.ds(pl.multiple_of(jc * ck, ck), ck), (h // 2) * LANES:(h // 2 + 1) * LANES]
        return _dot(kc, qt_ref[0, h])

    m_bound = []
    bmax = jnp.zeros((1, 1), F32)
    for h in range(A_HEADS):
        qf = qt_ref[0, h].astype(F32)
        bound = jnp.sqrt(row_sum(qf * qf) * kn_ref[0:1, h:h + 1])
        m_bound.append(bound)
        bmax = jnp.maximum(bmax, jnp.max(bound, axis=1, keepdims=True))

    def exact_max(_):
        def max_body(jc, mx):
            sel = sc_ref[jc] >= tau
            return tuple(_fold8(jnp.where(sel, logits_t(jc, h), neg_inf), jnp.maximum, mx[h]) for h in range(A_HEADS))

        mx = lax.fori_loop(0, n_chunks, max_body, tuple(full8(neg_inf) for _ in range(A_HEADS)))
        return tuple(row_max(m) for m in mx)

    m_ref_vals = lax.cond(bmax[0, 0] > LOGIT_BOUND_LIMIT, exact_max, lambda _: tuple(m_bound), 0)

    ones_rows = jnp.ones((16, ck), BF16)
    acc_ref[...] = jnp.zeros(acc_ref.shape, F32)

    def pv_stage(jc, h):
        lhs = jnp.concatenate([vt_ref[0, jc, h], ones_rows], axis=0)
        acc_ref[h] += _dot(lhs, p_ref[h])

    def qk_stage(jc, sel, h):
        p_ref[h] = jnp.where(sel, jnp.exp2(logits_t(jc, h) - m_ref_vals[h]), 0.0).astype(BF16)

    sel0 = sc_ref[0] >= tau
    for h in range(A_HEADS):
        qk_stage(0, sel0, h)

    def att_body(jc, carry):
        sel = sc_ref[jc] >= tau
        for h in range(A_HEADS):
            pv_stage(jc - 1, h)
            qk_stage(jc, sel, h)
        return carry

    lax.fori_loop(1, n_chunks, att_body, 0)
    for h in range(A_HEADS):
        pv_stage(n_chunks - 1, h)
    for h in range(A_HEADS):
        a = acc_ref[h]
        o_ref[0, h] = (a[:A_HEAD_DIM] * (1.0 / a[A_HEAD_DIM:A_HEAD_DIM + 1])).astype(o_ref.dtype)


def _dsat(qit, wt, ki, qt, k, vt, tril, tq, ck, topk):
    bsz, hi, di, seq = qit.shape
    nc = seq // ck
    kern = functools.partial(_dsat_kernel, tq=tq, ck=ck, topk=topk, nbis=BISECT_STEPS)
    return pl.pallas_call(
        kern,
        grid=(bsz, seq // tq),
        in_specs=[
            pl.BlockSpec((1, hi, di, tq), lambda b, i: (b, 0, 0, i)),
            pl.BlockSpec((1, 8, tq), lambda b, i: (b, 0, i)),
            pl.BlockSpec((1, seq, di), lambda b, i: (b, 0, 0)),
            pl.BlockSpec((1, A_HEADS, LANES, tq), lambda b, i: (b, 0, 0, i)),
            pl.BlockSpec((1, seq, A_WIDTH), lambda b, i: (b, 0, 0)),
            pl.BlockSpec((1, nc, A_HEADS, A_HEAD_DIM, ck), lambda b, i: (b, 0, 0, 0, 0)),
            pl.BlockSpec((ck, ck), lambda b, i: (0, 0)),
        ],
        out_specs=pl.BlockSpec((1, A_HEADS, A_HEAD_DIM, tq), lambda b, i: (b, 0, 0, i)),
        out_shape=jax.ShapeDtypeStruct((bsz, A_HEADS, A_HEAD_DIM, seq), BF16),
        scratch_shapes=[
            pltpu.VMEM((nc, ck, tq), F32),
            pltpu.VMEM((A_HEADS, A_HEAD_DIM + 16, tq), F32),
            pltpu.VMEM((A_HEADS, ck, tq), BF16),
            pltpu.VMEM((8, LANES), F32),
        ],
        compiler_params=pltpu.CompilerParams(
            dimension_semantics=("arbitrary", "arbitrary"), vmem_limit_bytes=VMEM_LIMIT_BYTES),
        name="dsa",
    )(qit, wt, ki, qt, k, vt, tril)


def _hgrn_kernel(bq_ref, bf_ref, bi_ref, bg_ref, lb_ref, go_ref, o_ref, st_ref, oi_ref, *, tl, layer):
    cs = HGRN_CHUNK
    nch = tl // cs

    @pl.when(pl.program_id(1) == 0)
    def _():
        st_ref[...] = jnp.zeros(st_ref.shape, F32)

    lbr = lb_ref[...]
    slots = [lbr[k:k + 1] for k in range(lbr.shape[0])]
    mx = functools.reduce(jnp.maximum, slots)
    es = [jnp.exp(s - mx) for s in slots]
    lb = functools.reduce(jnp.add, es[:layer + 1]) / functools.reduce(jnp.add, es)

    bq = bq_ref[...]
    q = bq * _sigmoid(bq)
    f = lb + (1.0 - lb) * _sigmoid(bf_ref[...])
    kk = 1.0 - f
    g = jnp.log(f)
    v = bi_ref[...]

    r_i = lax.broadcasted_iota(I32, (tl, tl), 0)
    c_i = lax.broadcasted_iota(I32, (tl, tl), 1)
    same = (r_i >> _log2(cs)) == (c_i >> _log2(cs))
    tri = jnp.where(same & (c_i <= r_i), 1.0, 0.0).astype(BF16)
    blk = jnp.where(same, 1.0, 0.0).astype(BF16)
    g1, g2, g3 = _split3(g)
    b = _dot(tri, g1) + _dot(tri, g2) + _dot(tri, g3)
    bend = _dot(blk, g1) + _dot(blk, g2) + _dot(blk, g3)

    pos = lax.broadcasted_iota(I32, (tl, B_WIDTH), 0) & (cs - 1)
    o = jnp.zeros((tl, B_WIDTH), F32)
    for r in range(cs):
        if r == 0:
            kk_s, b_s, v_s = kk, b, v
        else:
            kk_s = pltpu.roll(kk, r, axis=0)
            b_s = pltpu.roll(b, r, axis=0)
            v_s = pltpu.roll(v, r, axis=0)
        e = q * kk_s * jnp.exp(jnp.where(pos >= r, b - b_s, -jnp.inf))
        parts = []
        for h in range(B_HEADS):
            sl = slice(h * B_HEAD_DIM, (h + 1) * B_HEAD_DIM)
            parts.append(jnp.sum(e[:, sl], axis=-1, keepdims=True) * v_s[:, sl])
        o = o + jnp.concatenate(parts, axis=1)

    qe = (q * jnp.exp(b)).astype(BF16)
    kd = (kk * jnp.exp(bend - b)).astype(BF16)
    vb = v.astype(BF16)
    dec = jnp.exp(bend)
    for c in range(nch):
        rs = slice(c * cs, (c + 1) * cs)
        for h in range(B_HEADS):
            sl = slice(h * B_HEAD_DIM, (h + 1) * B_HEAD_DIM)
            st = st_ref[h]
            oi_ref[rs, sl] = lax.dot_general(qe[rs, sl], st.astype(BF16), (((1,), (1,)), ((), ())),
                                             preferred_element_type=F32)
            upd = lax.dot_general(vb[rs, sl], kd[rs, sl], (((0,), (0,)), ((), ())),
                                  preferred_element_type=F32)
            st_ref[h] = st * dec[c * cs:c * cs + 1, sl] + upd
    o = o + oi_ref[...]

    parts = []
    for h in range(B_HEADS):
        sl = slice(h * B_HEAD_DIM, (h + 1) * B_HEAD_DIM)
        oh = o[:, sl]
        parts.append(oh * lax.rsqrt(jnp.mean(oh * oh, axis=-1, keepdims=True) + EPS))
    on = jnp.concatenate(parts, axis=1) * go_ref[...]
    bg = bg_ref[...]
    o_ref[...] = (on * (bg * _sigmoid(bg))).astype(o_ref.dtype)


def _hgrn(proj, lb_table, go, bsz, seq, tl, layer):
    m = proj.shape[0]
    tpb = seq // tl
    col = lambda c: (lambda b, t: (b * tpb + t, c // B_WIDTH))
    return pl.pallas_call(
        functools.partial(_hgrn_kernel, tl=tl, layer=layer),
        grid=(bsz, tpb),
        in_specs=[
            pl.BlockSpec((tl, B_WIDTH), col(COL_BQ)),
            pl.BlockSpec((tl, B_WIDTH), col(COL_BF)),
            pl.BlockSpec((tl, B_WIDTH), col(COL_BI)),
            pl.BlockSpec((tl, B_WIDTH), col(COL_BG)),
            pl.BlockSpec(lb_table.shape, lambda b, t: (0, 0)),
            pl.BlockSpec((1, B_WIDTH), lambda b, t: (0, 0)),
        ],
        out_specs=pl.BlockSpec((tl, B_WIDTH), lambda b, t: (b * tpb + t, 0)),
        out_shape=jax.ShapeDtypeStruct((m, B_WIDTH), BF16),
        scratch_shapes=[
            pltpu.VMEM((B_HEADS, B_HEAD_DIM, B_HEAD_DIM), F32),
            pltpu.VMEM((tl, B_WIDTH), F32),
        ],
        compiler_params=pltpu.CompilerParams(
            dimension_semantics=("arbitrary", "arbitrary"), vmem_limit_bytes=VMEM_LIMIT_BYTES),
        name="hgrn",
    )(proj, proj, proj, proj, lb_table, go)


def _merge_kernel(oa_ref, ob_ref, ga_ref, gb_ref, x_ref, g1_ref, wa_ref, wb_ref, wo_ref, o_ref):
    pa = _dot(oa_ref[...], wa_ref[...])
    pb = _dot(ob_ref[...], wb_ref[...])
    merged = _sigmoid(ga_ref[...]) * pa + _sigmoid(gb_ref[...]) * pb
    y = _dot(merged.astype(BF16), wo_ref[...])
    o_ref[...] = x_ref[...] + g1_ref[0] * y


def _merge(oa, ob, proj, x2, gate1, wa, wb, wo, seq, tm):
    m, d = x2.shape
    tpb = seq // tm
    row = lambda i: (i, 0)
    const = lambda i: (0, 0)
    return pl.pallas_call(
        _merge_kernel,
        grid=(m // tm,),
        in_specs=[
            pl.BlockSpec((tm, A_WIDTH), row),
            pl.BlockSpec((tm, B_WIDTH), row),
            pl.BlockSpec((tm, d), lambda i: (i, COL_GA // d)),
            pl.BlockSpec((tm, d), lambda i: (i, COL_GB // d)),
            pl.BlockSpec((tm, d), row),
            pl.BlockSpec((1, 1, d), lambda i: (i // tpb, 0, 0)),
            pl.BlockSpec(wa.shape, const),
            pl.BlockSpec(wb.shape, const),
            pl.BlockSpec(wo.shape, const),
        ],
        out_specs=pl.BlockSpec((tm, d), row),
        out_shape=jax.ShapeDtypeStruct((m, d), F32),
        compiler_params=pltpu.CompilerParams(vmem_limit_bytes=VMEM_LIMIT_BYTES),
        name="merge",
    )(oa, ob, proj, proj, x2, gate1, wa, wb, wo)


def _mlp_kernel(x_ref, g_ref, sc_ref, sh_ref, g2_ref, w1_ref, b1_ref, w2_ref, b2_ref, o_ref, *, tf):
    x = x_ref[...]
    ms = jnp.mean(x * x, axis=-1, keepdims=True)
    h = x * lax.rsqrt(ms + EPS) * g_ref[...]
    h = (h * (1.0 + sc_ref[0]) + sh_ref[0]).astype(BF16)
    dff = w1_ref.shape[1]
    y = jnp.zeros(x.shape, F32)
    for c in range(dff // tf):
        cs = slice(c * tf, (c + 1) * tf)
        a = jnp.maximum(_dot(h, w1_ref[:, cs]) + b1_ref[:, cs], 0.0)
        y = y + _dot((a * a).astype(BF16), w2_ref[cs, :])
    o_ref[...] = x + g2_ref[0] * (y + b2_ref[...])


def _mlp(x1, g, scale, shift, gate2, w1, b1, w2, b2, seq, tm, tf):
    m, d = x1.shape
    tpb = seq // tm
    row = lambda i: (i, 0)
    const = lambda i: (0, 0)
    bat = lambda i: (i // tpb, 0, 0)
    return pl.pallas_call(
        functools.partial(_mlp_kernel, tf=tf),
        grid=(m // tm,),
        in_specs=[
            pl.BlockSpec((tm, d), row),
            pl.BlockSpec((1, d), const),
            pl.BlockSpec((1, 1, d), bat),
            pl.BlockSpec((1, 1, d), bat),
            pl.BlockSpec((1, 1, d), bat),
            pl.BlockSpec(w1.shape, const),
            pl.BlockSpec(b1.shape, const),
            pl.BlockSpec(w2.shape, const),
            pl.BlockSpec(b2.shape, const),
        ],
        out_specs=pl.BlockSpec((tm, d), row),
        out_shape=jax.ShapeDtypeStruct((m, d), F32),
        compiler_params=pltpu.CompilerParams(vmem_limit_bytes=VMEM_LIMIT_BYTES),
        name="mlp",
    )(x1, g, scale, shift, gate2, w1, b1, w2, b2)


def _regroup_w_in(w):
    d = w.shape[0]
    sizes = (Q_LORA_RANK, A_WIDTH, A_WIDTH, IDX_DIM, IDX_HEADS, B_WIDTH, B_WIDTH, B_WIDTH, B_WIDTH, d, d)
    offs = [0]
    for s in sizes:
        offs.append(offs[-1] + s)
    part = lambda k: w[:, offs[k]:offs[k + 1]]
    q_lat, a_k, a_v, k_idx, w_idx, b_q, b_f, b_i, b_g, gate_a, gate_b = (part(k) for k in range(11))
    pad = jnp.zeros((d, LANES - IDX_DIM - IDX_HEADS), w.dtype)
    out = jnp.concatenate([a_k, a_v, b_q, b_f, b_i, b_g, gate_a, gate_b, q_lat, k_idx, w_idx, pad], axis=1)
    assert out.shape[1] == PROJ_COLS
    return out.astype(BF16)


def _layer(x, mod, l, p, tiles):
    bsz, seq, d = x.shape
    m = bsz * seq
    shift1, scale1, gate1, shift2, scale2, gate2 = (mod[:, k * d:(k + 1) * d].reshape(bsz, 1, d) for k in range(6))
    x2 = x.reshape(m, d)

    proj = _inproj(x2, p['norm1_g'][l][None], scale1, shift1, _regroup_w_in(p['w_in'][l]), seq, tiles['tm_in'])

    wuq = jnp.concatenate([p['w_uq'][l], p['w_uq_idx'][l]], axis=1).astype(BF16)
    gki = jnp.concatenate([p['k_idx_norm_g'][l], jnp.zeros((LANES - IDX_DIM,), F32)])[None]
    aq, ak, qi, ki, wi = _qkprep(
        proj, p['q_lat_norm_g'][l][None], wuq,
        jnp.tile(p['q_norm_g'][l], A_HEADS)[None], jnp.tile(p['k_norm_g'][l], A_HEADS)[None], gki, tiles['tm'])

    tq, ck = tiles['tq'], tiles['ck']
    nc = seq // ck
    topk = min(TOPK_MAX, seq // 4)
    qit = qi.astype(BF16).reshape(bsz, seq, IDX_HEADS, IDX_DIM).transpose(0, 2, 3, 1)
    wt = wi[:, IDX_DIM:IDX_DIM + 8].reshape(bsz, seq, 8).transpose(0, 2, 1)
    q_t = aq.reshape(bsz, seq, A_HEADS // 2, 2, A_HEAD_DIM).transpose(0, 2, 3, 4, 1)
    zeros = jnp.zeros_like(q_t[:, :, 0])
    qt = jnp.stack([jnp.concatenate([q_t[:, :, 0], zeros], axis=2),
                    jnp.concatenate([zeros, q_t[:, :, 1]], axis=2)], axis=2).reshape(bsz, A_HEADS, LANES, seq)
    av = lax.slice_in_dim(proj, COL_AV, COL_AV + A_WIDTH, axis=1).astype(BF16)
    vt = av.reshape(bsz, nc, ck, A_HEADS, A_HEAD_DIM).transpose(0, 1, 3, 4, 2)
    tril = jnp.tril(jnp.ones((ck, ck), BF16))
    out_at = _dsat(qit, wt, ki.reshape(bsz, seq, IDX_DIM), qt, ak.reshape(bsz, seq, A_WIDTH), vt, tril, tq, ck, topk)
    out_a = out_at.transpose(0, 3, 1, 2).reshape(m, A_WIDTH)

    out_b = _hgrn(proj, p['hgrn_lb'], p['hgrn_o_norm_g'][l][None], bsz, seq, tiles['tl'], l)

    x1 = _merge(out_a, out_b, proj, x2, gate1, p['w_proj_a'][l].astype(BF16), p['w_proj_b'][l].astype(BF16),
                p['w_out'][l].astype(BF16), seq, tiles['tm'])
    out = _mlp(x1, p['norm2_g'][l][None], scale2, shift2, gate2, p['w_mlp1'][l].astype(BF16), p['b_mlp1'][l][None],
               p['w_mlp2'][l].astype(BF16), p['b_mlp2'][l][None], seq, tiles['tm'], tiles['tf'])
    return out.reshape(bsz, seq, d)


def _tiles(seq):
    pick = lambda want: min(want, seq)
    return dict(tm_in=pick(256), tm=pick(512), tq=pick(256), ck=pick(512), tl=pick(256), tf=1024)


def kernel(x, c, w_ada, b_ada, norm1_g, w_in, q_lat_norm_g, w_uq, w_uq_idx, q_norm_g, k_norm_g, k_idx_norm_g,
           hgrn_lb, hgrn_o_norm_g, w_proj_a, w_proj_b, w_out, norm2_g, w_mlp1, b_mlp1, w_mlp2, b_mlp2):
    p = dict(norm1_g=norm1_g, w_in=w_in, q_lat_norm_g=q_lat_norm_g, w_uq=w_uq, w_uq_idx=w_uq_idx,
             q_norm_g=q_norm_g, k_norm_g=k_norm_g, k_idx_norm_g=k_idx_norm_g, hgrn_lb=hgrn_lb,
             hgrn_o_norm_g=hgrn_o_norm_g, w_proj_a=w_proj_a, w_proj_b=w_proj_b, w_out=w_out, norm2_g=norm2_g,
             w_mlp1=w_mlp1, b_mlp1=b_mlp1, w_mlp2=w_mlp2, b_mlp2=b_mlp2)
    bsz, seq, d = x.shape
    depth = w_ada.shape[0]
    tiles = _tiles(seq)
    c_pad = jnp.zeros((8, d), F32).at[:bsz].set(c)
    for l in range(depth):
        mod = _adaln(c_pad, w_ada[l], b_ada[l][None])[:bsz]
        x = _layer(x, mod, l, p, tiles)
    return x
```

```python
import functools

import jax
import jax.numpy as jnp
from jax import lax
from jax.experimental import pallas as pl
from jax.experimental.pallas import tpu as pltpu

F32 = jnp.float32
BF16 = jnp.bfloat16
I32 = jnp.int32

EPS = 1e-6
A_HEADS = 8
A_HEAD_DIM = 64
A_WIDTH = A_HEADS * A_HEAD_DIM
Q_LORA_RANK = 256
IDX_HEADS = 4
IDX_DIM = 64
TOPK_MAX = 256
B_WIDTH = 512
B_HEADS = 4
B_HEAD_DIM = 128
HGRN_CHUNK = 16

LANES = 128
VMEM_LIMIT_BYTES = 56 * 1024 * 1024

COL_AK, COL_AV, COL_BQ, COL_BF, COL_BI, COL_BG = 0, 512, 1024, 1536, 2048, 2560
COL_GA, COL_GB, COL_QL, COL_KW = 3072, 4096, 5120, 5376
PROJ_COLS = 5504

LOG2E = 1.4426950408889634
BISECT_STEPS = 15
SELECT_ACC_ROWS = 32
LOGIT_BOUND_LIMIT = 60.0


def _sigmoid(x):
    return 0.5 * jnp.tanh(0.5 * x) + 0.5


def _split2(x):
    hi = x.astype(BF16)
    lo = (x - hi.astype(F32)).astype(BF16)
    return hi, lo


def _split3(x):
    a = x.astype(BF16)
    r = x - a.astype(F32)
    b = r.astype(BF16)
    c = (r - b.astype(F32)).astype(BF16)
    return a, b, c


def _dot(a, b):
    return jnp.dot(a, b, preferred_element_type=F32)


def _log2(n):
    assert n > 0 and n & (n - 1) == 0, n
    return n.bit_length() - 1


def _group_ones(n, group):
    r = lax.broadcasted_iota(I32, (n, n), 0) >> _log2(group)
    c = lax.broadcasted_iota(I32, (n, n), 1) >> _log2(group)
    return jnp.where(r == c, 1.0, 0.0).astype(BF16)


def _group_mean_sq(x, ones_bd, group):
    hi, lo = _split2(x * x)
    return (_dot(hi, ones_bd) + _dot(lo, ones_bd)) * (1.0 / group)


def _fold_rows(x, op, init):
    r = init.shape[0]
    for s in range(x.shape[0] // r):
        init = op(init, x[s * r:(s + 1) * r, :])
    return init


def _adaln_kernel(c_ref, w_ref, b_ref, o_ref):
    c = c_ref[...]
    a = c * _sigmoid(c)
    a1, a2, a3 = _split3(a)
    w1, w2, w3 = _split3(w_ref[...])
    acc = _dot(a1, w1) + (_dot(a1, w2) + _dot(a2, w1)) + (_dot(a2, w2) + _dot(a1, w3) + _dot(a3, w1))
    o_ref[...] = acc + b_ref[...]


def _adaln(c_pad, w, b):
    rows, d = c_pad.shape
    n = w.shape[1]
    tn = 1536
    return pl.pallas_call(
        _adaln_kernel,
        grid=(n // tn,),
        in_specs=[
            pl.BlockSpec((rows, d), lambda j: (0, 0)),
            pl.BlockSpec((d, tn), lambda j: (0, j)),
            pl.BlockSpec((1, tn), lambda j: (0, j)),
        ],
        out_specs=pl.BlockSpec((rows, tn), lambda j: (0, j)),
        out_shape=jax.ShapeDtypeStruct((rows, n), F32),
        compiler_params=pltpu.CompilerParams(vmem_limit_bytes=VMEM_LIMIT_BYTES),
        name="adaln",
    )(c_pad, w, b)


def _inproj_kernel(x_ref, g_ref, sc_ref, sh_ref, w_ref, o_ref):
    x = x_ref[...]
    ms = jnp.mean(x * x, axis=-1, keepdims=True)
    h = x * lax.rsqrt(ms + EPS) * g_ref[...]
    h = h * (1.0 + sc_ref[0]) + sh_ref[0]
    o_ref[...] = _dot(h.astype(BF16), w_ref[...])


def _inproj(x2, g, scale, shift, w, seq, tm):
    m, d = x2.shape
    n = w.shape[1]
    tpb = seq // tm
    return pl.pallas_call(
        _inproj_kernel,
        grid=(m // tm,),
        in_specs=[
            pl.BlockSpec((tm, d), lambda i: (i, 0)),
            pl.BlockSpec((1, d), lambda i: (0, 0)),
            pl.BlockSpec((1, 1, d), lambda i: (i // tpb, 0, 0)),
            pl.BlockSpec((1, 1, d), lambda i: (i // tpb, 0, 0)),
            pl.BlockSpec((d, n), lambda i: (0, 0)),
        ],
        out_specs=pl.BlockSpec((tm, n), lambda i: (i, 0)),
        out_shape=jax.ShapeDtypeStruct((m, n), F32),
        compiler_params=pltpu.CompilerParams(vmem_limit_bytes=VMEM_LIMIT_BYTES),
        name="inproj",
    )(x2, g, scale, shift, w)


def _qkprep_kernel(ak_ref, av_ref, ql_ref, kw_ref, gql_ref, wuq_ref, gq_ref, gk_ref, gki_ref,
                   qt_out, ak_out, qit_out, ki_out, wt_out, vt_out):
    tm = ak_ref.shape[0]
    ones64 = _group_ones(A_WIDTH, A_HEAD_DIM)
    ql = ql_ref[...]
    ql = ql * lax.rsqrt(jnp.mean(ql * ql, axis=-1, keepdims=True) + EPS) * gql_ref[...]
    up = _dot(ql.astype(BF16), wuq_ref[...])
    aq = up[:, :A_WIDTH]
    aq = aq * lax.rsqrt(_group_mean_sq(aq, ones64, A_HEAD_DIM) + EPS) * gq_ref[...]
    aq_t = (aq * (A_HEAD_DIM ** -0.5 * LOG2E)).T
    zero_half = jnp.zeros((A_HEAD_DIM, tm), BF16)
    for h in range(A_HEADS):
        e = h % 2
        qt_out[0, h, e * A_HEAD_DIM:(e + 1) * A_HEAD_DIM, :] = aq_t[h * A_HEAD_DIM:(h + 1) * A_HEAD_DIM].astype(BF16)
        qt_out[0, h, (1 - e) * A_HEAD_DIM:(2 - e) * A_HEAD_DIM, :] = zero_half
    qit_out[0] = up[:, A_WIDTH:].T.reshape(IDX_HEADS, IDX_DIM, tm).astype(BF16)
    ak = ak_ref[...]
    ak = ak * lax.rsqrt(_group_mean_sq(ak, ones64, A_HEAD_DIM) + EPS) * gk_ref[...]
    ak_out[...] = ak.astype(BF16)
    vt_out[0, 0] = av_ref[...].T.reshape(A_HEADS, A_HEAD_DIM, tm).astype(BF16)
    kw = kw_ref[...]
    lane = lax.broadcasted_iota(I32, kw.shape, 1)
    ksq = jnp.where(lane < IDX_DIM, kw * kw, 0.0)
    kms = jnp.sum(ksq, axis=-1, keepdims=True) * (1.0 / IDX_DIM)
    kn = kw * lax.rsqrt(kms + EPS) * gki_ref[...]
    ki_out[...] = kn[:, :IDX_DIM].astype(BF16)
    idx_scale = (IDX_DIM ** -0.5) * (IDX_HEADS ** -0.5)
    w_rows = jnp.where((lane >= IDX_DIM) & (lane < IDX_DIM + IDX_HEADS), kw * idx_scale, 0.0).T
    wt_out[0] = w_rows[IDX_DIM:IDX_DIM + 8]


def _qkprep(proj, gql, wuq, gq, gk, gki, bsz, seq, tm):
    m = proj.shape[0]
    tpb = seq // tm
    row = lambda i: (i, 0)
    const = lambda i: (0, 0)
    return pl.pallas_call(
        _qkprep_kernel,
        grid=(m // tm,),
        in_specs=[
            pl.BlockSpec((tm, A_WIDTH), lambda i: (i, COL_AK // A_WIDTH)),
            pl.BlockSpec((tm, A_WIDTH), lambda i: (i, COL_AV // A_WIDTH)),
            pl.BlockSpec((tm, Q_LORA_RANK), lambda i: (i, COL_QL // Q_LORA_RANK)),
            pl.BlockSpec((tm, LANES), lambda i: (i, COL_KW // LANES)),
            pl.BlockSpec((1, Q_LORA_RANK), const),
            pl.BlockSpec(wuq.shape, const),
            pl.BlockSpec((1, A_WIDTH), const),
            pl.BlockSpec((1, A_WIDTH), const),
            pl.BlockSpec((1, LANES), const),
        ],
        out_specs=[
            pl.BlockSpec((1, A_HEADS, LANES, tm), lambda i: (i // tpb, 0, 0, i % tpb)),
            pl.BlockSpec((tm, A_WIDTH), row),
            pl.BlockSpec((1, IDX_HEADS, IDX_DIM, tm), lambda i: (i // tpb, 0, 0, i % tpb)),
            pl.BlockSpec((tm, IDX_DIM), row),
            pl.BlockSpec((1, 8, tm), lambda i: (i // tpb, 0, i % tpb)),
            pl.BlockSpec((1, 1, A_HEADS, A_HEAD_DIM, tm), lambda i: (i // tpb, i % tpb, 0, 0, 0)),
        ],
        out_shape=[
            jax.ShapeDtypeStruct((bsz, A_HEADS, LANES, seq), BF16),
            jax.ShapeDtypeStruct((m, A_WIDTH), BF16),
            jax.ShapeDtypeStruct((bsz, IDX_HEADS, IDX_DIM, seq), BF16),
            jax.ShapeDtypeStruct((m, IDX_DIM), BF16),
            jax.ShapeDtypeStruct((bsz, 8, seq), F32),
            jax.ShapeDtypeStruct((bsz, tpb, A_HEADS, A_HEAD_DIM, tm), BF16),
        ],
        compiler_params=pltpu.CompilerParams(vmem_limit_bytes=VMEM_LIMIT_BYTES),
        name="qkprep",
    )(proj, proj, proj, proj, gql, wuq, gq, gk, gki)


def _dsa_kernel(qit_ref, wt_ref, ki_ref, qt_ref, k_ref, vt_ref, tril_ref, o_ref,
                sc_ref, acc_ref, p_ref, kn_ref, *, tq, ck, topk, nbis):
    i = pl.program_id(1)
    n_chunks = ((i + 1) * tq + ck - 1) // ck
    kf = float(topk)
    neg_inf = float("-inf")
    pos_inf = float("inf")
    ar = SELECT_ACC_ROWS
    full_acc = lambda val: jnp.full((ar, tq), val, F32)
    row_min = lambda x: jnp.min(x, axis=0, keepdims=True)
    row_max = lambda x: jnp.max(x, axis=0, keepdims=True)
    row_sum = lambda x: jnp.sum(x, axis=0, keepdims=True)
    chunk_rows = lambda jc: pl.ds(pl.multiple_of(jc * ck, ck), ck)

    @pl.when(i == 0)
    def _():
        head_of_lane = lax.broadcasted_iota(I32, (A_WIDTH, LANES), 0) >> _log2(A_HEAD_DIM)
        pick = jnp.where(head_of_lane == lax.broadcasted_iota(I32, (A_WIDTH, LANES), 1), 1.0, 0.0).astype(BF16)

        def kn_body(jc, mx):
            k = k_ref[0, chunk_rows(jc), :].astype(F32)
            return _fold_rows(_dot((k * k).astype(BF16), pick), jnp.maximum, mx)

        mx = lax.fori_loop(0, k_ref.shape[1] // ck, kn_body, jnp.zeros((8, LANES), F32))
        kn_ref[...] = jnp.broadcast_to(row_max(mx), (8, LANES))

    t_pos = i * tq + lax.broadcasted_iota(I32, (ck, tq), 1)
    s_iota = lax.broadcasted_iota(I32, (ck, tq), 0)

    def score_body(masked, jc, carry):
        mn, mx = carry
        kc = ki_ref[0, chunk_rows(jc), :]
        score = jnp.zeros((ck, tq), F32)
        for h in range(IDX_HEADS):
            score = score + wt_ref[0, h:h + 1, :] * jnp.maximum(_dot(kc, qit_ref[0, h]), 0.0)
        if masked:
            causal = (s_iota + jc * ck) <= t_pos
            sc_ref[jc] = jnp.where(causal, score, neg_inf)
            mn = _fold_rows(jnp.where(causal, score, pos_inf), jnp.minimum, mn)
            mx = _fold_rows(jnp.where(causal, score, neg_inf), jnp.maximum, mx)
        else:
            sc_ref[jc] = score
            mn = _fold_rows(score, jnp.minimum, mn)
            mx = _fold_rows(score, jnp.maximum, mx)
        return mn, mx

    n_full = (i * tq + 1) // ck
    carry = lax.fori_loop(0, n_full, functools.partial(score_body, False), (full_acc(pos_inf), full_acc(neg_inf)))
    mn_acc, mx_acc = lax.fori_loop(n_full, n_chunks, functools.partial(score_body, True), carry)
    rmin, rmax = row_min(mn_acc), row_max(mx_acc)

    def count_ge(cand):
        cand_b = jnp.broadcast_to(cand, (ar, tq))

        def body(jc, acc):
            return _fold_rows(sc_ref[jc], lambda a, x: a + jnp.where(x >= cand_b, 1.0, 0.0), acc)

        return row_sum(lax.fori_loop(0, n_chunks, body, full_acc(0.0)))

    def snap(lo, hi):
        lo_b = jnp.broadcast_to(lo, (ar, tq))
        hi_b = jnp.broadcast_to(hi, (ar, tq))

        def body(jc, carry):
            x = sc_ref[jc]
            a = _fold_rows(x, lambda a, x: jnp.minimum(a, jnp.where(x >= lo_b, x, pos_inf)), carry[0])
            b = _fold_rows(x, lambda b, x: jnp.maximum(b, jnp.where(x < hi_b, x, neg_inf)), carry[1])
            return a, b

        a, b = lax.fori_loop(0, n_chunks, body, (full_acc(pos_inf), full_acc(neg_inf)))
        return row_min(a), row_max(b)

    def probe(cand):
        cand_b = jnp.broadcast_to(cand, (ar, tq))

        def body(jc, carry):
            cnt, a, b = carry
            blk = sc_ref[jc]
            for s in range(ck // ar):
                x = blk[s * ar:(s + 1) * ar, :]
                ge = x >= cand_b
                cnt = cnt + jnp.where(ge, 1.0, 0.0)
                a = jnp.minimum(a, jnp.where(ge, x, pos_inf))
                b = jnp.maximum(b, jnp.where(ge, neg_inf, x))
            return cnt, a, b

        cnt, a, b = lax.fori_loop(0, n_chunks, body, (full_acc(0.0), full_acc(pos_inf), full_acc(neg_inf)))
        return row_sum(cnt), row_min(a), row_max(b)

    n_valid = (i * tq + 1 + lax.broadcasted_iota(I32, (1, tq), 1)).astype(F32)
    small = n_valid <= kf

    def bis_body(_, st):
        lo, hi, c_lo, c_hi = st
        mid = lo + (hi - lo) * 0.5
        ok = (mid > lo) & (mid < hi)
        c = count_ge(mid)
        up = ok & (c >= kf)
        dn = ok & (c < kf)
        return jnp.where(up, mid, lo), jnp.where(dn, mid, hi), jnp.where(up, c, c_lo), jnp.where(dn, c, c_hi)

    hi0 = rmax + jnp.maximum(jnp.abs(rmax) * 1e-6, 1e-30)
    lo, hi, c_lo, c_hi = lax.fori_loop(0, nbis, bis_body, (rmin, hi0, n_valid, jnp.zeros((1, tq), F32)))

    lo, hi = snap(lo, hi)

    def active_of(lo, hi, c_lo):
        return jnp.logical_not(small) & (lo < hi) & (c_lo != kf)

    def snap_body(st):
        lo, hi, c_lo, c_hi, _ = st
        act = active_of(lo, hi, c_lo)
        frac = jnp.clip((c_lo - kf + 0.5) / jnp.maximum(c_lo - c_hi, 1.0), 0.1, 0.9)
        mid = lo + (hi - lo) * frac
        mid = jnp.where((c_hi == kf - 1.0) | (mid <= lo) | (mid > hi), hi, mid)
        c, a, b = probe(mid)
        up = act & (c >= kf)
        dn = act & (c < kf)
        lo, c_lo = jnp.where(up, a, lo), jnp.where(up, c, c_lo)
        hi, c_hi = jnp.where(dn, b, hi), jnp.where(dn, c, c_hi)
        return lo, hi, c_lo, c_hi, jnp.max(jnp.where(active_of(lo, hi, c_lo), 1.0, 0.0))

    flag0 = jnp.max(jnp.where(active_of(lo, hi, c_lo), 1.0, 0.0))
    lo, hi, c_lo, c_hi, _ = lax.while_loop(lambda st: st[4] > 0.0, snap_body, (lo, hi, c_lo, c_hi, flag0))

    tau = jnp.where(small, rmin, lo)
    excess = jnp.logical_not(small) & (lo == hi) & (c_lo > kf)
    need = jnp.where(excess, kf - c_hi, 4.0 * 65536.0 * 65536.0)

    @pl.when(jnp.max(jnp.where(excess, 1.0, 0.0)) > 0.0)
    def _():
        def tie_body(jc, carry):
            blk = sc_ref[jc]
            eq = blk == tau
            pc = _dot(tril_ref[...], jnp.where(eq, 1.0, 0.0).astype(BF16)) + carry
            sc_ref[jc] = jnp.where(eq & (pc > need), neg_inf, blk)
            return pc[ck - 1:ck, :]

        lax.fori_loop(0, n_chunks, tie_body, jnp.zeros((1, tq), F32))

    def logits_t(jc, h):
        kc = k_ref[0, chunk_rows(jc), (h // 2) * LANES:(h // 2 + 1) * LANES]
        return _dot(kc, qt_ref[0, h])

    m_bound = []
    bmax = jnp.zeros((1, 1), F32)
    for h in range(A_HEADS):
        qf = qt_ref[0, h].astype(F32)
        bound = jnp.sqrt(row_sum(qf * qf) * kn_ref[0:1, h:h + 1])
        m_bound.append(bound)
        bmax = jnp.maximum(bmax, jnp.max(bound, axis=1, keepdims=True))

    def exact_max(_):
        def max_body(jc, mx):
            sel = sc_ref[jc] >= tau
            return tuple(_fold_rows(jnp.where(sel, logits_t(jc, h), neg_inf), jnp.maximum, mx[h])
                         for h in range(A_HEADS))

        mx = lax.fori_loop(0, n_chunks, max_body, tuple(jnp.full((8, tq), neg_inf, F32) for _ in range(A_HEADS)))
        return tuple(row_max(m) for m in mx)

    m_ref_vals = lax.cond(bmax[0, 0] > LOGIT_BOUND_LIMIT, exact_max, lambda _: tuple(m_bound), 0)

    ones_rows = jnp.ones((16, ck), BF16)
    acc_ref[...] = jnp.zeros(acc_ref.shape, F32)

    def pv_stage(jc, h):
        lhs = jnp.concatenate([vt_ref[0, jc, h], ones_rows], axis=0)
        acc_ref[h] += _dot(lhs, p_ref[h])

    def qk_stage(jc, sel, h):
        p_ref[h] = jnp.where(sel, jnp.exp2(logits_t(jc, h) - m_ref_vals[h]), 0.0).astype(BF16)

    sel0 = sc_ref[0] >= tau
    for h in range(A_HEADS):
        qk_stage(0, sel0, h)

    def att_body(jc, carry):
        sel = sc_ref[jc] >= tau
        for h in range(A_HEADS):
            pv_stage(jc - 1, h)
            qk_stage(jc, sel, h)
        return carry

    lax.fori_loop(1, n_chunks, att_body, 0)
    for h in range(A_HEADS):
        pv_stage(n_chunks - 1, h)
    for h in range(A_HEADS):
        a = acc_ref[h]
        o_ref[0, h] = (a[:A_HEAD_DIM] * (1.0 / a[A_HEAD_DIM:A_HEAD_DIM + 1])).astype(o_ref.dtype)


def _dsa(qit, wt, ki, qt, k, vt, tril, tq, ck, topk):
    bsz, hi, di, seq = qit.shape
    nc = seq // ck
    kern = functools.partial(_dsa_kernel, tq=tq, ck=ck, topk=topk, nbis=BISECT_STEPS)
    return pl.pallas_call(
        kern,
        grid=(bsz, seq // tq),
        in_specs=[
            pl.BlockSpec((1, hi, di, tq), lambda b, i: (b, 0, 0, i)),
            pl.BlockSpec((1, 8, tq), lambda b, i: (b, 0, i)),
            pl.BlockSpec((1, seq, di), lambda b, i: (b, 0, 0)),
            pl.BlockSpec((1, A_HEADS, LANES, tq), lambda b, i: (b, 0, 0, i)),
            pl.BlockSpec((1, seq, A_WIDTH), lambda b, i: (b, 0, 0)),
            pl.BlockSpec((1, nc, A_HEADS, A_HEAD_DIM, ck), lambda b, i: (b, 0, 0, 0, 0)),
            pl.BlockSpec((ck, ck), lambda b, i: (0, 0)),
        ],
        out_specs=pl.BlockSpec((1, A_HEADS, A_HEAD_DIM, tq), lambda b, i: (b, 0, 0, i)),
        out_shape=jax.ShapeDtypeStruct((bsz, A_HEADS, A_HEAD_DIM, seq), BF16),
        scratch_shapes=[
            pltpu.VMEM((nc, ck, tq), F32),
            pltpu.VMEM((A_HEADS, A_HEAD_DIM + 16, tq), F32),
            pltpu.VMEM((A_HEADS, ck, tq), BF16),
            pltpu.VMEM((8, LANES), F32),
        ],
        compiler_params=pltpu.CompilerParams(
            dimension_semantics=("arbitrary", "arbitrary"), vmem_limit_bytes=VMEM_LIMIT_BYTES),
        name="dsa",
    )(qit, wt, ki, qt, k, vt, tril)


def _hgrn_kernel(bq_ref, bf_ref, bi_ref, bg_ref, lb_ref, go_ref, o_ref, st_ref, oi_ref, *, tl, layer):
    cs = HGRN_CHUNK
    nch = tl // cs

    @pl.when(pl.program_id(1) == 0)
    def _():
        st_ref[...] = jnp.zeros(st_ref.shape, F32)

    lbr = lb_ref[...]
    slots = [lbr[k:k + 1] for k in range(lbr.shape[0])]
    mx = functools.reduce(jnp.maximum, slots)
    es = [jnp.exp(s - mx) for s in slots]
    lb = functools.reduce(jnp.add, es[:layer + 1]) / functools.reduce(jnp.add, es)

    bq = bq_ref[...]
    q = bq * _sigmoid(bq)
    f = lb + (1.0 - lb) * _sigmoid(bf_ref[...])
    kk = 1.0 - f
    g = jnp.log(f)
    v = bi_ref[...]

    r_i = lax.broadcasted_iota(I32, (tl, tl), 0)
    c_i = lax.broadcasted_iota(I32, (tl, tl), 1)
    same = (r_i >> _log2(cs)) == (c_i >> _log2(cs))
    tri = jnp.where(same & (c_i <= r_i), 1.0, 0.0).astype(BF16)
    blk = jnp.where(same, 1.0, 0.0).astype(BF16)
    g1, g2, g3 = _split3(g)
    b = _dot(tri, g1) + _dot(tri, g2) + _dot(tri, g3)
    bend = _dot(blk, g1) + _dot(blk, g2) + _dot(blk, g3)

    bk = b - jnp.log(kk)
    pos = lax.broadcasted_iota(I32, (tl, B_WIDTH), 0) & (cs - 1)
    o = jnp.zeros((tl, B_WIDTH), F32)
    for r in range(cs):
        if r == 0:
            bk_s, v_s = bk, v
        else:
            bk_s = pltpu.roll(bk, r, axis=0)
            v_s = pltpu.roll(v, r, axis=0)
        e = q * jnp.exp(jnp.where(pos >= r, b - bk_s, -jnp.inf))
        parts = []
        for h in range(B_HEADS):
            sl = slice(h * B_HEAD_DIM, (h + 1) * B_HEAD_DIM)
            parts.append(jnp.sum(e[:, sl], axis=-1, keepdims=True) * v_s[:, sl])
        o = o + jnp.concatenate(parts, axis=1)

    qe = (q * jnp.exp(b)).astype(BF16)
    kd = (kk * jnp.exp(bend - b)).astype(BF16)
    vb = v.astype(BF16)
    dec = jnp.exp(bend)
    for c in range(nch):
        rs = slice(c * cs, (c + 1) * cs)
        for h in range(B_HEADS):
            sl = slice(h * B_HEAD_DIM, (h + 1) * B_HEAD_DIM)
            st = st_ref[h]
            oi_ref[rs, sl] = lax.dot_general(qe[rs, sl], st.astype(BF16), (((1,), (1,)), ((), ())),
                                             preferred_element_type=F32)
            upd = lax.dot_general(vb[rs, sl], kd[rs, sl], (((0,), (0,)), ((), ())),
                                  preferred_element_type=F32)
            st_ref[h] = st * dec[c * cs:c * cs + 1, sl] + upd
    o = o + oi_ref[...]

    parts = []
    for h in range(B_HEADS):
        sl = slice(h * B_HEAD_DIM, (h + 1) * B_HEAD_DIM)
        oh = o[:, sl]
        parts.append(oh * lax.rsqrt(jnp.mean(oh * oh, axis=-1, keepdims=True) + EPS))
    on = jnp.concatenate(parts, axis=1) * go_ref[...]
    bg = bg_ref[...]
    o_ref[...] = (on * (bg * _sigmoid(bg))).astype(o_ref.dtype)


def _hgrn(proj, lb_table, go, bsz, seq, tl, layer):
    m = proj.shape[0]
    tpb = seq // tl
    col = lambda c: (lambda b, t: (b * tpb + t, c // B_WIDTH))
    return pl.pallas_call(
        functools.partial(_hgrn_kernel, tl=tl, layer=layer),
        grid=(bsz, tpb),
        in_specs=[
            pl.BlockSpec((tl, B_WIDTH), col(COL_BQ)),
            pl.BlockSpec((tl, B_WIDTH), col(COL_BF)),
            pl.BlockSpec((tl, B_WIDTH), col(COL_BI)),
            pl.BlockSpec((tl, B_WIDTH), col(COL_BG)),
            pl.BlockSpec(lb_table.shape, lambda b, t: (0, 0)),
            pl.BlockSpec((1, B_WIDTH), lambda b, t: (0, 0)),
        ],
        out_specs=pl.BlockSpec((tl, B_WIDTH), lambda b, t: (b * tpb + t, 0)),
        out_shape=jax.ShapeDtypeStruct((m, B_WIDTH), BF16),
        scratch_shapes=[
            pltpu.VMEM((B_HEADS, B_HEAD_DIM, B_HEAD_DIM), F32),
            pltpu.VMEM((tl, B_WIDTH), F32),
        ],
        compiler_params=pltpu.CompilerParams(
            dimension_semantics=("arbitrary", "arbitrary"), vmem_limit_bytes=VMEM_LIMIT_BYTES),
        name="hgrn",
    )(proj, proj, proj, proj, lb_table, go)


def _merge_kernel(oat_ref, ob_ref, ga_ref, gb_ref, x_ref, g1_ref, wa_ref, wb_ref, wo_ref, o_ref):
    tm = x_ref.shape[0]
    oa = oat_ref[0].reshape(A_WIDTH, tm).astype(F32).T.astype(BF16)
    pa = _dot(oa, wa_ref[...])
    pb = _dot(ob_ref[...], wb_ref[...])
    merged = _sigmoid(ga_ref[...]) * pa + _sigmoid(gb_ref[...]) * pb
    y = _dot(merged.astype(BF16), wo_ref[...])
    o_ref[...] = x_ref[...] + g1_ref[0] * y


def _merge(oat, ob, proj, x2, gate1, wa, wb, wo, seq, tm):
    m, d = x2.shape
    tpb = seq // tm
    row = lambda i: (i, 0)
    const = lambda i: (0, 0)
    return pl.pallas_call(
        _merge_kernel,
        grid=(m // tm,),
        in_specs=[
            pl.BlockSpec((1, A_HEADS, A_HEAD_DIM, tm), lambda i: (i // tpb, 0, 0, i % tpb)),
            pl.BlockSpec((tm, B_WIDTH), row),
            pl.BlockSpec((tm, d), lambda i: (i, COL_GA // d)),
            pl.BlockSpec((tm, d), lambda i: (i, COL_GB // d)),
            pl.BlockSpec((tm, d), row),
            pl.BlockSpec((1, 1, d), lambda i: (i // tpb, 0, 0)),
            pl.BlockSpec(wa.shape, const),
            pl.BlockSpec(wb.shape, const),
            pl.BlockSpec(wo.shape, const),
        ],
        out_specs=pl.BlockSpec((tm, d), row),
        out_shape=jax.ShapeDtypeStruct((m, d), F32),
        compiler_params=pltpu.CompilerParams(vmem_limit_bytes=VMEM_LIMIT_BYTES),
        name="merge",
    )(oat, ob, proj, proj, x2, gate1, wa, wb, wo)


def _mlp_kernel(x_ref, g_ref, sc_ref, sh_ref, g2_ref, w1_ref, b1_ref, w2_ref, b2_ref, o_ref, *, tf):
    x = x_ref[...]
    ms = jnp.mean(x * x, axis=-1, keepdims=True)
    h = x * lax.rsqrt(ms + EPS) * g_ref[...]
    h = (h * (1.0 + sc_ref[0]) + sh_ref[0]).astype(BF16)
    dff = w1_ref.shape[1]
    y = jnp.zeros(x.shape, F32)
    for c in range(dff // tf):
        cs = slice(c * tf, (c + 1) * tf)
        a = jnp.maximum(_dot(h, w1_ref[:, cs]) + b1_ref[:, cs], 0.0)
        y = y + _dot((a * a).astype(BF16), w2_ref[cs, :])
    o_ref[...] = x + g2_ref[0] * (y + b2_ref[...])


def _mlp(x1, g, scale, shift, gate2, w1, b1, w2, b2, seq, tm, tf):
    m, d = x1.shape
    tpb = seq // tm
    row = lambda i: (i, 0)
    const = lambda i: (0, 0)
    bat = lambda i: (i // tpb, 0, 0)
    return pl.pallas_call(
        functools.partial(_mlp_kernel, tf=tf),
        grid=(m // tm,),
        in_specs=[
            pl.BlockSpec((tm, d), row),
            pl.BlockSpec((1, d), const),
            pl.BlockSpec((1, 1, d), bat),
            pl.BlockSpec((1, 1, d), bat),
            pl.BlockSpec((1, 1, d), bat),
            pl.BlockSpec(w1.shape, const),
            pl.BlockSpec(b1.shape, const),
            pl.BlockSpec(w2.shape, const),
            pl.BlockSpec(b2.shape, const),
        ],
        out_specs=pl.BlockSpec((tm, d), row),
        out_shape=jax.ShapeDtypeStruct((m, d), F32),
        compiler_params=pltpu.CompilerParams(vmem_limit_bytes=VMEM_LIMIT_BYTES),
        name="mlp",
    )(x1, g, scale, shift, gate2, w1, b1, w2, b2)


def _regroup_w_in(w):
    w = w.astype(BF16)
    d = w.shape[0]
    sizes = (Q_LORA_RANK, A_WIDTH, A_WIDTH, IDX_DIM, IDX_HEADS, B_WIDTH, B_WIDTH, B_WIDTH, B_WIDTH, d, d)
    offs = [0]
    for s in sizes:
        offs.append(offs[-1] + s)
    part = lambda k: w[:, offs[k]:offs[k + 1]]
    q_lat, a_k, a_v, k_idx, w_idx, b_q, b_f, b_i, b_g, gate_a, gate_b = (part(k) for k in range(11))
    pad = jnp.zeros((d, LANES - IDX_DIM - IDX_HEADS), w.dtype)
    out = jnp.concatenate([a_k, a_v, b_q, b_f, b_i, b_g, gate_a, gate_b, q_lat, k_idx, w_idx, pad], axis=1)
    assert out.shape[1] == PROJ_COLS
    return out


def _layer(x, mod, l, p, tiles):
    bsz, seq, d = x.shape
    m = bsz * seq
    shift1, scale1, gate1, shift2, scale2, gate2 = (mod[:, k * d:(k + 1) * d].reshape(bsz, 1, d) for k in range(6))
    x2 = x.reshape(m, d)

    proj = _inproj(x2, p['norm1_g'][l][None], scale1, shift1, _regroup_w_in(p['w_in'][l]), seq, tiles['tm_in'])

    tq, ck = tiles['tq'], tiles['ck']
    wuq = jnp.concatenate([p['w_uq'][l], p['w_uq_idx'][l]], axis=1).astype(BF16)
    gki = jnp.concatenate([p['k_idx_norm_g'][l], jnp.zeros((LANES - IDX_DIM,), F32)])[None]
    qt, ak, qit, ki, wt, vt = _qkprep(
        proj, p['q_lat_norm_g'][l][None], wuq,
        jnp.tile(p['q_norm_g'][l], A_HEADS)[None], jnp.tile(p['k_norm_g'][l], A_HEADS)[None], gki, bsz, seq, ck)

    topk = min(TOPK_MAX, seq // 4)
    tril = jnp.tril(jnp.ones((ck, ck), BF16))
    out_at = _dsa(qit, wt, ki.reshape(bsz, seq, IDX_DIM), qt, ak.reshape(bsz, seq, A_WIDTH), vt, tril, tq, ck, topk)

    out_b = _hgrn(proj, p['hgrn_lb'], p['hgrn_o_norm_g'][l][None], bsz, seq, tiles['tl'], l)

    x1 = _merge(out_at, out_b, proj, x2, gate1, p['w_proj_a'][l].astype(BF16), p['w_proj_b'][l].astype(BF16),
                p['w_out'][l].astype(BF16), seq, tiles['tm'])
    out = _mlp(x1, p['norm2_g'][l][None], scale2, shift2, gate2, p['w_mlp1'][l].astype(BF16), p['b_mlp1'][l][None],
               p['w_mlp2'][l].astype(BF16), p['b_mlp2'][l][None], seq, tiles['tm'], tiles['tf'])
    return out.reshape(bsz, seq, d)


def _tiles(seq):
    pick = lambda want: min(want, seq)
    return dict(tm_in=pick(256), tm=pick(512), tq=pick(256), ck=pick(512), tl=pick(256), tf=1024)


def kernel(x, c, w_ada, b_ada, norm1_g, w_in, q_lat_norm_g, w_uq, w_uq_idx, q_norm_g, k_norm_g, k_idx_norm_g,
           hgrn_lb, hgrn_o_norm_g, w_proj_a, w_proj_b, w_out, norm2_g, w_mlp1, b_mlp1, w_mlp2, b_mlp2):
    p = dict(norm1_g=norm1_g, w_in=w_in, q_lat_norm_g=q_lat_norm_g, w_uq=w_uq, w_uq_idx=w_uq_idx,
             q_norm_g=q_norm_g, k_norm_g=k_norm_g, k_idx_norm_g=k_idx_norm_g, hgrn_lb=hgrn_lb,
             hgrn_o_norm_g=hgrn_o_norm_g, w_proj_a=w_proj_a, w_proj_b=w_proj_b, w_out=w_out, norm2_g=norm2_g,
             w_mlp1=w_mlp1, b_mlp1=b_mlp1, w_mlp2=w_mlp2, b_mlp2=b_mlp2)
    bsz, seq, d = x.shape
    depth = w_ada.shape[0]
    tiles = _tiles(seq)
    c_pad = jnp.zeros((8, d), F32).at[:bsz].set(c)
    for l in range(depth):
        mod = _adaln(c_pad, w_ada[l], b_ada[l][None])[:bsz]
        x = _layer(x, mod, l, p, tiles)
    return x
```

```python
import functools

import jax
import jax.numpy as jnp
from jax import lax
from jax.experimental import pallas as pl
from jax.experimental.pallas import tpu as pltpu

F32 = jnp.float32
BF16 = jnp.bfloat16
I32 = jnp.int32

EPS = 1e-6
A_HEADS = 8
A_HEAD_DIM = 64
A_WIDTH = A_HEADS * A_HEAD_DIM
Q_LORA_RANK = 256
IDX_HEADS = 4
IDX_DIM = 64
TOPK_MAX = 256
B_WIDTH = 512
B_HEADS = 4
B_HEAD_DIM = 128
HGRN_CHUNK = 16

LANES = 128
VMEM_LIMIT_BYTES = 56 * 1024 * 1024

COL_AK, COL_AV, COL_BQ, COL_BF, COL_BI, COL_BG = 0, 512, 1024, 1536, 2048, 2560
COL_GA, COL_GB, COL_QL, COL_KW = 3072, 4096, 5120, 5376
PROJ_COLS = 5504

LOG2E = 1.4426950408889634
BISECT_STEPS = 12
SELECT_ACC_ROWS = 32
LOGIT_BOUND_LIMIT = 60.0


def _sigmoid(x):
    return 0.5 * jnp.tanh(0.5 * x) + 0.5


def _split2(x):
    hi = x.astype(BF16)
    lo = (x - hi.astype(F32)).astype(BF16)
    return hi, lo


def _split3(x):
    a = x.astype(BF16)
    r = x - a.astype(F32)
    b = r.astype(BF16)
    c = (r - b.astype(F32)).astype(BF16)
    return a, b, c


def _dot(a, b):
    return jnp.dot(a, b, preferred_element_type=F32)


def _log2(n):
    assert n > 0 and n & (n - 1) == 0, n
    return n.bit_length() - 1


def _group_ones(n, group):
    r = lax.broadcasted_iota(I32, (n, n), 0) >> _log2(group)
    c = lax.broadcasted_iota(I32, (n, n), 1) >> _log2(group)
    return jnp.where(r == c, 1.0, 0.0).astype(BF16)


def _group_mean_sq(x, ones_bd, group):
    hi, lo = _split2(x * x)
    return (_dot(hi, ones_bd) + _dot(lo, ones_bd)) * (1.0 / group)


def _fold_rows(x, op, init):
    r = init.shape[0]
    for s in range(x.shape[0] // r):
        init = op(init, x[s * r:(s + 1) * r, :])
    return init


def _adaln_kernel(c_ref, w_ref, b_ref, o_ref):
    c = c_ref[...]
    a = c * _sigmoid(c)
    a1, a2, a3 = _split3(a)
    w1, w2, w3 = _split3(w_ref[...])
    acc = _dot(a1, w1) + (_dot(a1, w2) + _dot(a2, w1)) + (_dot(a2, w2) + _dot(a1, w3) + _dot(a3, w1))
    o_ref[...] = acc + b_ref[...]


def _adaln(c_pad, w, b):
    rows, d = c_pad.shape
    n = w.shape[1]
    tn = 1536
    return pl.pallas_call(
        _adaln_kernel,
        grid=(n // tn,),
        in_specs=[
            pl.BlockSpec((rows, d), lambda j: (0, 0)),
            pl.BlockSpec((d, tn), lambda j: (0, j)),
            pl.BlockSpec((1, tn), lambda j: (0, j)),
        ],
        out_specs=pl.BlockSpec((rows, tn), lambda j: (0, j)),
        out_shape=jax.ShapeDtypeStruct((rows, n), F32),
        compiler_params=pltpu.CompilerParams(vmem_limit_bytes=VMEM_LIMIT_BYTES),
        name="adaln",
    )(c_pad, w, b)


def _inproj_kernel(x_ref, g_ref, sc_ref, sh_ref, w_ref, o_ref):
    x = x_ref[...]
    ms = jnp.mean(x * x, axis=-1, keepdims=True)
    h = x * lax.rsqrt(ms + EPS) * g_ref[...]
    h = h * (1.0 + sc_ref[0]) + sh_ref[0]
    o_ref[...] = _dot(h.astype(BF16), w_ref[...])


def _inproj(x2, g, scale, shift, w, seq, tm):
    m, d = x2.shape
    n = w.shape[1]
    tpb = seq // tm
    return pl.pallas_call(
        _inproj_kernel,
        grid=(m // tm,),
        in_specs=[
            pl.BlockSpec((tm, d), lambda i: (i, 0)),
            pl.BlockSpec((1, d), lambda i: (0, 0)),
            pl.BlockSpec((1, 1, d), lambda i: (i // tpb, 0, 0)),
            pl.BlockSpec((1, 1, d), lambda i: (i // tpb, 0, 0)),
            pl.BlockSpec((d, n), lambda i: (0, 0)),
        ],
        out_specs=pl.BlockSpec((tm, n), lambda i: (i, 0)),
        out_shape=jax.ShapeDtypeStruct((m, n), F32),
        compiler_params=pltpu.CompilerParams(vmem_limit_bytes=VMEM_LIMIT_BYTES),
        name="inproj",
    )(x2, g, scale, shift, w)


def _qkprep_kernel(ak_ref, av_ref, ql_ref, kw_ref, gql_ref, wuq_ref, gq_ref, gk_ref, gki_ref,
                   qt_out, ak_out, qit_out, ki_out, wt_out, vt_out):
    tm = ak_ref.shape[0]
    ones64 = _group_ones(A_WIDTH, A_HEAD_DIM)
    ql = ql_ref[...]
    ql = ql * lax.rsqrt(jnp.mean(ql * ql, axis=-1, keepdims=True) + EPS) * gql_ref[...]
    up = _dot(ql.astype(BF16), wuq_ref[...])
    aq = up[:, :A_WIDTH]
    aq = aq * lax.rsqrt(_group_mean_sq(aq, ones64, A_HEAD_DIM) + EPS) * gq_ref[...]
    aq_t = (aq * (A_HEAD_DIM ** -0.5 * LOG2E)).T
    zero_half = jnp.zeros((A_HEAD_DIM, tm), BF16)
    for h in range(A_HEADS):
        e = h % 2
        qt_out[0, h, e * A_HEAD_DIM:(e + 1) * A_HEAD_DIM, :] = aq_t[h * A_HEAD_DIM:(h + 1) * A_HEAD_DIM].astype(BF16)
        qt_out[0, h, (1 - e) * A_HEAD_DIM:(2 - e) * A_HEAD_DIM, :] = zero_half
    qit_out[0] = up[:, A_WIDTH:].T.reshape(IDX_HEADS, IDX_DIM, tm).astype(BF16)
    ak = ak_ref[...]
    ak = ak * lax.rsqrt(_group_mean_sq(ak, ones64, A_HEAD_DIM) + EPS) * gk_ref[...]
    ak_out[...] = ak.astype(BF16)
    vt_out[0, 0] = av_ref[...].T.reshape(A_HEADS, A_HEAD_DIM, tm).astype(BF16)
    kw = kw_ref[...]
    lane = lax.broadcasted_iota(I32, kw.shape, 1)
    ksq = jnp.where(lane < IDX_DIM, kw * kw, 0.0)
    kms = jnp.sum(ksq, axis=-1, keepdims=True) * (1.0 / IDX_DIM)
    kn = kw * lax.rsqrt(kms + EPS) * gki_ref[...]
    ki_out[...] = kn[:, :IDX_DIM].astype(BF16)
    idx_scale = (IDX_DIM ** -0.5) * (IDX_HEADS ** -0.5)
    w_rows = jnp.where((lane >= IDX_DIM) & (lane < IDX_DIM + IDX_HEADS), kw * idx_scale, 0.0).T
    wt_out[0] = w_rows[IDX_DIM:IDX_DIM + 8]


def _qkprep(proj, gql, wuq, gq, gk, gki, bsz, seq, tm):
    m = proj.shape[0]
    tpb = seq // tm
    row = lambda i: (i, 0)
    const = lambda i: (0, 0)
    return pl.pallas_call(
        _qkprep_kernel,
        grid=(m // tm,),
        in_specs=[
            pl.BlockSpec((tm, A_WIDTH), lambda i: (i, COL_AK // A_WIDTH)),
            pl.BlockSpec((tm, A_WIDTH), lambda i: (i, COL_AV // A_WIDTH)),
            pl.BlockSpec((tm, Q_LORA_RANK), lambda i: (i, COL_QL // Q_LORA_RANK)),
            pl.BlockSpec((tm, LANES), lambda i: (i, COL_KW // LANES)),
            pl.BlockSpec((1, Q_LORA_RANK), const),
            pl.BlockSpec(wuq.shape, const),
            pl.BlockSpec((1, A_WIDTH), const),
            pl.BlockSpec((1, A_WIDTH), const),
            pl.BlockSpec((1, LANES), const),
        ],
        out_specs=[
            pl.BlockSpec((1, A_HEADS, LANES, tm), lambda i: (i // tpb, 0, 0, i % tpb)),
            pl.BlockSpec((tm, A_WIDTH), row),
            pl.BlockSpec((1, IDX_HEADS, IDX_DIM, tm), lambda i: (i // tpb, 0, 0, i % tpb)),
            pl.BlockSpec((tm, IDX_DIM), row),
            pl.BlockSpec((1, 8, tm), lambda i: (i // tpb, 0, i % tpb)),
            pl.BlockSpec((1, 1, A_HEADS, A_HEAD_DIM, tm), lambda i: (i // tpb, i % tpb, 0, 0, 0)),
        ],
        out_shape=[
            jax.ShapeDtypeStruct((bsz, A_HEADS, LANES, seq), BF16),
            jax.ShapeDtypeStruct((m, A_WIDTH), BF16),
            jax.ShapeDtypeStruct((bsz, IDX_HEADS, IDX_DIM, seq), BF16),
            jax.ShapeDtypeStruct((m, IDX_DIM), BF16),
            jax.ShapeDtypeStruct((bsz, 8, seq), F32),
            jax.ShapeDtypeStruct((bsz, tpb, A_HEADS, A_HEAD_DIM, tm), BF16),
        ],
        compiler_params=pltpu.CompilerParams(vmem_limit_bytes=VMEM_LIMIT_BYTES),
        name="qkprep",
    )(proj, proj, proj, proj, gql, wuq, gq, gk, gki)


def _dsa_kernel(qit_ref, wt_ref, ki_ref, qt_ref, k_ref, vt_ref, tril_ref, o_ref,
                sc_ref, acc_ref, p_ref, kn_ref, *, tq, ck, topk, nbis):
    i = pl.program_id(1)
    n_chunks = ((i + 1) * tq + ck - 1) // ck
    kf = float(topk)
    neg_inf = float("-inf")
    pos_inf = float("inf")
    ar = SELECT_ACC_ROWS
    full_acc = lambda val: jnp.full((ar, tq), val, F32)
    row_min = lambda x: jnp.min(x, axis=0, keepdims=True)
    row_max = lambda x: jnp.max(x, axis=0, keepdims=True)
    row_sum = lambda x: jnp.sum(x, axis=0, keepdims=True)
    chunk_rows = lambda jc: pl.ds(pl.multiple_of(jc * ck, ck), ck)

    @pl.when(i == 0)
    def _():
        head_of_lane = lax.broadcasted_iota(I32, (A_WIDTH, LANES), 0) >> _log2(A_HEAD_DIM)
        pick = jnp.where(head_of_lane == lax.broadcasted_iota(I32, (A_WIDTH, LANES), 1), 1.0, 0.0).astype(BF16)

        def kn_body(jc, mx):
            k = k_ref[0, chunk_rows(jc), :].astype(F32)
            return _fold_rows(_dot((k * k).astype(BF16), pick), jnp.maximum, mx)

        mx = lax.fori_loop(0, k_ref.shape[1] // ck, kn_body, jnp.zeros((8, LANES), F32))
        kn_ref[...] = jnp.broadcast_to(row_max(mx), (8, LANES))

    t_pos = i * tq + lax.broadcasted_iota(I32, (ck, tq), 1)
    s_iota = lax.broadcasted_iota(I32, (ck, tq), 0)

    def score_body(masked, jc, carry):
        mn, mx = carry
        kc = ki_ref[0, chunk_rows(jc), :]
        score = jnp.zeros((ck, tq), F32)
        for h in range(IDX_HEADS):
            score = score + wt_ref[0, h:h + 1, :] * jnp.maximum(_dot(kc, qit_ref[0, h]), 0.0)
        if masked:
            causal = (s_iota + jc * ck) <= t_pos
            sc_ref[jc] = jnp.where(causal, score, neg_inf)
            mn = _fold_rows(jnp.where(causal, score, pos_inf), jnp.minimum, mn)
            mx = _fold_rows(jnp.where(causal, score, neg_inf), jnp.maximum, mx)
        else:
            sc_ref[jc] = score
            mn = _fold_rows(score, jnp.minimum, mn)
            mx = _fold_rows(score, jnp.maximum, mx)
        return mn, mx

    n_full = (i * tq + 1) // ck
    carry = lax.fori_loop(0, n_full, functools.partial(score_body, False), (full_acc(pos_inf), full_acc(neg_inf)))
    mn_acc, mx_acc = lax.fori_loop(n_full, n_chunks, functools.partial(score_body, True), carry)
    rmin, rmax = row_min(mn_acc), row_max(mx_acc)

    def count_ge(cand):
        cand_b = jnp.broadcast_to(cand, (ar, tq))

        def body(jc, acc):
            return _fold_rows(sc_ref[jc], lambda a, x: a + jnp.where(x >= cand_b, 1.0, 0.0), acc)

        return row_sum(lax.fori_loop(0, n_chunks, body, full_acc(0.0)))

    def snap(lo, hi):
        lo_b = jnp.broadcast_to(lo, (ar, tq))
        hi_b = jnp.broadcast_to(hi, (ar, tq))

        def body(jc, carry):
            x = sc_ref[jc]
            a = _fold_rows(x, lambda a, x: jnp.minimum(a, jnp.where(x >= lo_b, x, pos_inf)), carry[0])
            b = _fold_rows(x, lambda b, x: jnp.maximum(b, jnp.where(x < hi_b, x, neg_inf)), carry[1])
            return a, b

        a, b = lax.fori_loop(0, n_chunks, body, (full_acc(pos_inf), full_acc(neg_inf)))
        return row_min(a), row_max(b)

    def probe(cand):
        cand_b = jnp.broadcast_to(cand, (ar, tq))

        def body(jc, carry):
            cnt, a, b = carry
            blk = sc_ref[jc]
            for s in range(ck // ar):
                x = blk[s * ar:(s + 1) * ar, :]
                ge = x >= cand_b
                cnt = cnt + jnp.where(ge, 1.0, 0.0)
                a = jnp.minimum(a, jnp.where(ge, x, pos_inf))
                b = jnp.maximum(b, jnp.where(ge, neg_inf, x))
            return cnt, a, b

        cnt, a, b = lax.fori_loop(0, n_chunks, body, (full_acc(0.0), full_acc(pos_inf), full_acc(neg_inf)))
        return row_sum(cnt), row_min(a), row_max(b)

    n_valid = (i * tq + 1 + lax.broadcasted_iota(I32, (1, tq), 1)).astype(F32)
    small = n_valid <= kf

    def interpolate(lo, hi, c_lo, c_hi, w_lo, w_hi):
        f_lo = (c_lo - kf + 0.5) * w_lo
        f_hi = (kf - c_hi - 0.5) * w_hi
        return lo + (hi - lo) * jnp.clip(f_lo / jnp.maximum(f_lo + f_hi, 0.5), 0.1, 0.9)

    def search_body(_, st):
        lo, hi, c_lo, c_hi, w_lo, w_hi, last = st
        mid = interpolate(lo, hi, c_lo, c_hi, w_lo, w_hi)
        ok = (mid > lo) & (mid < hi)
        c = count_ge(mid)
        up = ok & (c >= kf)
        dn = ok & (c < kf)
        w_hi = jnp.where(up, jnp.where(last > 0.0, w_hi * 0.5, 1.0), jnp.where(dn, 1.0, w_hi))
        w_lo = jnp.where(dn, jnp.where(last < 0.0, w_lo * 0.5, 1.0), jnp.where(up, 1.0, w_lo))
        last = jnp.where(up, 1.0, jnp.where(dn, -1.0, last))
        return (jnp.where(up, mid, lo), jnp.where(dn, mid, hi), jnp.where(up, c, c_lo), jnp.where(dn, c, c_hi),
                w_lo, w_hi, last)

    hi0 = rmax + jnp.maximum(jnp.abs(rmax) * 1e-6, 1e-30)
    zeros_q, ones_q = jnp.zeros((1, tq), F32), jnp.ones((1, tq), F32)
    lo, hi, c_lo, c_hi, _, _, _ = lax.fori_loop(0, nbis, search_body,
                                                (rmin, hi0, n_valid, zeros_q, ones_q, ones_q, zeros_q))

    lo, hi = snap(lo, hi)

    def active_of(lo, hi, c_lo):
        return jnp.logical_not(small) & (lo < hi) & (c_lo != kf)

    def snap_body(st):
        lo, hi, c_lo, c_hi, _ = st
        act = active_of(lo, hi, c_lo)
        mid = interpolate(lo, hi, c_lo, c_hi, 1.0, 1.0)
        mid = jnp.where((c_hi == kf - 1.0) | (mid <= lo) | (mid > hi), hi, mid)
        c, a, b = probe(mid)
        up = act & (c >= kf)
        dn = act & (c < kf)
        lo, c_lo = jnp.where(up, a, lo), jnp.where(up, c, c_lo)
        hi, c_hi = jnp.where(dn, b, hi), jnp.where(dn, c, c_hi)
        return lo, hi, c_lo, c_hi, jnp.max(jnp.where(active_of(lo, hi, c_lo), 1.0, 0.0))

    flag0 = jnp.max(jnp.where(active_of(lo, hi, c_lo), 1.0, 0.0))
    lo, hi, c_lo, c_hi, _ = lax.while_loop(lambda st: st[4] > 0.0, snap_body, (lo, hi, c_lo, c_hi, flag0))

    tau = jnp.where(small, rmin, lo)
    excess = jnp.logical_not(small) & (lo == hi) & (c_lo > kf)
    need = jnp.where(excess, kf - c_hi, 4.0 * 65536.0 * 65536.0)

    @pl.when(jnp.max(jnp.where(excess, 1.0, 0.0)) > 0.0)
    def _():
        def tie_body(jc, carry):
            blk = sc_ref[jc]
            eq = blk == tau
            pc = _dot(tril_ref[...], jnp.where(eq, 1.0, 0.0).astype(BF16)) + carry
            sc_ref[jc] = jnp.where(eq & (pc > need), neg_inf, blk)
            return pc[ck - 1:ck, :]

        lax.fori_loop(0, n_chunks, tie_body, jnp.zeros((1, tq), F32))

    def logits_t(jc, h):
        kc = k_ref[0, chunk_rows(jc), (h // 2) * LANES:(h // 2 + 1) * LANES]
        return _dot(kc, qt_ref[0, h])

    m_bound = []
    bmax = jnp.zeros((1, 1), F32)
    for h in range(A_HEADS):
        qf = qt_ref[0, h].astype(F32)
        bound = jnp.sqrt(row_sum(qf * qf) * kn_ref[0:1, h:h + 1])
        m_bound.append(bound)
        bmax = jnp.maximum(bmax, jnp.max(bound, axis=1, keepdims=True))

    def exact_max(_):
        def max_body(jc, mx):
            sel = sc_ref[jc] >= tau
            return tuple(_fold_rows(jnp.where(sel, logits_t(jc, h), neg_inf), jnp.maximum, mx[h])
                         for h in range(A_HEADS))

        mx = lax.fori_loop(0, n_chunks, max_body, tuple(jnp.full((8, tq), neg_inf, F32) for _ in range(A_HEADS)))
        return tuple(row_max(m) for m in mx)

    m_ref_vals = lax.cond(bmax[0, 0] > LOGIT_BOUND_LIMIT, exact_max, lambda _: tuple(m_bound), 0)

    ones_rows = jnp.ones((16, ck), BF16)
    acc_ref[...] = jnp.zeros(acc_ref.shape, F32)

    def pv_stage(jc, h):
        lhs = jnp.concatenate([vt_ref[0, jc, h], ones_rows], axis=0)
        acc_ref[h] += _dot(lhs, p_ref[h])

    def qk_stage(jc, sel, h):
        p_ref[h] = jnp.where(sel, jnp.exp2(logits_t(jc, h) - m_ref_vals[h]), 0.0).astype(BF16)

    sel0 = sc_ref[0] >= tau
    for h in range(A_HEADS):
        qk_stage(0, sel0, h)

    def att_body(jc, carry):
        sel = sc_ref[jc] >= tau
        for h in range(A_HEADS):
            pv_stage(jc - 1, h)
            qk_stage(jc, sel, h)
        return carry

    lax.fori_loop(1, n_chunks, att_body, 0)
    for h in range(A_HEADS):
        pv_stage(n_chunks - 1, h)
    for h in range(A_HEADS):
        a = acc_ref[h]
        o_ref[0, h] = (a[:A_HEAD_DIM] * (1.0 / a[A_HEAD_DIM:A_HEAD_DIM + 1])).astype(o_ref.dtype)


def _dsa(qit, wt, ki, qt, k, vt, tril, tq, ck, topk):
    bsz, hi, di, seq = qit.shape
    nc = seq // ck
    kern = functools.partial(_dsa_kernel, tq=tq, ck=ck, topk=topk, nbis=BISECT_STEPS)
    return pl.pallas_call(
        kern,
        grid=(bsz, seq // tq),
        in_specs=[
            pl.BlockSpec((1, hi, di, tq), lambda b, i: (b, 0, 0, i)),
            pl.BlockSpec((1, 8, tq), lambda b, i: (b, 0, i)),
            pl.BlockSpec((1, seq, di), lambda b, i: (b, 0, 0)),
            pl.BlockSpec((1, A_HEADS, LANES, tq), lambda b, i: (b, 0, 0, i)),
            pl.BlockSpec((1, seq, A_WIDTH), lambda b, i: (b, 0, 0)),
            pl.BlockSpec((1, nc, A_HEADS, A_HEAD_DIM, ck), lambda b, i: (b, 0, 0, 0, 0)),
            pl.BlockSpec((ck, ck), lambda b, i: (0, 0)),
        ],
        out_specs=pl.BlockSpec((1, A_HEADS, A_HEAD_DIM, tq), lambda b, i: (b, 0, 0, i)),
        out_shape=jax.ShapeDtypeStruct((bsz, A_HEADS, A_HEAD_DIM, seq), BF16),
        scratch_shapes=[
            pltpu.VMEM((nc, ck, tq), F32),
            pltpu.VMEM((A_HEADS, A_HEAD_DIM + 16, tq), F32),
            pltpu.VMEM((A_HEADS, ck, tq), BF16),
            pltpu.VMEM((8, LANES), F32),
        ],
        compiler_params=pltpu.CompilerParams(
            dimension_semantics=("arbitrary", "arbitrary"), vmem_limit_bytes=VMEM_LIMIT_BYTES),
        name="dsa",
    )(qit, wt, ki, qt, k, vt, tril)


def _hgrn_kernel(bq_ref, bf_ref, bi_ref, bg_ref, lb_ref, go_ref, o_ref, st_ref, oi_ref, *, tl, layer):
    cs = HGRN_CHUNK
    nch = tl // cs

    @pl.when(pl.program_id(1) == 0)
    def _():
        st_ref[...] = jnp.zeros(st_ref.shape, F32)

    lbr = lb_ref[...]
    slots = [lbr[k:k + 1] for k in range(lbr.shape[0])]
    mx = functools.reduce(jnp.maximum, slots)
    es = [jnp.exp(s - mx) for s in slots]
    lb = functools.reduce(jnp.add, es[:layer + 1]) / functools.reduce(jnp.add, es)

    bq = bq_ref[...]
    q = bq * _sigmoid(bq)
    f = lb + (1.0 - lb) * _sigmoid(bf_ref[...])
    kk = 1.0 - f
    g = jnp.log(f)
    v = bi_ref[...]

    r_i = lax.broadcasted_iota(I32, (tl, tl), 0)
    c_i = lax.broadcasted_iota(I32, (tl, tl), 1)
    same = (r_i >> _log2(cs)) == (c_i >> _log2(cs))
    tri = jnp.where(same & (c_i <= r_i), 1.0, 0.0).astype(BF16)
    blk = jnp.where(same, 1.0, 0.0).astype(BF16)
    half = cs // 2
    first_half_col = (c_i & (cs - 1)) < half
    mid = jnp.where(same & first_half_col, 1.0, 0.0).astype(BF16)
    g1, g2, g3 = _split3(g)
    b = _dot(tri, g1) + _dot(tri, g2) + _dot(tri, g3)
    bend = _dot(blk, g1) + _dot(blk, g2) + _dot(blk, g3)
    bmid = _dot(mid, g1) + _dot(mid, g2) + _dot(mid, g3)
    vb = v.astype(BF16)

    bk = b - jnp.log(kk)
    pos = lax.broadcasted_iota(I32, (tl, B_WIDTH), 0) & (cs - 1)
    pos_half = pos & (half - 1)
    o = jnp.zeros((tl, B_WIDTH), F32)
    for r in range(half):
        if r == 0:
            bk_s, v_s = bk, v
        else:
            bk_s = pltpu.roll(bk, r, axis=0)
            v_s = pltpu.roll(v, r, axis=0)
        e = q * jnp.exp(jnp.where(pos_half >= r, b - bk_s, -jnp.inf))
        parts = []
        for h in range(B_HEADS):
            sl = slice(h * B_HEAD_DIM, (h + 1) * B_HEAD_DIM)
            parts.append(jnp.sum(e[:, sl], axis=-1, keepdims=True) * v_s[:, sl])
        o = o + jnp.concatenate(parts, axis=1)
    second = pos >= half
    qx = (q * jnp.exp(jnp.where(second, b - bmid, -jnp.inf))).astype(BF16)
    kx = (kk * jnp.exp(jnp.where(second, -jnp.inf, bmid - b))).astype(BF16)
    parts = []
    for h in range(B_HEADS):
        sl = slice(h * B_HEAD_DIM, (h + 1) * B_HEAD_DIM)
        a = lax.dot_general(qx[:, sl], kx[:, sl], (((1,), (1,)), ((), ())), preferred_element_type=F32)
        parts.append(_dot(jnp.where(same, a, 0.0).astype(BF16), vb[:, sl]))
    o = o + jnp.concatenate(parts, axis=1)

    qe = (q * jnp.exp(b)).astype(BF16)
    kd = (kk * jnp.exp(bend - b)).astype(BF16)
    dec = jnp.exp(bend)
    for c in range(nch):
        rs = slice(c * cs, (c + 1) * cs)
        for h in range(B_HEADS):
            sl = slice(h * B_HEAD_DIM, (h + 1) * B_HEAD_DIM)
            st = st_ref[h]
            oi_ref[rs, sl] = lax.dot_general(qe[rs, sl], st.astype(BF16), (((1,), (1,)), ((), ())),
                                             preferred_element_type=F32)
            upd = lax.dot_general(vb[rs, sl], kd[rs, sl], (((0,), (0,)), ((), ())),
                                  preferred_element_type=F32)
            st_ref[h] = st * dec[c * cs:c * cs + 1, sl] + upd
    o = o + oi_ref[...]

    parts = []
    for h in range(B_HEADS):
        sl = slice(h * B_HEAD_DIM, (h + 1) * B_HEAD_DIM)
        oh = o[:, sl]
        parts.append(oh * lax.rsqrt(jnp.mean(oh * oh, axis=-1, keepdims=True) + EPS))
    on = jnp.concatenate(parts, axis=1) * go_ref[...]
    bg = bg_ref[...]
    o_ref[...] = (on * (bg * _sigmoid(bg))).astype(o_ref.dtype)


def _hgrn(proj, lb_table, go, bsz, seq, tl, layer):
    m = proj.shape[0]
    tpb = seq // tl
    col = lambda c: (lambda b, t: (b * tpb + t, c // B_WIDTH))
    return pl.pallas_call(
        functools.partial(_hgrn_kernel, tl=tl, layer=layer),
        grid=(bsz, tpb),
        in_specs=[
            pl.BlockSpec((tl, B_WIDTH), col(COL_BQ)),
            pl.BlockSpec((tl, B_WIDTH), col(COL_BF)),
            pl.BlockSpec((tl, B_WIDTH), col(COL_BI)),
            pl.BlockSpec((tl, B_WIDTH), col(COL_BG)),
            pl.BlockSpec(lb_table.shape, lambda b, t: (0, 0)),
            pl.BlockSpec((1, B_WIDTH), lambda b, t: (0, 0)),
        ],
        out_specs=pl.BlockSpec((tl, B_WIDTH), lambda b, t: (b * tpb + t, 0)),
        out_shape=jax.ShapeDtypeStruct((m, B_WIDTH), BF16),
        scratch_shapes=[
            pltpu.VMEM((B_HEADS, B_HEAD_DIM, B_HEAD_DIM), F32),
            pltpu.VMEM((tl, B_WIDTH), F32),
        ],
        compiler_params=pltpu.CompilerParams(
            dimension_semantics=("arbitrary", "arbitrary"), vmem_limit_bytes=VMEM_LIMIT_BYTES),
        name="hgrn",
    )(proj, proj, proj, proj, lb_table, go)


def _merge_kernel(oat_ref, ob_ref, ga_ref, gb_ref, x_ref, g1_ref, wa_ref, wb_ref, wo_ref, o_ref):
    tm = x_ref.shape[0]
    oa = oat_ref[0].reshape(A_WIDTH, tm).astype(F32).T.astype(BF16)
    pa = _dot(oa, wa_ref[...])
    pb = _dot(ob_ref[...], wb_ref[...])
    merged = _sigmoid(ga_ref[...]) * pa + _sigmoid(gb_ref[...]) * pb
    y = _dot(merged.astype(BF16), wo_ref[...])
    o_ref[...] = x_ref[...] + g1_ref[0] * y


def _merge(oat, ob, proj, x2, gate1, wa, wb, wo, seq, tm):
    m, d = x2.shape
    tpb = seq // tm
    row = lambda i: (i, 0)
    const = lambda i: (0, 0)
    return pl.pallas_call(
        _merge_kernel,
        grid=(m // tm,),
        in_specs=[
            pl.BlockSpec((1, A_HEADS, A_HEAD_DIM, tm), lambda i: (i // tpb, 0, 0, i % tpb)),
            pl.BlockSpec((tm, B_WIDTH), row),
            pl.BlockSpec((tm, d), lambda i: (i, COL_GA // d)),
            pl.BlockSpec((tm, d), lambda i: (i, COL_GB // d)),
            pl.BlockSpec((tm, d), row),
            pl.BlockSpec((1, 1, d), lambda i: (i // tpb, 0, 0)),
            pl.BlockSpec(wa.shape, const),
            pl.BlockSpec(wb.shape, const),
            pl.BlockSpec(wo.shape, const),
        ],
        out_specs=pl.BlockSpec((tm, d), row),
        out_shape=jax.ShapeDtypeStruct((m, d), F32),
        compiler_params=pltpu.CompilerParams(vmem_limit_bytes=VMEM_LIMIT_BYTES),
        name="merge",
    )(oat, ob, proj, proj, x2, gate1, wa, wb, wo)


def _mlp_kernel(x_ref, g_ref, sc_ref, sh_ref, g2_ref, w1_ref, b1_ref, w2_ref, b2_ref, o_ref, *, tf):
    x = x_ref[...]
    ms = jnp.mean(x * x, axis=-1, keepdims=True)
    h = x * lax.rsqrt(ms + EPS) * g_ref[...]
    h = (h * (1.0 + sc_ref[0]) + sh_ref[0]).astype(BF16)
    dff = w1_ref.shape[1]
    y = jnp.zeros(x.shape, F32)
    for c in range(dff // tf):
        cs = slice(c * tf, (c + 1) * tf)
        a = jnp.maximum(_dot(h, w1_ref[:, cs]) + b1_ref[:, cs], 0.0)
        y = y + _dot((a * a).astype(BF16), w2_ref[cs, :])
    o_ref[...] = x + g2_ref[0] * (y + b2_ref[...])


def _mlp(x1, g, scale, shift, gate2, w1, b1, w2, b2, seq, tm, tf):
    m, d = x1.shape
    tpb = seq // tm
    row = lambda i: (i, 0)
    const = lambda i: (0, 0)
    bat = lambda i: (i // tpb, 0, 0)
    return pl.pallas_call(
        functools.partial(_mlp_kernel, tf=tf),
        grid=(m // tm,),
        in_specs=[
            pl.BlockSpec((tm, d), row),
            pl.BlockSpec((1, d), const),
            pl.BlockSpec((1, 1, d), bat),
            pl.BlockSpec((1, 1, d), bat),
            pl.BlockSpec((1, 1, d), bat),
            pl.BlockSpec(w1.shape, const),
            pl.BlockSpec(b1.shape, const),
            pl.BlockSpec(w2.shape, const),
            pl.BlockSpec(b2.shape, const),
        ],
        out_specs=pl.BlockSpec((tm, d), row),
        out_shape=jax.ShapeDtypeStruct((m, d), F32),
        compiler_params=pltpu.CompilerParams(vmem_limit_bytes=VMEM_LIMIT_BYTES),
        name="mlp",
    )(x1, g, scale, shift, gate2, w1, b1, w2, b2)


def _regroup_kernel(w_ref, o_ref, *, d):
    kv0 = Q_LORA_RANK
    kw0 = kv0 + 2 * A_WIDTH
    rest0 = kw0 + IDX_DIM + IDX_HEADS
    rest = 4 * B_WIDTH + 2 * d
    w = w_ref[...]
    o_ref[:, COL_AK:COL_AK + 2 * A_WIDTH] = w[:, kv0:kw0].astype(BF16)
    o_ref[:, COL_BQ:COL_BQ + rest] = w[:, rest0:rest0 + rest].astype(BF16)
    o_ref[:, COL_QL:COL_QL + Q_LORA_RANK] = w[:, :Q_LORA_RANK].astype(BF16)
    kw = w[:, kw0:kw0 + LANES]
    lane = lax.broadcasted_iota(I32, kw.shape, 1)
    o_ref[:, COL_KW:COL_KW + LANES] = jnp.where(lane < IDX_DIM + IDX_HEADS, kw, 0.0).astype(BF16)


def _regroup_w_in(w):
    d, n = w.shape
    tr = 128
    assert COL_BQ + 4 * B_WIDTH + 2 * d == COL_QL and COL_QL + Q_LORA_RANK == COL_KW
    return pl.pallas_call(
        functools.partial(_regroup_kernel, d=d),
        grid=(d // tr,),
        in_specs=[pl.BlockSpec((tr, n), lambda i: (i, 0))],
        out_specs=pl.BlockSpec((tr, PROJ_COLS), lambda i: (i, 0)),
        out_shape=jax.ShapeDtypeStruct((d, PROJ_COLS), BF16),
        compiler_params=pltpu.CompilerParams(vmem_limit_bytes=VMEM_LIMIT_BYTES),
        name="regroup",
    )(w)


def _layer(x, mod, l, p, tiles):
    bsz, seq, d = x.shape
    m = bsz * seq
    shift1, scale1, gate1, shift2, scale2, gate2 = (mod[:, k * d:(k + 1) * d].reshape(bsz, 1, d) for k in range(6))
    x2 = x.reshape(m, d)

    proj = _inproj(x2, p['norm1_g'][l][None], scale1, shift1, _regroup_w_in(p['w_in'][l]), seq, tiles['tm_in'])

    tq, ck = tiles['tq'], tiles['ck']
    wuq = jnp.concatenate([p['w_uq'][l], p['w_uq_idx'][l]], axis=1).astype(BF16)
    gki = jnp.concatenate([p['k_idx_norm_g'][l], jnp.zeros((LANES - IDX_DIM,), F32)])[None]
    qt, ak, qit, ki, wt, vt = _qkprep(
        proj, p['q_lat_norm_g'][l][None], wuq,
        jnp.tile(p['q_norm_g'][l], A_HEADS)[None], jnp.tile(p['k_norm_g'][l], A_HEADS)[None], gki, bsz, seq, ck)

    topk = min(TOPK_MAX, seq // 4)
    tril = jnp.tril(jnp.ones((ck, ck), BF16))
    out_at = _dsa(qit, wt, ki.reshape(bsz, seq, IDX_DIM), qt, ak.reshape(bsz, seq, A_WIDTH), vt, tril, tq, ck, topk)

    out_b = _hgrn(proj, p['hgrn_lb'], p['hgrn_o_norm_g'][l][None], bsz, seq, tiles['tl'], l)

    x1 = _merge(out_at, out_b, proj, x2, gate1, p['w_proj_a'][l].astype(BF16), p['w_proj_b'][l].astype(BF16),
                p['w_out'][l].astype(BF16), seq, tiles['tm'])
    out = _mlp(x1, p['norm2_g'][l][None], scale2, shift2, gate2, p['w_mlp1'][l].astype(BF16), p['b_mlp1'][l][None],
               p['w_mlp2'][l].astype(BF16), p['b_mlp2'][l][None], seq, tiles['tm'], tiles['tf'])
    return out.reshape(bsz, seq, d)


def _tiles(seq):
    pick = lambda want: min(want, seq)
    return dict(tm_in=pick(256), tm=pick(512), tq=pick(256), ck=pick(512), tl=pick(256), tf=1024)


def kernel(x, c, w_ada, b_ada, norm1_g, w_in, q_lat_norm_g, w_uq, w_uq_idx, q_norm_g, k_norm_g, k_idx_norm_g,
           hgrn_lb, hgrn_o_norm_g, w_proj_a, w_proj_b, w_out, norm2_g, w_mlp1, b_mlp1, w_mlp2, b_mlp2):
    p = dict(norm1_g=norm1_g, w_in=w_in, q_lat_norm_g=q_lat_norm_g, w_uq=w_uq, w_uq_idx=w_uq_idx,
             q_norm_g=q_norm_g, k_norm_g=k_norm_g, k_idx_norm_g=k_idx_norm_g, hgrn_lb=hgrn_lb,
             hgrn_o_norm_g=hgrn_o_norm_g, w_proj_a=w_proj_a, w_proj_b=w_proj_b, w_out=w_out, norm2_g=norm2_g,
             w_mlp1=w_mlp1, b_mlp1=b_mlp1, w_mlp2=w_mlp2, b_mlp2=b_mlp2)
    bsz, seq, d = x.shape
    depth = w_ada.shape[0]
    tiles = _tiles(seq)
    c_pad = jnp.zeros((8, d), F32).at[:bsz].set(c)
    for l in range(depth):
        mod = _adaln(c_pad, w_ada[l], b_ada[l][None])[:bsz]
        x = _layer(x, mod, l, p, tiles)
    return x
```

```python
import functools

import jax
import jax.numpy as jnp
from jax import lax
from jax.experimental import pallas as pl
from jax.experimental.pallas import tpu as pltpu

F32 = jnp.float32
BF16 = jnp.bfloat16
I32 = jnp.int32

EPS = 1e-6
A_HEADS = 8
A_HEAD_DIM = 64
A_WIDTH = A_HEADS * A_HEAD_DIM
Q_LORA_RANK = 256
IDX_HEADS = 4
IDX_DIM = 64
TOPK_MAX = 256
B_WIDTH = 512
B_HEADS = 4
B_HEAD_DIM = 128
HGRN_CHUNK = 16

LANES = 128
VMEM_LIMIT_BYTES = 56 * 1024 * 1024

COL_AK, COL_AV, COL_BQ, COL_BF, COL_BI, COL_BG = 0, 512, 1024, 1536, 2048, 2560
COL_GA, COL_GB, COL_QL, COL_KW = 3072, 4096, 5120, 5376
PROJ_COLS = 5504

LOG2E = 1.4426950408889634
BISECT_STEPS = 12
SELECT_ACC_ROWS = 32
LOGIT_BOUND_LIMIT = 60.0


def _sigmoid(x):
    return 0.5 * jnp.tanh(0.5 * x) + 0.5


def _split2(x):
    hi = x.astype(BF16)
    lo = (x - hi.astype(F32)).astype(BF16)
    return hi, lo


def _split3(x):
    a = x.astype(BF16)
    r = x - a.astype(F32)
    b = r.astype(BF16)
    c = (r - b.astype(F32)).astype(BF16)
    return a, b, c


def _dot(a, b):
    return jnp.dot(a, b, preferred_element_type=F32)


def _log2(n):
    assert n > 0 and n & (n - 1) == 0, n
    return n.bit_length() - 1


def _group_ones(n, group):
    r = lax.broadcasted_iota(I32, (n, n), 0) >> _log2(group)
    c = lax.broadcasted_iota(I32, (n, n), 1) >> _log2(group)
    return jnp.where(r == c, 1.0, 0.0).astype(BF16)


def _group_mean_sq(x, ones_bd, group):
    hi, lo = _split2(x * x)
    return (_dot(hi, ones_bd) + _dot(lo, ones_bd)) * (1.0 / group)


def _fold_rows(x, op, init):
    r = init.shape[0]
    for s in range(x.shape[0] // r):
        init = op(init, x[s * r:(s + 1) * r, :])
    return init


def _adaln_kernel(c_ref, w_ref, b_ref, o_ref):
    c = c_ref[...]
    a = c * _sigmoid(c)
    a1, a2, a3 = _split3(a)
    w1, w2, w3 = _split3(w_ref[...])
    acc = _dot(a1, w1) + (_dot(a1, w2) + _dot(a2, w1)) + (_dot(a2, w2) + _dot(a1, w3) + _dot(a3, w1))
    o_ref[...] = acc + b_ref[...]


def _adaln(c_pad, w, b):
    rows, d = c_pad.shape
    n = w.shape[1]
    tn = 1536
    return pl.pallas_call(
        _adaln_kernel,
        grid=(n // tn,),
        in_specs=[
            pl.BlockSpec((rows, d), lambda j: (0, 0)),
            pl.BlockSpec((d, tn), lambda j: (0, j)),
            pl.BlockSpec((1, tn), lambda j: (0, j)),
        ],
        out_specs=pl.BlockSpec((rows, tn), lambda j: (0, j)),
        out_shape=jax.ShapeDtypeStruct((rows, n), F32),
        compiler_params=pltpu.CompilerParams(vmem_limit_bytes=VMEM_LIMIT_BYTES),
        name="adaln",
    )(c_pad, w, b)


def _inproj_kernel(x_ref, g_ref, sc_ref, sh_ref, w_ref, o_ref):
    x = x_ref[...]
    ms = jnp.mean(x * x, axis=-1, keepdims=True)
    h = x * lax.rsqrt(ms + EPS) * g_ref[...]
    h = h * (1.0 + sc_ref[0]) + sh_ref[0]
    o_ref[...] = _dot(h.astype(BF16), w_ref[...])


def _inproj(x2, g, scale, shift, w, seq, tm):
    m, d = x2.shape
    n = w.shape[1]
    tpb = seq // tm
    return pl.pallas_call(
        _inproj_kernel,
        grid=(m // tm,),
        in_specs=[
            pl.BlockSpec((tm, d), lambda i: (i, 0)),
            pl.BlockSpec((1, d), lambda i: (0, 0)),
            pl.BlockSpec((1, 1, d), lambda i: (i // tpb, 0, 0)),
            pl.BlockSpec((1, 1, d), lambda i: (i // tpb, 0, 0)),
            pl.BlockSpec((d, n), lambda i: (0, 0)),
        ],
        out_specs=pl.BlockSpec((tm, n), lambda i: (i, 0)),
        out_shape=jax.ShapeDtypeStruct((m, n), F32),
        compiler_params=pltpu.CompilerParams(vmem_limit_bytes=VMEM_LIMIT_BYTES),
        name="inproj",
    )(x2, g, scale, shift, w)


def _qkprep_kernel(ak_ref, av_ref, ql_ref, kw_ref, gql_ref, wuq_ref, gq_ref, gk_ref, gki_ref,
                   qt_out, ak_out, qit_out, ki_out, wt_out, vt_out):
    tm = ak_ref.shape[0]
    ones64 = _group_ones(A_WIDTH, A_HEAD_DIM)
    ql = ql_ref[...]
    ql = ql * lax.rsqrt(jnp.mean(ql * ql, axis=-1, keepdims=True) + EPS) * gql_ref[...]
    up = _dot(ql.astype(BF16), wuq_ref[...])
    aq = up[:, :A_WIDTH]
    aq = aq * lax.rsqrt(_group_mean_sq(aq, ones64, A_HEAD_DIM) + EPS) * gq_ref[...]
    aq_t = (aq * (A_HEAD_DIM ** -0.5 * LOG2E)).T
    zero_half = jnp.zeros((A_HEAD_DIM, tm), BF16)
    for h in range(A_HEADS):
        e = h % 2
        qt_out[0, h, e * A_HEAD_DIM:(e + 1) * A_HEAD_DIM, :] = aq_t[h * A_HEAD_DIM:(h + 1) * A_HEAD_DIM].astype(BF16)
        qt_out[0, h, (1 - e) * A_HEAD_DIM:(2 - e) * A_HEAD_DIM, :] = zero_half
    qit_out[0] = up[:, A_WIDTH:].T.reshape(IDX_HEADS, IDX_DIM, tm).astype(BF16)
    ak = ak_ref[...]
    ak = ak * lax.rsqrt(_group_mean_sq(ak, ones64, A_HEAD_DIM) + EPS) * gk_ref[...]
    ak_out[...] = ak.astype(BF16)
    vt_out[0, 0] = av_ref[...].T.reshape(A_HEADS, A_HEAD_DIM, tm).astype(BF16)
    kw = kw_ref[...]
    lane = lax.broadcasted_iota(I32, kw.shape, 1)
    ksq = jnp.where(lane < IDX_DIM, kw * kw, 0.0)
    kms = jnp.sum(ksq, axis=-1, keepdims=True) * (1.0 / IDX_DIM)
    kn = kw * lax.rsqrt(kms + EPS) * gki_ref[...]
    ki_out[...] = kn[:, :IDX_DIM].astype(BF16)
    idx_scale = (IDX_DIM ** -0.5) * (IDX_HEADS ** -0.5)
    w_rows = jnp.where((lane >= IDX_DIM) & (lane < IDX_DIM + IDX_HEADS), kw * idx_scale, 0.0).T
    wt_out[0] = w_rows[IDX_DIM:IDX_DIM + 8]


def _qkprep(proj, gql, wuq, gq, gk, gki, bsz, seq, tm):
    m = proj.shape[0]
    tpb = seq // tm
    row = lambda i: (i, 0)
    const = lambda i: (0, 0)
    return pl.pallas_call(
        _qkprep_kernel,
        grid=(m // tm,),
        in_specs=[
            pl.BlockSpec((tm, A_WIDTH), lambda i: (i, COL_AK // A_WIDTH)),
            pl.BlockSpec((tm, A_WIDTH), lambda i: (i, COL_AV // A_WIDTH)),
            pl.BlockSpec((tm, Q_LORA_RANK), lambda i: (i, COL_QL // Q_LORA_RANK)),
            pl.BlockSpec((tm, LANES), lambda i: (i, COL_KW // LANES)),
            pl.BlockSpec((1, Q_LORA_RANK), const),
            pl.BlockSpec(wuq.shape, const),
            pl.BlockSpec((1, A_WIDTH), const),
            pl.BlockSpec((1, A_WIDTH), const),
            pl.BlockSpec((1, LANES), const),
        ],
        out_specs=[
            pl.BlockSpec((1, A_HEADS, LANES, tm), lambda i: (i // tpb, 0, 0, i % tpb)),
            pl.BlockSpec((tm, A_WIDTH), row),
            pl.BlockSpec((1, IDX_HEADS, IDX_DIM, tm), lambda i: (i // tpb, 0, 0, i % tpb)),
            pl.BlockSpec((tm, IDX_DIM), row),
            pl.BlockSpec((1, 8, tm), lambda i: (i // tpb, 0, i % tpb)),
            pl.BlockSpec((1, 1, A_HEADS, A_HEAD_DIM, tm), lambda i: (i // tpb, i % tpb, 0, 0, 0)),
        ],
        out_shape=[
            jax.ShapeDtypeStruct((bsz, A_HEADS, LANES, seq), BF16),
            jax.ShapeDtypeStruct((m, A_WIDTH), BF16),
            jax.ShapeDtypeStruct((bsz, IDX_HEADS, IDX_DIM, seq), BF16),
            jax.ShapeDtypeStruct((m, IDX_DIM), BF16),
            jax.ShapeDtypeStruct((bsz, 8, seq), F32),
            jax.ShapeDtypeStruct((bsz, tpb, A_HEADS, A_HEAD_DIM, tm), BF16),
        ],
        compiler_params=pltpu.CompilerParams(vmem_limit_bytes=VMEM_LIMIT_BYTES),
        name="qkprep",
    )(proj, proj, proj, proj, gql, wuq, gq, gk, gki)


def _dsa_kernel(qit_ref, wt_ref, ki_ref, qt_ref, k_ref, vt_ref, tril_ref, o_ref,
                sc_ref, acc_ref, p_ref, kn_ref, *, tq, ck, topk, nbis):
    i = pl.program_id(1)
    n_chunks = ((i + 1) * tq + ck - 1) // ck
    kf = float(topk)
    neg_inf = float("-inf")
    pos_inf = float("inf")
    ar = SELECT_ACC_ROWS
    full_acc = lambda val: jnp.full((ar, tq), val, F32)
    row_min = lambda x: jnp.min(x, axis=0, keepdims=True)
    row_max = lambda x: jnp.max(x, axis=0, keepdims=True)
    row_sum = lambda x: jnp.sum(x, axis=0, keepdims=True)
    chunk_rows = lambda jc: pl.ds(pl.multiple_of(jc * ck, ck), ck)

    @pl.when(i == 0)
    def _():
        head_of_lane = lax.broadcasted_iota(I32, (A_WIDTH, LANES), 0) >> _log2(A_HEAD_DIM)
        pick = jnp.where(head_of_lane == lax.broadcasted_iota(I32, (A_WIDTH, LANES), 1), 1.0, 0.0).astype(BF16)

        def kn_body(jc, mx):
            k = k_ref[0, chunk_rows(jc), :].astype(F32)
            return _fold_rows(_dot((k * k).astype(BF16), pick), jnp.maximum, mx)

        mx = lax.fori_loop(0, k_ref.shape[1] // ck, kn_body, jnp.zeros((8, LANES), F32))
        kn_ref[...] = jnp.broadcast_to(row_max(mx), (8, LANES))

    t_pos = i * tq + lax.broadcasted_iota(I32, (ck, tq), 1)
    s_iota = lax.broadcasted_iota(I32, (ck, tq), 0)

    def score_body(masked, jc, carry):
        mn, mx = carry
        kc = ki_ref[0, chunk_rows(jc), :]
        score = jnp.zeros((ck, tq), F32)
        for h in range(IDX_HEADS):
            score = score + wt_ref[0, h:h + 1, :] * jnp.maximum(_dot(kc, qit_ref[0, h]), 0.0)
        if masked:
            causal = (s_iota + jc * ck) <= t_pos
            sc_ref[jc] = jnp.where(causal, score, neg_inf)
            mn = _fold_rows(jnp.where(causal, score, pos_inf), jnp.minimum, mn)
            mx = _fold_rows(jnp.where(causal, score, neg_inf), jnp.maximum, mx)
        else:
            sc_ref[jc] = score
            mn = _fold_rows(score, jnp.minimum, mn)
            mx = _fold_rows(score, jnp.maximum, mx)
        return mn, mx

    n_full = (i * tq + 1) // ck
    carry = lax.fori_loop(0, n_full, functools.partial(score_body, False), (full_acc(pos_inf), full_acc(neg_inf)))
    mn_acc, mx_acc = lax.fori_loop(n_full, n_chunks, functools.partial(score_body, True), carry)
    rmin, rmax = row_min(mn_acc), row_max(mx_acc)

    def count_ge(cand):
        cand_b = jnp.broadcast_to(cand, (ar, tq))

        def body(jc, acc):
            return _fold_rows(sc_ref[jc], lambda a, x: a + jnp.where(x >= cand_b, 1.0, 0.0), acc)

        return row_sum(lax.fori_loop(0, n_chunks, body, full_acc(0.0)))

    def snap(lo, hi):
        lo_b = jnp.broadcast_to(lo, (ar, tq))
        hi_b = jnp.broadcast_to(hi, (ar, tq))

        def body(jc, carry):
            x = sc_ref[jc]
            a = _fold_rows(x, lambda a, x: jnp.minimum(a, jnp.where(x >= lo_b, x, pos_inf)), carry[0])
            b = _fold_rows(x, lambda b, x: jnp.maximum(b, jnp.where(x < hi_b, x, neg_inf)), carry[1])
            return a, b

        a, b = lax.fori_loop(0, n_chunks, body, (full_acc(pos_inf), full_acc(neg_inf)))
        return row_min(a), row_max(b)

    def probe(cand):
        cand_b = jnp.broadcast_to(cand, (ar, tq))

        def body(jc, carry):
            cnt, a, b = carry
            blk = sc_ref[jc]
            for s in range(ck // ar):
                x = blk[s * ar:(s + 1) * ar, :]
                ge = x >= cand_b
                cnt = cnt + jnp.where(ge, 1.0, 0.0)
                a = jnp.minimum(a, jnp.where(ge, x, pos_inf))
                b = jnp.maximum(b, jnp.where(ge, neg_inf, x))
            return cnt, a, b

        cnt, a, b = lax.fori_loop(0, n_chunks, body, (full_acc(0.0), full_acc(pos_inf), full_acc(neg_inf)))
        return row_sum(cnt), row_min(a), row_max(b)

    n_valid = (i * tq + 1 + lax.broadcasted_iota(I32, (1, tq), 1)).astype(F32)
    small = n_valid <= kf

    def interpolate(lo, hi, c_lo, c_hi, w_lo, w_hi):
        f_lo = (c_lo - kf + 0.5) * w_lo
        f_hi = (kf - c_hi - 0.5) * w_hi
        return lo + (hi - lo) * jnp.clip(f_lo / jnp.maximum(f_lo + f_hi, 0.5), 0.1, 0.9)

    def illinois(up, dn, w_lo, w_hi, last):
        w_hi = jnp.where(up, jnp.where(last > 0.0, w_hi * 0.5, 1.0), jnp.where(dn, 1.0, w_hi))
        w_lo = jnp.where(dn, jnp.where(last < 0.0, w_lo * 0.5, 1.0), jnp.where(up, 1.0, w_lo))
        return w_lo, w_hi, jnp.where(up, 1.0, jnp.where(dn, -1.0, last))

    def zero_stats():
        def body(jc, carry):
            c_ge0, c_pos, min_pos = carry
            blk = sc_ref[jc]
            for s in range(ck // ar):
                x = blk[s * ar:(s + 1) * ar, :]
                pos = x > 0.0
                c_ge0 = c_ge0 + jnp.where(x >= 0.0, 1.0, 0.0)
                c_pos = c_pos + jnp.where(pos, 1.0, 0.0)
                min_pos = jnp.minimum(min_pos, jnp.where(pos, x, pos_inf))
            return c_ge0, c_pos, min_pos

        c_ge0, c_pos, min_pos = lax.fori_loop(0, n_chunks, body, (full_acc(0.0), full_acc(0.0), full_acc(pos_inf)))
        return row_sum(c_ge0), row_sum(c_pos), row_min(min_pos)

    c_ge0, c_pos, min_pos = zero_stats()
    below_zero = c_ge0 < kf
    above_zero = c_pos >= kf
    at_zero = jnp.logical_not(below_zero | above_zero)
    hi0 = rmax + jnp.maximum(jnp.abs(rmax) * 1e-6, 1e-30)
    lo = jnp.where(above_zero, min_pos, jnp.where(at_zero, 0.0, rmin))
    c_lo = jnp.where(above_zero, c_pos, jnp.where(at_zero, c_ge0, n_valid))
    hi = jnp.where(above_zero, hi0, 0.0)
    c_hi = jnp.where(above_zero, 0.0, jnp.where(at_zero, c_pos, c_ge0))

    def search_body(_, st):
        lo, hi, c_lo, c_hi, w_lo, w_hi, last = st
        mid = interpolate(lo, hi, c_lo, c_hi, w_lo, w_hi)
        ok = (mid > lo) & (mid < hi)
        c = count_ge(mid)
        up = ok & (c >= kf)
        dn = ok & (c < kf)
        w_lo, w_hi, last = illinois(up, dn, w_lo, w_hi, last)
        return (jnp.where(up, mid, lo), jnp.where(dn, mid, hi), jnp.where(up, c, c_lo), jnp.where(dn, c, c_hi),
                w_lo, w_hi, last)

    zeros_q, ones_q = jnp.zeros((1, tq), F32), jnp.ones((1, tq), F32)
    lo, hi, c_lo, c_hi, _, _, _ = lax.fori_loop(0, nbis, search_body,
                                                (lo, hi, c_lo, c_hi, ones_q, ones_q, zeros_q))

    lo, hi = snap(lo, hi)
    lo, hi = jnp.where(at_zero, 0.0, lo), jnp.where(at_zero, 0.0, hi)

    def active_of(lo, hi, c_lo):
        return jnp.logical_not(small) & (lo < hi) & (c_lo != kf)

    def snap_body(st):
        lo, hi, c_lo, c_hi, w_lo, w_hi, last, _ = st
        act = active_of(lo, hi, c_lo)
        mid = interpolate(lo, hi, c_lo, c_hi, w_lo, w_hi)
        mid = jnp.where((c_hi == kf - 1.0) | (mid <= lo) | (mid > hi), hi, mid)
        c, a, b = probe(mid)
        up = act & (c >= kf)
        dn = act & (c < kf)
        w_lo, w_hi, last = illinois(up, dn, w_lo, w_hi, last)
        lo, c_lo = jnp.where(up, a, lo), jnp.where(up, c, c_lo)
        hi, c_hi = jnp.where(dn, b, hi), jnp.where(dn, c, c_hi)
        return lo, hi, c_lo, c_hi, w_lo, w_hi, last, jnp.max(jnp.where(active_of(lo, hi, c_lo), 1.0, 0.0))

    flag0 = jnp.max(jnp.where(active_of(lo, hi, c_lo), 1.0, 0.0))
    lo, hi, c_lo, c_hi = lax.while_loop(lambda st: st[7] > 0.0, snap_body,
                                        (lo, hi, c_lo, c_hi, ones_q, ones_q, zeros_q, flag0))[:4]

    tau = jnp.where(small, rmin, lo)
    excess = jnp.logical_not(small) & (lo == hi) & (c_lo > kf)
    need = jnp.where(excess, kf - c_hi, 4.0 * 65536.0 * 65536.0)

    @pl.when(jnp.max(jnp.where(excess, 1.0, 0.0)) > 0.0)
    def _():
        def tie_body(jc, carry):
            blk = sc_ref[jc]
            eq = blk == tau
            pc = _dot(tril_ref[...], jnp.where(eq, 1.0, 0.0).astype(BF16)) + carry
            sc_ref[jc] = jnp.where(eq & (pc > need), neg_inf, blk)
            return pc[ck - 1:ck, :]

        lax.fori_loop(0, n_chunks, tie_body, jnp.zeros((1, tq), F32))

    def logits_t(jc, h):
        kc = k_ref[0, chunk_rows(jc), (h // 2) * LANES:(h // 2 + 1) * LANES]
        return _dot(kc, qt_ref[0, h])

    m_bound = []
    bmax = jnp.zeros((1, 1), F32)
    for h in range(A_HEADS):
        qf = qt_ref[0, h].astype(F32)
        bound = jnp.sqrt(row_sum(qf * qf) * kn_ref[0:1, h:h + 1])
        m_bound.append(bound)
        bmax = jnp.maximum(bmax, jnp.max(bound, axis=1, keepdims=True))

    def exact_max(_):
        def max_body(jc, mx):
            sel = sc_ref[jc] >= tau
            return tuple(_fold_rows(jnp.where(sel, logits_t(jc, h), neg_inf), jnp.maximum, mx[h])
                         for h in range(A_HEADS))

        mx = lax.fori_loop(0, n_chunks, max_body, tuple(jnp.full((8, tq), neg_inf, F32) for _ in range(A_HEADS)))
        return tuple(row_max(m) for m in mx)

    m_ref_vals = lax.cond(bmax[0, 0] > LOGIT_BOUND_LIMIT, exact_max, lambda _: tuple(m_bound), 0)

    ones_rows = jnp.ones((16, ck), BF16)
    acc_ref[...] = jnp.zeros(acc_ref.shape, F32)

    def pv_stage(jc, h):
        lhs = jnp.concatenate([vt_ref[0, jc, h], ones_rows], axis=0)
        acc_ref[h] += _dot(lhs, p_ref[h])

    def qk_stage(jc, sel, h):
        p_ref[h] = jnp.where(sel, jnp.exp2(logits_t(jc, h) - m_ref_vals[h]), 0.0).astype(BF16)

    sel0 = sc_ref[0] >= tau
    for h in range(A_HEADS):
        qk_stage(0, sel0, h)

    def att_body(jc, carry):
        sel = sc_ref[jc] >= tau
        for h in range(A_HEADS):
            pv_stage(jc - 1, h)
            qk_stage(jc, sel, h)
        return carry

    lax.fori_loop(1, n_chunks, att_body, 0)
    for h in range(A_HEADS):
        pv_stage(n_chunks - 1, h)
    for h in range(A_HEADS):
        a = acc_ref[h]
        o_ref[0, h] = (a[:A_HEAD_DIM] * (1.0 / a[A_HEAD_DIM:A_HEAD_DIM + 1])).astype(o_ref.dtype)


def _dsa(qit, wt, ki, qt, k, vt, tril, tq, ck, topk):
    bsz, hi, di, seq = qit.shape
    nc = seq // ck
    kern = functools.partial(_dsa_kernel, tq=tq, ck=ck, topk=topk, nbis=BISECT_STEPS)
    return pl.pallas_call(
        kern,
        grid=(bsz, seq // tq),
        in_specs=[
            pl.BlockSpec((1, hi, di, tq), lambda b, i: (b, 0, 0, i)),
            pl.BlockSpec((1, 8, tq), lambda b, i: (b, 0, i)),
            pl.BlockSpec((1, seq, di), lambda b, i: (b, 0, 0)),
            pl.BlockSpec((1, A_HEADS, LANES, tq), lambda b, i: (b, 0, 0, i)),
            pl.BlockSpec((1, seq, A_WIDTH), lambda b, i: (b, 0, 0)),
            pl.BlockSpec((1, nc, A_HEADS, A_HEAD_DIM, ck), lambda b, i: (b, 0, 0, 0, 0)),
            pl.BlockSpec((ck, ck), lambda b, i: (0, 0)),
        ],
        out_specs=pl.BlockSpec((1, A_HEADS, A_HEAD_DIM, tq), lambda b, i: (b, 0, 0, i)),
        out_shape=jax.ShapeDtypeStruct((bsz, A_HEADS, A_HEAD_DIM, seq), BF16),
        scratch_shapes=[
            pltpu.VMEM((nc, ck, tq), F32),
            pltpu.VMEM((A_HEADS, A_HEAD_DIM + 16, tq), F32),
            pltpu.VMEM((A_HEADS, ck, tq), BF16),
            pltpu.VMEM((8, LANES), F32),
        ],
        compiler_params=pltpu.CompilerParams(
            dimension_semantics=("arbitrary", "arbitrary"), vmem_limit_bytes=VMEM_LIMIT_BYTES),
        name="dsa",
    )(qit, wt, ki, qt, k, vt, tril)


def _hgrn_kernel(bq_ref, bf_ref, bi_ref, bg_ref, lb_ref, go_ref, o_ref, st_ref, oi_ref, *, tl, layer):
    cs = HGRN_CHUNK
    nch = tl // cs

    @pl.when(pl.program_id(1) == 0)
    def _():
        st_ref[...] = jnp.zeros(st_ref.shape, F32)

    lbr = lb_ref[...]
    slots = [lbr[k:k + 1] for k in range(lbr.shape[0])]
    mx = functools.reduce(jnp.maximum, slots)
    es = [jnp.exp(s - mx) for s in slots]
    lb = functools.reduce(jnp.add, es[:layer + 1]) / functools.reduce(jnp.add, es)

    bq = bq_ref[...]
    q = bq * _sigmoid(bq)
    f = lb + (1.0 - lb) * _sigmoid(bf_ref[...])
    kk = 1.0 - f
    g = jnp.log(f)
    v = bi_ref[...]

    r_i = lax.broadcasted_iota(I32, (tl, tl), 0)
    c_i = lax.broadcasted_iota(I32, (tl, tl), 1)
    same = (r_i >> _log2(cs)) == (c_i >> _log2(cs))
    tri = jnp.where(same & (c_i <= r_i), 1.0, 0.0).astype(BF16)
    blk = jnp.where(same, 1.0, 0.0).astype(BF16)
    g1, g2, g3 = _split3(g)
    b = _dot(tri, g1) + _dot(tri, g2) + _dot(tri, g3)
    bend = _dot(blk, g1) + _dot(blk, g2) + _dot(blk, g3)

    bk = b - jnp.log(kk)
    pos = lax.broadcasted_iota(I32, (tl, B_WIDTH), 0) & (cs - 1)
    o = jnp.zeros((tl, B_WIDTH), F32)
    for r in range(cs):
        if r == 0:
            bk_s, v_s = bk, v
        else:
            bk_s = pltpu.roll(bk, r, axis=0)
            v_s = pltpu.roll(v, r, axis=0)
        e = q * jnp.exp(jnp.where(pos >= r, b - bk_s, -jnp.inf))
        parts = []
        for h in range(B_HEADS):
            sl = slice(h * B_HEAD_DIM, (h + 1) * B_HEAD_DIM)
            parts.append(jnp.sum(e[:, sl], axis=-1, keepdims=True) * v_s[:, sl])
        o = o + jnp.concatenate(parts, axis=1)

    qe = (q * jnp.exp(b)).astype(BF16)
    kd = (kk * jnp.exp(bend - b)).astype(BF16)
    vb = v.astype(BF16)
    dec = jnp.exp(bend)
    for c in range(nch):
        rs = slice(c * cs, (c + 1) * cs)
        for h in range(B_HEADS):
            sl = slice(h * B_HEAD_DIM, (h + 1) * B_HEAD_DIM)
            st = st_ref[h]
            oi_ref[rs, sl] = lax.dot_general(qe[rs, sl], st.astype(BF16), (((1,), (1,)), ((), ())),
                                             preferred_element_type=F32)
            upd = lax.dot_general(vb[rs, sl], kd[rs, sl], (((0,), (0,)), ((), ())),
                                  preferred_element_type=F32)
            st_ref[h] = st * dec[c * cs:c * cs + 1, sl] + upd
    o = o + oi_ref[...]

    parts = []
    for h in range(B_HEADS):
        sl = slice(h * B_HEAD_DIM, (h + 1) * B_HEAD_DIM)
        oh = o[:, sl]
        parts.append(oh * lax.rsqrt(jnp.mean(oh * oh, axis=-1, keepdims=True) + EPS))
    on = jnp.concatenate(parts, axis=1) * go_ref[...]
    bg = bg_ref[...]
    o_ref[...] = (on * (bg * _sigmoid(bg))).astype(o_ref.dtype)


def _hgrn(proj, lb_table, go, bsz, seq, tl, layer):
    m = proj.shape[0]
    tpb = seq // tl
    col = lambda c: (lambda b, t: (b * tpb + t, c // B_WIDTH))
    return pl.pallas_call(
        functools.partial(_hgrn_kernel, tl=tl, layer=layer),
        grid=(bsz, tpb),
        in_specs=[
            pl.BlockSpec((tl, B_WIDTH), col(COL_BQ)),
            pl.BlockSpec((tl, B_WIDTH), col(COL_BF)),
            pl.BlockSpec((tl, B_WIDTH), col(COL_BI)),
            pl.BlockSpec((tl, B_WIDTH), col(COL_BG)),
            pl.BlockSpec(lb_table.shape, lambda b, t: (0, 0)),
            pl.BlockSpec((1, B_WIDTH), lambda b, t: (0, 0)),
        ],
        out_specs=pl.BlockSpec((tl, B_WIDTH), lambda b, t: (b * tpb + t, 0)),
        out_shape=jax.ShapeDtypeStruct((m, B_WIDTH), BF16),
        scratch_shapes=[
            pltpu.VMEM((B_HEADS, B_HEAD_DIM, B_HEAD_DIM), F32),
            pltpu.VMEM((tl, B_WIDTH), F32),
        ],
        compiler_params=pltpu.CompilerParams(
            dimension_semantics=("arbitrary", "arbitrary"), vmem_limit_bytes=VMEM_LIMIT_BYTES),
        name="hgrn",
    )(proj, proj, proj, proj, lb_table, go)


def _merge_kernel(oat_ref, ob_ref, ga_ref, gb_ref, x_ref, g1_ref, wa_ref, wb_ref, wo_ref, o_ref):
    tm = x_ref.shape[0]
    oa = oat_ref[0].reshape(A_WIDTH, tm).astype(F32).T.astype(BF16)
    pa = _dot(oa, wa_ref[...])
    pb = _dot(ob_ref[...], wb_ref[...])
    merged = _sigmoid(ga_ref[...]) * pa + _sigmoid(gb_ref[...]) * pb
    y = _dot(merged.astype(BF16), wo_ref[...])
    o_ref[...] = x_ref[...] + g1_ref[0] * y


def _merge(oat, ob, proj, x2, gate1, wa, wb, wo, seq, tm):
    m, d = x2.shape
    tpb = seq // tm
    row = lambda i: (i, 0)
    const = lambda i: (0, 0)
    return pl.pallas_call(
        _merge_kernel,
        grid=(m // tm,),
        in_specs=[
            pl.BlockSpec((1, A_HEADS, A_HEAD_DIM, tm), lambda i: (i // tpb, 0, 0, i % tpb)),
            pl.BlockSpec((tm, B_WIDTH), row),
            pl.BlockSpec((tm, d), lambda i: (i, COL_GA // d)),
            pl.BlockSpec((tm, d), lambda i: (i, COL_GB // d)),
            pl.BlockSpec((tm, d), row),
            pl.BlockSpec((1, 1, d), lambda i: (i // tpb, 0, 0)),
            pl.BlockSpec(wa.shape, const),
            pl.BlockSpec(wb.shape, const),
            pl.BlockSpec(wo.shape, const),
        ],
        out_specs=pl.BlockSpec((tm, d), row),
        out_shape=jax.ShapeDtypeStruct((m, d), F32),
        compiler_params=pltpu.CompilerParams(vmem_limit_bytes=VMEM_LIMIT_BYTES),
        name="merge",
    )(oat, ob, proj, proj, x2, gate1, wa, wb, wo)


def _mlp_kernel(x_ref, g_ref, sc_ref, sh_ref, g2_ref, w1_ref, b1_ref, w2_ref, b2_ref, o_ref, *, tf):
    x = x_ref[...]
    ms = jnp.mean(x * x, axis=-1, keepdims=True)
    h = x * lax.rsqrt(ms + EPS) * g_ref[...]
    h = (h * (1.0 + sc_ref[0]) + sh_ref[0]).astype(BF16)
    dff = w1_ref.shape[1]
    y = jnp.zeros(x.shape, F32)
    for c in range(dff // tf):
        cs = slice(c * tf, (c + 1) * tf)
        a = jnp.maximum(_dot(h, w1_ref[:, cs]) + b1_ref[:, cs], 0.0)
        y = y + _dot((a * a).astype(BF16), w2_ref[cs, :])
    o_ref[...] = x + g2_ref[0] * (y + b2_ref[...])


def _mlp(x1, g, scale, shift, gate2, w1, b1, w2, b2, seq, tm, tf):
    m, d = x1.shape
    tpb = seq // tm
    row = lambda i: (i, 0)
    const = lambda i: (0, 0)
    bat = lambda i: (i // tpb, 0, 0)
    return pl.pallas_call(
        functools.partial(_mlp_kernel, tf=tf),
        grid=(m // tm,),
        in_specs=[
            pl.BlockSpec((tm, d), row),
            pl.BlockSpec((1, d), const),
            pl.BlockSpec((1, 1, d), bat),
            pl.BlockSpec((1, 1, d), bat),
            pl.BlockSpec((1, 1, d), bat),
            pl.BlockSpec(w1.shape, const),
            pl.BlockSpec(b1.shape, const),
            pl.BlockSpec(w2.shape, const),
            pl.BlockSpec(b2.shape, const),
        ],
        out_specs=pl.BlockSpec((tm, d), row),
        out_shape=jax.ShapeDtypeStruct((m, d), F32),
        compiler_params=pltpu.CompilerParams(vmem_limit_bytes=VMEM_LIMIT_BYTES),
        name="mlp",
    )(x1, g, scale, shift, gate2, w1, b1, w2, b2)


def _regroup_kernel(w_ref, o_ref, *, d):
    kv0 = Q_LORA_RANK
    kw0 = kv0 + 2 * A_WIDTH
    rest0 = kw0 + IDX_DIM + IDX_HEADS
    rest = 4 * B_WIDTH + 2 * d
    w = w_ref[0]
    o_ref[:, COL_AK:COL_AK + 2 * A_WIDTH] = w[:, kv0:kw0].astype(BF16)
    o_ref[:, COL_BQ:COL_BQ + rest] = w[:, rest0:rest0 + rest].astype(BF16)
    o_ref[:, COL_QL:COL_QL + Q_LORA_RANK] = w[:, :Q_LORA_RANK].astype(BF16)
    kw = w[:, kw0:kw0 + LANES]
    lane = lax.broadcasted_iota(I32, kw.shape, 1)
    o_ref[:, COL_KW:COL_KW + LANES] = jnp.where(lane < IDX_DIM + IDX_HEADS, kw, 0.0).astype(BF16)


def _regroup_w_in(w_all, layer):
    _, d, n = w_all.shape
    tr = 128
    assert COL_BQ + 4 * B_WIDTH + 2 * d == COL_QL and COL_QL + Q_LORA_RANK == COL_KW
    return pl.pallas_call(
        functools.partial(_regroup_kernel, d=d),
        grid=(d // tr,),
        in_specs=[pl.BlockSpec((1, tr, n), lambda i: (layer, i, 0))],
        out_specs=pl.BlockSpec((tr, PROJ_COLS), lambda i: (i, 0)),
        out_shape=jax.ShapeDtypeStruct((d, PROJ_COLS), BF16),
        compiler_params=pltpu.CompilerParams(vmem_limit_bytes=VMEM_LIMIT_BYTES),
        name="regroup",
    )(w_all)


def _layer(x, mod, l, p, tiles):
    bsz, seq, d = x.shape
    m = bsz * seq
    shift1, scale1, gate1, shift2, scale2, gate2 = (mod[:, k * d:(k + 1) * d].reshape(bsz, 1, d) for k in range(6))
    x2 = x.reshape(m, d)

    proj = _inproj(x2, p['norm1_g'][l][None], scale1, shift1, _regroup_w_in(p['w_in'], l), seq, tiles['tm_in'])

    tq, ck = tiles['tq'], tiles['ck']
    wuq = jnp.concatenate([p['w_uq'][l], p['w_uq_idx'][l]], axis=1).astype(BF16)
    gki = jnp.concatenate([p['k_idx_norm_g'][l], jnp.zeros((LANES - IDX_DIM,), F32)])[None]
    qt, ak, qit, ki, wt, vt = _qkprep(
        proj, p['q_lat_norm_g'][l][None], wuq,
        jnp.tile(p['q_norm_g'][l], A_HEADS)[None], jnp.tile(p['k_norm_g'][l], A_HEADS)[None], gki, bsz, seq, ck)

    topk = min(TOPK_MAX, seq // 4)
    tril = jnp.tril(jnp.ones((ck, ck), BF16))
    out_at = _dsa(qit, wt, ki.reshape(bsz, seq, IDX_DIM), qt, ak.reshape(bsz, seq, A_WIDTH), vt, tril, tq, ck, topk)

    out_b = _hgrn(proj, p['hgrn_lb'], p['hgrn_o_norm_g'][l][None], bsz, seq, tiles['tl'], l)

    x1 = _merge(out_at, out_b, proj, x2, gate1, p['w_proj_a'][l].astype(BF16), p['w_proj_b'][l].astype(BF16),
                p['w_out'][l].astype(BF16), seq, tiles['tm'])
    out = _mlp(x1, p['norm2_g'][l][None], scale2, shift2, gate2, p['w_mlp1'][l].astype(BF16), p['b_mlp1'][l][None],
               p['w_mlp2'][l].astype(BF16), p['b_mlp2'][l][None], seq, tiles['tm'], tiles['tf'])
    return out.reshape(bsz, seq, d)


def _tiles(seq):
    pick = lambda want: min(want, seq)
    return dict(tm_in=pick(256), tm=pick(512), tq=pick(256), ck=pick(512), tl=pick(256), tf=1024)


def kernel(x, c, w_ada, b_ada, norm1_g, w_in, q_lat_norm_g, w_uq, w_uq_idx, q_norm_g, k_norm_g, k_idx_norm_g,
           hgrn_lb, hgrn_o_norm_g, w_proj_a, w_proj_b, w_out, norm2_g, w_mlp1, b_mlp1, w_mlp2, b_mlp2):
    p = dict(norm1_g=norm1_g, w_in=w_in, q_lat_norm_g=q_lat_norm_g, w_uq=w_uq, w_uq_idx=w_uq_idx,
             q_norm_g=q_norm_g, k_norm_g=k_norm_g, k_idx_norm_g=k_idx_norm_g, hgrn_lb=hgrn_lb,
             hgrn_o_norm_g=hgrn_o_norm_g, w_proj_a=w_proj_a, w_proj_b=w_proj_b, w_out=w_out, norm2_g=norm2_g,
             w_mlp1=w_mlp1, b_mlp1=b_mlp1, w_mlp2=w_mlp2, b_mlp2=b_mlp2)
    bsz, seq, d = x.shape
    depth = w_ada.shape[0]
    tiles = _tiles(seq)
    c_pad = jnp.zeros((8, d), F32).at[:bsz].set(c)
    for l in range(depth):
        mod = _adaln(c_pad, w_ada[l], b_ada[l][None])[:bsz]
        x = _layer(x, mod, l, p, tiles)
    return x
```

```python
import functools

import jax
import jax.numpy as jnp
from jax import lax
from jax.experimental import pallas as pl
from jax.experimental.pallas import tpu as pltpu

F32 = jnp.float32
BF16 = jnp.bfloat16
I32 = jnp.int32

EPS = 1e-6
A_HEADS = 8
A_HEAD_DIM = 64
A_WIDTH = A_HEADS * A_HEAD_DIM
Q_LORA_RANK = 256
IDX_HEADS = 4
IDX_DIM = 64
TOPK_MAX = 256
B_WIDTH = 512
B_HEADS = 4
B_HEAD_DIM = 128
HGRN_CHUNK = 16

LANES = 128
VMEM_LIMIT_BYTES = 56 * 1024 * 1024

COL_AK, COL_AV, COL_BQ, COL_BF, COL_BI, COL_BG = 0, 512, 1024, 1536, 2048, 2560
COL_GA, COL_GB, COL_QL, COL_KW = 3072, 4096, 5120, 5376
PROJ_COLS = 5504

LOG2E = 1.4426950408889634
BISECT_STEPS = 14
SELECT_ACC_ROWS = 32
LOGIT_BOUND_LIMIT = 60.0


def _sigmoid(x):
    return 0.5 * jnp.tanh(0.5 * x) + 0.5


def _split2(x):
    hi = x.astype(BF16)
    lo = (x - hi.astype(F32)).astype(BF16)
    return hi, lo


def _split3(x):
    a = x.astype(BF16)
    r = x - a.astype(F32)
    b = r.astype(BF16)
    c = (r - b.astype(F32)).astype(BF16)
    return a, b, c


def _dot(a, b):
    return jnp.dot(a, b, preferred_element_type=F32)


def _log2(n):
    assert n > 0 and n & (n - 1) == 0, n
    return n.bit_length() - 1


def _group_ones(n, group):
    r = lax.broadcasted_iota(I32, (n, n), 0) >> _log2(group)
    c = lax.broadcasted_iota(I32, (n, n), 1) >> _log2(group)
    return jnp.where(r == c, 1.0, 0.0).astype(BF16)


def _group_mean_sq(x, ones_bd, group):
    hi, lo = _split2(x * x)
    return (_dot(hi, ones_bd) + _dot(lo, ones_bd)) * (1.0 / group)


def _fold_rows(x, op, init):
    r = init.shape[0]
    for s in range(x.shape[0] // r):
        init = op(init, x[s * r:(s + 1) * r, :])
    return init


def _adaln_kernel(c_ref, w_ref, b_ref, o_ref):
    c = c_ref[...]
    a = c * _sigmoid(c)
    a1, a2, a3 = _split3(a)
    w1, w2, w3 = _split3(w_ref[...])
    acc = _dot(a1, w1) + (_dot(a1, w2) + _dot(a2, w1)) + (_dot(a2, w2) + _dot(a1, w3) + _dot(a3, w1))
    o_ref[...] = acc + b_ref[...]


def _adaln(c_pad, w, b):
    rows, d = c_pad.shape
    n = w.shape[1]
    tn = 1536
    return pl.pallas_call(
        _adaln_kernel,
        grid=(n // tn,),
        in_specs=[
            pl.BlockSpec((rows, d), lambda j: (0, 0)),
            pl.BlockSpec((d, tn), lambda j: (0, j)),
            pl.BlockSpec((1, tn), lambda j: (0, j)),
        ],
        out_specs=pl.BlockSpec((rows, tn), lambda j: (0, j)),
        out_shape=jax.ShapeDtypeStruct((rows, n), F32),
        compiler_params=pltpu.CompilerParams(vmem_limit_bytes=VMEM_LIMIT_BYTES),
        name="adaln",
    )(c_pad, w, b)


def _inproj_kernel(x_ref, g_ref, sc_ref, sh_ref, w_ref, o_ref):
    x = x_ref[...]
    ms = jnp.mean(x * x, axis=-1, keepdims=True)
    h = x * lax.rsqrt(ms + EPS) * g_ref[...]
    h = h * (1.0 + sc_ref[0]) + sh_ref[0]
    o_ref[...] = _dot(h.astype(BF16), w_ref[...])


def _inproj(x2, g, scale, shift, w, seq, tm):
    m, d = x2.shape
    n = w.shape[1]
    tpb = seq // tm
    return pl.pallas_call(
        _inproj_kernel,
        grid=(m // tm,),
        in_specs=[
            pl.BlockSpec((tm, d), lambda i: (i, 0)),
            pl.BlockSpec((1, d), lambda i: (0, 0)),
            pl.BlockSpec((1, 1, d), lambda i: (i // tpb, 0, 0)),
            pl.BlockSpec((1, 1, d), lambda i: (i // tpb, 0, 0)),
            pl.BlockSpec((d, n), lambda i: (0, 0), pipeline_mode=pl.Buffered(1)),
        ],
        out_specs=pl.BlockSpec((tm, n), lambda i: (i, 0)),
        out_shape=jax.ShapeDtypeStruct((m, n), F32),
        compiler_params=pltpu.CompilerParams(vmem_limit_bytes=VMEM_LIMIT_BYTES),
        name="inproj",
    )(x2, g, scale, shift, w)


def _qkprep_kernel(ak_ref, av_ref, ql_ref, kw_ref, gql_ref, wuq_ref, gq_ref, gk_ref, gki_ref,
                   qt_out, ak_out, qit_out, ki_out, wt_out, vt_out):
    tm = ak_ref.shape[0]
    ones64 = _group_ones(A_WIDTH, A_HEAD_DIM)
    ql = ql_ref[...]
    ql = ql * lax.rsqrt(jnp.mean(ql * ql, axis=-1, keepdims=True) + EPS) * gql_ref[...]
    up = _dot(ql.astype(BF16), wuq_ref[...])
    aq = up[:, :A_WIDTH]
    aq = aq * lax.rsqrt(_group_mean_sq(aq, ones64, A_HEAD_DIM) + EPS) * gq_ref[...]
    aq_t = (aq * (A_HEAD_DIM ** -0.5 * LOG2E)).T
    zero_half = jnp.zeros((A_HEAD_DIM, tm), BF16)
    for h in range(A_HEADS):
        e = h % 2
        qt_out[0, h, e * A_HEAD_DIM:(e + 1) * A_HEAD_DIM, :] = aq_t[h * A_HEAD_DIM:(h + 1) * A_HEAD_DIM].astype(BF16)
        qt_out[0, h, (1 - e) * A_HEAD_DIM:(2 - e) * A_HEAD_DIM, :] = zero_half
    qit_out[0] = up[:, A_WIDTH:].T.reshape(IDX_HEADS, IDX_DIM, tm).astype(BF16)
    ak = ak_ref[...]
    ak = ak * lax.rsqrt(_group_mean_sq(ak, ones64, A_HEAD_DIM) + EPS) * gk_ref[...]
    ak_out[...] = ak.astype(BF16)
    vt_out[0, 0] = av_ref[...].T.reshape(A_HEADS, A_HEAD_DIM, tm).astype(BF16)
    kw = kw_ref[...]
    lane = lax.broadcasted_iota(I32, kw.shape, 1)
    ksq = jnp.where(lane < IDX_DIM, kw * kw, 0.0)
    kms = jnp.sum(ksq, axis=-1, keepdims=True) * (1.0 / IDX_DIM)
    kn = kw * lax.rsqrt(kms + EPS) * gki_ref[...]
    ki_out[...] = kn[:, :IDX_DIM].astype(BF16)
    idx_scale = (IDX_DIM ** -0.5) * (IDX_HEADS ** -0.5)
    w_rows = jnp.where((lane >= IDX_DIM) & (lane < IDX_DIM + IDX_HEADS), kw * idx_scale, 0.0).T
    wt_out[0] = w_rows[IDX_DIM:IDX_DIM + 8]


def _qkprep(proj, gql, wuq, gq, gk, gki, bsz, seq, tm):
    m = proj.shape[0]
    tpb = seq // tm
    row = lambda i: (i, 0)
    const = lambda i: (0, 0)
    return pl.pallas_call(
        _qkprep_kernel,
        grid=(m // tm,),
        in_specs=[
            pl.BlockSpec((tm, A_WIDTH), lambda i: (i, COL_AK // A_WIDTH)),
            pl.BlockSpec((tm, A_WIDTH), lambda i: (i, COL_AV // A_WIDTH)),
            pl.BlockSpec((tm, Q_LORA_RANK), lambda i: (i, COL_QL // Q_LORA_RANK)),
            pl.BlockSpec((tm, LANES), lambda i: (i, COL_KW // LANES)),
            pl.BlockSpec((1, Q_LORA_RANK), const),
            pl.BlockSpec(wuq.shape, const),
            pl.BlockSpec((1, A_WIDTH), const),
            pl.BlockSpec((1, A_WIDTH), const),
            pl.BlockSpec((1, LANES), const),
        ],
        out_specs=[
            pl.BlockSpec((1, A_HEADS, LANES, tm), lambda i: (i // tpb, 0, 0, i % tpb)),
            pl.BlockSpec((tm, A_WIDTH), row),
            pl.BlockSpec((1, IDX_HEADS, IDX_DIM, tm), lambda i: (i // tpb, 0, 0, i % tpb)),
            pl.BlockSpec((tm, IDX_DIM), row),
            pl.BlockSpec((1, 8, tm), lambda i: (i // tpb, 0, i % tpb)),
            pl.BlockSpec((1, 1, A_HEADS, A_HEAD_DIM, tm), lambda i: (i // tpb, i % tpb, 0, 0, 0)),
        ],
        out_shape=[
            jax.ShapeDtypeStruct((bsz, A_HEADS, LANES, seq), BF16),
            jax.ShapeDtypeStruct((m, A_WIDTH), BF16),
            jax.ShapeDtypeStruct((bsz, IDX_HEADS, IDX_DIM, seq), BF16),
            jax.ShapeDtypeStruct((m, IDX_DIM), BF16),
            jax.ShapeDtypeStruct((bsz, 8, seq), F32),
            jax.ShapeDtypeStruct((bsz, tpb, A_HEADS, A_HEAD_DIM, tm), BF16),
        ],
        compiler_params=pltpu.CompilerParams(vmem_limit_bytes=VMEM_LIMIT_BYTES),
        name="qkprep",
    )(proj, proj, proj, proj, gql, wuq, gq, gk, gki)


def _dsa_kernel(qit_ref, wt_ref, ki_ref, qt_ref, k_ref, vt_ref, tril_ref, o_ref,
                sc_ref, acc_ref, p_ref, kn_ref, *, tq, ck, topk, nbis):
    i = pl.program_id(1)
    n_chunks = ((i + 1) * tq + ck - 1) // ck
    kf = float(topk)
    neg_inf = float("-inf")
    pos_inf = float("inf")
    ar = SELECT_ACC_ROWS
    full_acc = lambda val: jnp.full((ar, tq), val, F32)
    row_min = lambda x: jnp.min(x, axis=0, keepdims=True)
    row_max = lambda x: jnp.max(x, axis=0, keepdims=True)
    row_sum = lambda x: jnp.sum(x, axis=0, keepdims=True)
    chunk_rows = lambda jc: pl.ds(pl.multiple_of(jc * ck, ck), ck)

    @pl.when(i == 0)
    def _():
        head_of_lane = lax.broadcasted_iota(I32, (A_WIDTH, LANES), 0) >> _log2(A_HEAD_DIM)
        pick = jnp.where(head_of_lane == lax.broadcasted_iota(I32, (A_WIDTH, LANES), 1), 1.0, 0.0).astype(BF16)

        def kn_body(jc, mx):
            k = k_ref[0, chunk_rows(jc), :].astype(F32)
            return _fold_rows(_dot((k * k).astype(BF16), pick), jnp.maximum, mx)

        mx = lax.fori_loop(0, k_ref.shape[1] // ck, kn_body, jnp.zeros((8, LANES), F32))
        kn_ref[...] = jnp.broadcast_to(row_max(mx), (8, LANES))

    t_pos = i * tq + lax.broadcasted_iota(I32, (ck, tq), 1)
    s_iota = lax.broadcasted_iota(I32, (ck, tq), 0)

    def score_body(masked, jc, carry):
        mn, mx = carry
        kc = ki_ref[0, chunk_rows(jc), :]
        score = jnp.zeros((ck, tq), F32)
        for h in range(IDX_HEADS):
            score = score + wt_ref[0, h:h + 1, :] * jnp.maximum(_dot(kc, qit_ref[0, h]), 0.0)
        if masked:
            causal = (s_iota + jc * ck) <= t_pos
            sc_ref[jc] = jnp.where(causal, score, neg_inf)
            mn = _fold_rows(jnp.where(causal, score, pos_inf), jnp.minimum, mn)
            mx = _fold_rows(jnp.where(causal, score, neg_inf), jnp.maximum, mx)
        else:
            sc_ref[jc] = score
            mn = _fold_rows(score, jnp.minimum, mn)
            mx = _fold_rows(score, jnp.maximum, mx)
        return mn, mx

    n_full = (i * tq + 1) // ck
    carry = lax.fori_loop(0, n_full, functools.partial(score_body, False), (full_acc(pos_inf), full_acc(neg_inf)))
    mn_acc, mx_acc = lax.fori_loop(n_full, n_chunks, functools.partial(score_body, True), carry)
    rmin, rmax = row_min(mn_acc), row_max(mx_acc)

    def count_ge(cand):
        cand_b = jnp.broadcast_to(cand, (ar, tq))

        def body(jc, acc):
            return _fold_rows(sc_ref[jc], lambda a, x: a + jnp.where(x >= cand_b, 1.0, 0.0), acc)

        return row_sum(lax.fori_loop(0, n_chunks, body, full_acc(0.0)))

    def snap(lo, hi):
        lo_b = jnp.broadcast_to(lo, (ar, tq))
        hi_b = jnp.broadcast_to(hi, (ar, tq))

        def body(jc, carry):
            x = sc_ref[jc]
            a = _fold_rows(x, lambda a, x: jnp.minimum(a, jnp.where(x >= lo_b, x, pos_inf)), carry[0])
            b = _fold_rows(x, lambda b, x: jnp.maximum(b, jnp.where(x < hi_b, x, neg_inf)), carry[1])
            return a, b

        a, b = lax.fori_loop(0, n_chunks, body, (full_acc(pos_inf), full_acc(neg_inf)))
        return row_min(a), row_max(b)

    def probe(cand):
        cand_b = jnp.broadcast_to(cand, (ar, tq))

        def body(jc, carry):
            cnt, a, b = carry
            blk = sc_ref[jc]
            for s in range(ck // ar):
                x = blk[s * ar:(s + 1) * ar, :]
                ge = x >= cand_b
                cnt = cnt + jnp.where(ge, 1.0, 0.0)
                a = jnp.minimum(a, jnp.where(ge, x, pos_inf))
                b = jnp.maximum(b, jnp.where(ge, neg_inf, x))
            return cnt, a, b

        cnt, a, b = lax.fori_loop(0, n_chunks, body, (full_acc(0.0), full_acc(pos_inf), full_acc(neg_inf)))
        return row_sum(cnt), row_min(a), row_max(b)

    n_valid = (i * tq + 1 + lax.broadcasted_iota(I32, (1, tq), 1)).astype(F32)
    small = n_valid <= kf

    def interpolate(lo, hi, c_lo, c_hi, w_lo, w_hi):
        f_lo = (c_lo - kf + 0.5) * w_lo
        f_hi = (kf - c_hi - 0.5) * w_hi
        return lo + (hi - lo) * jnp.clip(f_lo / jnp.maximum(f_lo + f_hi, 0.5), 0.1, 0.9)

    def illinois(up, dn, w_lo, w_hi, last):
        w_hi = jnp.where(up, jnp.where(last > 0.0, w_hi * 0.5, 1.0), jnp.where(dn, 1.0, w_hi))
        w_lo = jnp.where(dn, jnp.where(last < 0.0, w_lo * 0.5, 1.0), jnp.where(up, 1.0, w_lo))
        return w_lo, w_hi, jnp.where(up, 1.0, jnp.where(dn, -1.0, last))

    def zero_stats():
        def body(jc, carry):
            c_ge0, c_pos, min_pos = carry
            blk = sc_ref[jc]
            for s in range(ck // ar):
                x = blk[s * ar:(s + 1) * ar, :]
                pos = x > 0.0
                c_ge0 = c_ge0 + jnp.where(x >= 0.0, 1.0, 0.0)
                c_pos = c_pos + jnp.where(pos, 1.0, 0.0)
                min_pos = jnp.minimum(min_pos, jnp.where(pos, x, pos_inf))
            return c_ge0, c_pos, min_pos

        c_ge0, c_pos, min_pos = lax.fori_loop(0, n_chunks, body, (full_acc(0.0), full_acc(0.0), full_acc(pos_inf)))
        return row_sum(c_ge0), row_sum(c_pos), row_min(min_pos)

    c_ge0, c_pos, min_pos = zero_stats()
    below_zero = c_ge0 < kf
    above_zero = c_pos >= kf
    at_zero = jnp.logical_not(below_zero | above_zero)
    hi0 = rmax + jnp.maximum(jnp.abs(rmax) * 1e-6, 1e-30)
    lo = jnp.where(above_zero, min_pos, jnp.where(at_zero, 0.0, rmin))
    c_lo = jnp.where(above_zero, c_pos, jnp.where(at_zero, c_ge0, n_valid))
    hi = jnp.where(above_zero, hi0, 0.0)
    c_hi = jnp.where(above_zero, 0.0, jnp.where(at_zero, c_pos, c_ge0))

    def search_body(_, st):
        lo, hi, c_lo, c_hi, w_lo, w_hi, last = st
        mid = interpolate(lo, hi, c_lo, c_hi, w_lo, w_hi)
        ok = (mid > lo) & (mid < hi)
        c = count_ge(mid)
        up = ok & (c >= kf)
        dn = ok & (c < kf)
        w_lo, w_hi, last = illinois(up, dn, w_lo, w_hi, last)
        return (jnp.where(up, mid, lo), jnp.where(dn, mid, hi), jnp.where(up, c, c_lo), jnp.where(dn, c, c_hi),
                w_lo, w_hi, last)

    zeros_q, ones_q = jnp.zeros((1, tq), F32), jnp.ones((1, tq), F32)
    lo, hi, c_lo, c_hi, _, _, _ = lax.fori_loop(0, nbis, search_body,
                                                (lo, hi, c_lo, c_hi, ones_q, ones_q, zeros_q))

    lo, hi = snap(lo, hi)
    lo, hi = jnp.where(at_zero, 0.0, lo), jnp.where(at_zero, 0.0, hi)

    def active_of(lo, hi, c_lo):
        return jnp.logical_not(small) & (lo < hi) & (c_lo != kf)

    def snap_body(st):
        lo, hi, c_lo, c_hi, w_lo, w_hi, last, _ = st
        act = active_of(lo, hi, c_lo)
        mid = interpolate(lo, hi, c_lo, c_hi, w_lo, w_hi)
        mid = jnp.where((c_hi == kf - 1.0) | (mid <= lo) | (mid > hi), hi, mid)
        c, a, b = probe(mid)
        up = act & (c >= kf)
        dn = act & (c < kf)
        w_lo, w_hi, last = illinois(up, dn, w_lo, w_hi, last)
        lo, c_lo = jnp.where(up, a, lo), jnp.where(up, c, c_lo)
        hi, c_hi = jnp.where(dn, b, hi), jnp.where(dn, c, c_hi)
        return lo, hi, c_lo, c_hi, w_lo, w_hi, last, jnp.max(jnp.where(active_of(lo, hi, c_lo), 1.0, 0.0))

    flag0 = jnp.max(jnp.where(active_of(lo, hi, c_lo), 1.0, 0.0))
    lo, hi, c_lo, c_hi = lax.while_loop(lambda st: st[7] > 0.0, snap_body,
                                        (lo, hi, c_lo, c_hi, ones_q, ones_q, zeros_q, flag0))[:4]

    tau = jnp.where(small, rmin, lo)
    excess = jnp.logical_not(small) & (lo == hi) & (c_lo > kf)
    need = jnp.where(excess, kf - c_hi, 4.0 * 65536.0 * 65536.0)

    @pl.when(jnp.max(jnp.where(excess, 1.0, 0.0)) > 0.0)
    def _():
        def tie_body(jc, carry):
            blk = sc_ref[jc]
            eq = blk == tau
            pc = _dot(tril_ref[...], jnp.where(eq, 1.0, 0.0).astype(BF16)) + carry
            sc_ref[jc] = jnp.where(eq & (pc > need), neg_inf, blk)
            return pc[ck - 1:ck, :]

        lax.fori_loop(0, n_chunks, tie_body, jnp.zeros((1, tq), F32))

    def logits_t(jc, h):
        kc = k_ref[0, chunk_rows(jc), (h // 2) * LANES:(h // 2 + 1) * LANES]
        return _dot(kc, qt_ref[0, h])

    m_bound = []
    bmax = jnp.zeros((1, 1), F32)
    for h in range(A_HEADS):
        qf = qt_ref[0, h].astype(F32)
        bound = jnp.sqrt(row_sum(qf * qf) * kn_ref[0:1, h:h + 1])
        m_bound.append(bound)
        bmax = jnp.maximum(bmax, jnp.max(bound, axis=1, keepdims=True))

    def exact_max(_):
        def max_body(jc, mx):
            sel = sc_ref[jc] >= tau
            return tuple(_fold_rows(jnp.where(sel, logits_t(jc, h), neg_inf), jnp.maximum, mx[h])
                         for h in range(A_HEADS))

        mx = lax.fori_loop(0, n_chunks, max_body, tuple(jnp.full((8, tq), neg_inf, F32) for _ in range(A_HEADS)))
        return tuple(row_max(m) for m in mx)

    m_ref_vals = lax.cond(bmax[0, 0] > LOGIT_BOUND_LIMIT, exact_max, lambda _: tuple(m_bound), 0)

    ones_rows = jnp.ones((16, ck), BF16)
    acc_ref[...] = jnp.zeros(acc_ref.shape, F32)

    def pv_stage(jc, h):
        lhs = jnp.concatenate([vt_ref[0, jc, h], ones_rows], axis=0)
        acc_ref[h] += _dot(lhs, p_ref[h])

    def qk_stage(jc, sel, h):
        p_ref[h] = jnp.where(sel, jnp.exp2(logits_t(jc, h) - m_ref_vals[h]), 0.0).astype(BF16)

    sel0 = sc_ref[0] >= tau
    for h in range(A_HEADS):
        qk_stage(0, sel0, h)

    def att_body(jc, carry):
        sel = sc_ref[jc] >= tau
        for h in range(A_HEADS):
            pv_stage(jc - 1, h)
            qk_stage(jc, sel, h)
        return carry

    lax.fori_loop(1, n_chunks, att_body, 0)
    for h in range(A_HEADS):
        pv_stage(n_chunks - 1, h)
    for h in range(A_HEADS):
        a = acc_ref[h]
        o_ref[0, h] = (a[:A_HEAD_DIM] * (1.0 / a[A_HEAD_DIM:A_HEAD_DIM + 1])).astype(o_ref.dtype)


def _dsa(qit, wt, ki, qt, k, vt, tril, tq, ck, topk):
    bsz, hi, di, seq = qit.shape
    nc = seq // ck
    kern = functools.partial(_dsa_kernel, tq=tq, ck=ck, topk=topk, nbis=BISECT_STEPS)
    return pl.pallas_call(
        kern,
        grid=(bsz, seq // tq),
        in_specs=[
            pl.BlockSpec((1, hi, di, tq), lambda b, i: (b, 0, 0, i)),
            pl.BlockSpec((1, 8, tq), lambda b, i: (b, 0, i)),
            pl.BlockSpec((1, seq, di), lambda b, i: (b, 0, 0)),
            pl.BlockSpec((1, A_HEADS, LANES, tq), lambda b, i: (b, 0, 0, i)),
            pl.BlockSpec((1, seq, A_WIDTH), lambda b, i: (b, 0, 0)),
            pl.BlockSpec((1, nc, A_HEADS, A_HEAD_DIM, ck), lambda b, i: (b, 0, 0, 0, 0)),
            pl.BlockSpec((ck, ck), lambda b, i: (0, 0)),
        ],
        out_specs=pl.BlockSpec((1, A_HEADS, A_HEAD_DIM, tq), lambda b, i: (b, 0, 0, i)),
        out_shape=jax.ShapeDtypeStruct((bsz, A_HEADS, A_HEAD_DIM, seq), BF16),
        scratch_shapes=[
            pltpu.VMEM((nc, ck, tq), F32),
            pltpu.VMEM((A_HEADS, A_HEAD_DIM + 16, tq), F32),
            pltpu.VMEM((A_HEADS, ck, tq), BF16),
            pltpu.VMEM((8, LANES), F32),
        ],
        compiler_params=pltpu.CompilerParams(
            dimension_semantics=("arbitrary", "arbitrary"), vmem_limit_bytes=VMEM_LIMIT_BYTES),
        name="dsa",
    )(qit, wt, ki, qt, k, vt, tril)


def _hgrn_kernel(bq_ref, bf_ref, bi_ref, bg_ref, lb_ref, go_ref, o_ref, st_ref, oi_ref, *, tl, layer):
    cs = HGRN_CHUNK
    nch = tl // cs

    @pl.when(pl.program_id(1) == 0)
    def _():
        st_ref[...] = jnp.zeros(st_ref.shape, F32)

    lbr = lb_ref[...]
    slots = [lbr[k:k + 1] for k in range(lbr.shape[0])]
    mx = functools.reduce(jnp.maximum, slots)
    es = [jnp.exp(s - mx) for s in slots]
    lb = functools.reduce(jnp.add, es[:layer + 1]) / functools.reduce(jnp.add, es)

    bq = bq_ref[...]
    q = bq * _sigmoid(bq)
    f = lb + (1.0 - lb) * _sigmoid(bf_ref[...])
    kk = 1.0 - f
    g = jnp.log(f)
    v = bi_ref[...]

    r_i = lax.broadcasted_iota(I32, (tl, tl), 0)
    c_i = lax.broadcasted_iota(I32, (tl, tl), 1)
    same = (r_i >> _log2(cs)) == (c_i >> _log2(cs))
    tri = jnp.where(same & (c_i <= r_i), 1.0, 0.0).astype(BF16)
    blk = jnp.where(same, 1.0, 0.0).astype(BF16)
    g1, g2, g3 = _split3(g)
    b = _dot(tri, g1) + _dot(tri, g2) + _dot(tri, g3)
    bend = _dot(blk, g1) + _dot(blk, g2) + _dot(blk, g3)

    bk = b - jnp.log(kk)
    pos = lax.broadcasted_iota(I32, (tl, B_WIDTH), 0) & (cs - 1)
    o = jnp.zeros((tl, B_WIDTH), F32)
    for r in range(cs):
        if r == 0:
            bk_s, v_s = bk, v
        else:
            bk_s = pltpu.roll(bk, r, axis=0)
            v_s = pltpu.roll(v, r, axis=0)
        e = q * jnp.exp(jnp.where(pos >= r, b - bk_s, -jnp.inf))
        parts = []
        for h in range(B_HEADS):
            sl = slice(h * B_HEAD_DIM, (h + 1) * B_HEAD_DIM)
            parts.append(jnp.sum(e[:, sl], axis=-1, keepdims=True) * v_s[:, sl])
        o = o + jnp.concatenate(parts, axis=1)

    qe = (q * jnp.exp(b)).astype(BF16)
    kd = (kk * jnp.exp(bend - b)).astype(BF16)
    vb = v.astype(BF16)
    dec = jnp.exp(bend)
    for c in range(nch):
        rs = slice(c * cs, (c + 1) * cs)
        for h in range(B_HEADS):
            sl = slice(h * B_HEAD_DIM, (h + 1) * B_HEAD_DIM)
            st = st_ref[h]
            oi_ref[rs, sl] = lax.dot_general(qe[rs, sl], st.astype(BF16), (((1,), (1,)), ((), ())),
                                             preferred_element_type=F32)
            upd = lax.dot_general(vb[rs, sl], kd[rs, sl], (((0,), (0,)), ((), ())),
                                  preferred_element_type=F32)
            st_ref[h] = st * dec[c * cs:c * cs + 1, sl] + upd
    o = o + oi_ref[...]

    parts = []
    for h in range(B_HEADS):
        sl = slice(h * B_HEAD_DIM, (h + 1) * B_HEAD_DIM)
        oh = o[:, sl]
        parts.append(oh * lax.rsqrt(jnp.mean(oh * oh, axis=-1, keepdims=True) + EPS))
    on = jnp.concatenate(parts, axis=1) * go_ref[...]
    bg = bg_ref[...]
    o_ref[...] = (on * (bg * _sigmoid(bg))).astype(o_ref.dtype)


def _hgrn(proj, lb_table, go, bsz, seq, tl, layer):
    m = proj.shape[0]
    tpb = seq // tl
    col = lambda c: (lambda b, t: (b * tpb + t, c // B_WIDTH))
    return pl.pallas_call(
        functools.partial(_hgrn_kernel, tl=tl, layer=layer),
        grid=(bsz, tpb),
        in_specs=[
            pl.BlockSpec((tl, B_WIDTH), col(COL_BQ)),
            pl.BlockSpec((tl, B_WIDTH), col(COL_BF)),
            pl.BlockSpec((tl, B_WIDTH), col(COL_BI)),
            pl.BlockSpec((tl, B_WIDTH), col(COL_BG)),
            pl.BlockSpec(lb_table.shape, lambda b, t: (0, 0)),
            pl.BlockSpec((1, B_WIDTH), lambda b, t: (0, 0)),
        ],
        out_specs=pl.BlockSpec((tl, B_WIDTH), lambda b, t: (b * tpb + t, 0)),
        out_shape=jax.ShapeDtypeStruct((m, B_WIDTH), BF16),
        scratch_shapes=[
            pltpu.VMEM((B_HEADS, B_HEAD_DIM, B_HEAD_DIM), F32),
            pltpu.VMEM((tl, B_WIDTH), F32),
        ],
        compiler_params=pltpu.CompilerParams(
            dimension_semantics=("arbitrary", "arbitrary"), vmem_limit_bytes=VMEM_LIMIT_BYTES),
        name="hgrn",
    )(proj, proj, proj, proj, lb_table, go)


def _tail_kernel(oat_ref, ob_ref, ga_ref, gb_ref, x_ref, g1_ref, wa_ref, wb_ref, wo_ref,
                 n2_ref, sc_ref, sh_ref, g2_ref, w1_ref, b1_ref, w2_ref, b2_ref, o_ref, *, tf):
    tm = x_ref.shape[0]
    oa = oat_ref[0].reshape(A_WIDTH, tm).astype(F32).T.astype(BF16)
    pa = _dot(oa, wa_ref[...])
    pb = _dot(ob_ref[...], wb_ref[...])
    merged = _sigmoid(ga_ref[...]) * pa + _sigmoid(gb_ref[...]) * pb
    x = x_ref[...] + g1_ref[0] * _dot(merged.astype(BF16), wo_ref[...])
    ms = jnp.mean(x * x, axis=-1, keepdims=True)
    h = x * lax.rsqrt(ms + EPS) * n2_ref[...]
    h = (h * (1.0 + sc_ref[0]) + sh_ref[0]).astype(BF16)
    dff = w1_ref.shape[1]
    y = jnp.zeros(x.shape, F32)
    for c in range(dff // tf):
        cs = slice(c * tf, (c + 1) * tf)
        a = jnp.maximum(_dot(h, w1_ref[:, cs]) + b1_ref[:, cs], 0.0)
        y = y + _dot((a * a).astype(BF16), w2_ref[cs, :])
    o_ref[...] = x + g2_ref[0] * (y + b2_ref[...])


def _tail(oat, ob, proj, x2, gate1, wa, wb, wo, n2, scale2, shift2, gate2, w1, b1, w2, b2, seq, tm, tf):
    m, d = x2.shape
    tpb = seq // tm
    row = lambda i: (i, 0)
    bat = lambda i: (i // tpb, 0, 0)
    resident = lambda a: pl.BlockSpec(a.shape, lambda i: (0, 0), pipeline_mode=pl.Buffered(1))
    return pl.pallas_call(
        functools.partial(_tail_kernel, tf=tf),
        grid=(m // tm,),
        in_specs=[
            pl.BlockSpec((1, A_HEADS, A_HEAD_DIM, tm), lambda i: (i // tpb, 0, 0, i % tpb)),
            pl.BlockSpec((tm, B_WIDTH), row),
            pl.BlockSpec((tm, d), lambda i: (i, COL_GA // d)),
            pl.BlockSpec((tm, d), lambda i: (i, COL_GB // d)),
            pl.BlockSpec((tm, d), row),
            pl.BlockSpec((1, 1, d), bat),
            resident(wa), resident(wb), resident(wo),
            resident(n2),
            pl.BlockSpec((1, 1, d), bat),
            pl.BlockSpec((1, 1, d), bat),
            pl.BlockSpec((1, 1, d), bat),
            resident(w1), resident(b1), resident(w2), resident(b2),
        ],
        out_specs=pl.BlockSpec((tm, d), row),
        out_shape=jax.ShapeDtypeStruct((m, d), F32),
        compiler_params=pltpu.CompilerParams(vmem_limit_bytes=VMEM_LIMIT_BYTES),
        name="tail",
    )(oat, ob, proj, proj, x2, gate1, wa, wb, wo, n2, scale2, shift2, gate2, w1, b1, w2, b2)


def _regroup_kernel(w_ref, o_ref, *, d):
    kv0 = Q_LORA_RANK
    kw0 = kv0 + 2 * A_WIDTH
    rest0 = kw0 + IDX_DIM + IDX_HEADS
    rest = 4 * B_WIDTH + 2 * d
    w = w_ref[0]
    o_ref[:, COL_AK:COL_AK + 2 * A_WIDTH] = w[:, kv0:kw0].astype(BF16)
    o_ref[:, COL_BQ:COL_BQ + rest] = w[:, rest0:rest0 + rest].astype(BF16)
    o_ref[:, COL_QL:COL_QL + Q_LORA_RANK] = w[:, :Q_LORA_RANK].astype(BF16)
    kw = w[:, kw0:kw0 + LANES]
    lane = lax.broadcasted_iota(I32, kw.shape, 1)
    o_ref[:, COL_KW:COL_KW + LANES] = jnp.where(lane < IDX_DIM + IDX_HEADS, kw, 0.0).astype(BF16)


def _regroup_w_in(w_all, layer):
    _, d, n = w_all.shape
    tr = 128
    assert COL_BQ + 4 * B_WIDTH + 2 * d == COL_QL and COL_QL + Q_LORA_RANK == COL_KW
    return pl.pallas_call(
        functools.partial(_regroup_kernel, d=d),
        grid=(d // tr,),
        in_specs=[pl.BlockSpec((1, tr, n), lambda i: (layer, i, 0))],
        out_specs=pl.BlockSpec((tr, PROJ_COLS), lambda i: (i, 0)),
        out_shape=jax.ShapeDtypeStruct((d, PROJ_COLS), BF16),
        compiler_params=pltpu.CompilerParams(vmem_limit_bytes=VMEM_LIMIT_BYTES),
        name="regroup",
    )(w_all)


def _layer(x, mod, l, p, tiles):
    bsz, seq, d = x.shape
    m = bsz * seq
    shift1, scale1, gate1, shift2, scale2, gate2 = (mod[:, k * d:(k + 1) * d].reshape(bsz, 1, d) for k in range(6))
    x2 = x.reshape(m, d)

    proj = _inproj(x2, p['norm1_g'][l][None], scale1, shift1, _regroup_w_in(p['w_in'], l), seq, tiles['tm_in'])

    tq, ck = tiles['tq'], tiles['ck']
    wuq = jnp.concatenate([p['w_uq'][l], p['w_uq_idx'][l]], axis=1).astype(BF16)
    gki = jnp.concatenate([p['k_idx_norm_g'][l], jnp.zeros((LANES - IDX_DIM,), F32)])[None]
    qt, ak, qit, ki, wt, vt = _qkprep(
        proj, p['q_lat_norm_g'][l][None], wuq,
        jnp.tile(p['q_norm_g'][l], A_HEADS)[None], jnp.tile(p['k_norm_g'][l], A_HEADS)[None], gki, bsz, seq, ck)

    topk = min(TOPK_MAX, seq // 4)
    tril = jnp.tril(jnp.ones((ck, ck), BF16))
    out_at = _dsa(qit, wt, ki.reshape(bsz, seq, IDX_DIM), qt, ak.reshape(bsz, seq, A_WIDTH), vt, tril, tq, ck, topk)

    out_b = _hgrn(proj, p['hgrn_lb'], p['hgrn_o_norm_g'][l][None], bsz, seq, tiles['tl'], l)

    out = _tail(out_at, out_b, proj, x2, gate1, p['w_proj_a'][l].astype(BF16), p['w_proj_b'][l].astype(BF16),
                p['w_out'][l].astype(BF16), p['norm2_g'][l][None], scale2, shift2, gate2,
                p['w_mlp1'][l].astype(BF16), p['b_mlp1'][l][None], p['w_mlp2'][l].astype(BF16), p['b_mlp2'][l][None],
                seq, tiles['tm'], tiles['tf'])
    return out.reshape(bsz, seq, d)


def _tiles(seq):
    pick = lambda want: min(want, seq)
    return dict(tm_in=pick(512), tm=pick(512), tq=pick(256), ck=pick(512), tl=pick(256), tf=1024)


def kernel(x, c, w_ada, b_ada, norm1_g, w_in, q_lat_norm_g, w_uq, w_uq_idx, q_norm_g, k_norm_g, k_idx_norm_g,
           hgrn_lb, hgrn_o_norm_g, w_proj_a, w_proj_b, w_out, norm2_g, w_mlp1, b_mlp1, w_mlp2, b_mlp2):
    p = dict(norm1_g=norm1_g, w_in=w_in, q_lat_norm_g=q_lat_norm_g, w_uq=w_uq, w_uq_idx=w_uq_idx,
             q_norm_g=q_norm_g, k_norm_g=k_norm_g, k_idx_norm_g=k_idx_norm_g, hgrn_lb=hgrn_lb,
             hgrn_o_norm_g=hgrn_o_norm_g, w_proj_a=w_proj_a, w_proj_b=w_proj_b, w_out=w_out, norm2_g=norm2_g,
             w_mlp1=w_mlp1, b_mlp1=b_mlp1, w_mlp2=w_mlp2, b_mlp2=b_mlp2)
    bsz, seq, d = x.shape
    depth = w_ada.shape[0]
    tiles = _tiles(seq)
    c_pad = jnp.zeros((8, d), F32).at[:bsz].set(c)
    for l in range(depth):
        mod = _adaln(c_pad, w_ada[l], b_ada[l][None])[:bsz]
        x = _layer(x, mod, l, p, tiles)
    return x
```

```python
import functools

import jax
import jax.numpy as jnp
from jax import lax
from jax.experimental import pallas as pl
from jax.experimental.pallas import tpu as pltpu

F32 = jnp.float32
BF16 = jnp.bfloat16
I32 = jnp.int32

EPS = 1e-6
A_HEADS = 8
A_HEAD_DIM = 64
A_WIDTH = A_HEADS * A_HEAD_DIM
Q_LORA_RANK = 256
IDX_HEADS = 4
IDX_DIM = 64
TOPK_MAX = 256
B_WIDTH = 512
B_HEADS = 4
B_HEAD_DIM = 128
HGRN_CHUNK = 16

LANES = 128
VMEM_LIMIT_BYTES = 56 * 1024 * 1024

COL_AK, COL_AV, COL_BQ, COL_BF, COL_BI, COL_BG = 0, 512, 1024, 1536, 2048, 2560
COL_GA, COL_GB, COL_QL, COL_KW = 3072, 4096, 5120, 5376
PROJ_COLS = 5504

LOG2E = 1.4426950408889634
BISECT_STEPS = 12
SELECT_ACC_ROWS = 32
LOGIT_BOUND_LIMIT = 60.0


def _sigmoid(x):
    return 0.5 * jnp.tanh(0.5 * x) + 0.5


def _split2(x):
    hi = x.astype(BF16)
    lo = (x - hi.astype(F32)).astype(BF16)
    return hi, lo


def _split3(x):
    a = x.astype(BF16)
    r = x - a.astype(F32)
    b = r.astype(BF16)
    c = (r - b.astype(F32)).astype(BF16)
    return a, b, c


def _dot(a, b):
    return jnp.dot(a, b, preferred_element_type=F32)


def _log2(n):
    assert n > 0 and n & (n - 1) == 0, n
    return n.bit_length() - 1


def _group_ones(n, group):
    r = lax.broadcasted_iota(I32, (n, n), 0) >> _log2(group)
    c = lax.broadcasted_iota(I32, (n, n), 1) >> _log2(group)
    return jnp.where(r == c, 1.0, 0.0).astype(BF16)


def _group_mean_sq(x, ones_bd, group):
    hi, lo = _split2(x * x)
    return (_dot(hi, ones_bd) + _dot(lo, ones_bd)) * (1.0 / group)


def _fold_rows(x, op, init):
    r = init.shape[0]
    for s in range(x.shape[0] // r):
        init = op(init, x[s * r:(s + 1) * r, :])
    return init


def _adaln_kernel(c_ref, w_ref, b_ref, o_ref):
    c = c_ref[...]
    a = c * _sigmoid(c)
    a1, a2, a3 = _split3(a)
    w1, w2, w3 = _split3(w_ref[...])
    acc = _dot(a1, w1) + (_dot(a1, w2) + _dot(a2, w1)) + (_dot(a2, w2) + _dot(a1, w3) + _dot(a3, w1))
    o_ref[...] = acc + b_ref[...]


def _adaln(c_pad, w, b):
    rows, d = c_pad.shape
    n = w.shape[1]
    tn = 1536
    return pl.pallas_call(
        _adaln_kernel,
        grid=(n // tn,),
        in_specs=[
            pl.BlockSpec((rows, d), lambda j: (0, 0)),
            pl.BlockSpec((d, tn), lambda j: (0, j)),
            pl.BlockSpec((1, tn), lambda j: (0, j)),
        ],
        out_specs=pl.BlockSpec((rows, tn), lambda j: (0, j)),
        out_shape=jax.ShapeDtypeStruct((rows, n), F32),
        compiler_params=pltpu.CompilerParams(vmem_limit_bytes=VMEM_LIMIT_BYTES),
        name="adaln",
    )(c_pad, w, b)


def _inproj_kernel(x_ref, g_ref, sc_ref, sh_ref, w_ref, o_ref):
    x = x_ref[...]
    ms = jnp.mean(x * x, axis=-1, keepdims=True)
    h = x * lax.rsqrt(ms + EPS) * g_ref[...]
    h = h * (1.0 + sc_ref[0]) + sh_ref[0]
    o_ref[...] = _dot(h.astype(BF16), w_ref[...])


def _inproj(x2, g, scale, shift, w, seq, tm):
    m, d = x2.shape
    n = w.shape[1]
    tpb = seq // tm
    return pl.pallas_call(
        _inproj_kernel,
        grid=(m // tm,),
        in_specs=[
            pl.BlockSpec((tm, d), lambda i: (i, 0)),
            pl.BlockSpec((1, d), lambda i: (0, 0)),
            pl.BlockSpec((1, 1, d), lambda i: (i // tpb, 0, 0)),
            pl.BlockSpec((1, 1, d), lambda i: (i // tpb, 0, 0)),
            pl.BlockSpec((d, n), lambda i: (0, 0), pipeline_mode=pl.Buffered(1)),
        ],
        out_specs=pl.BlockSpec((tm, n), lambda i: (i, 0)),
        out_shape=jax.ShapeDtypeStruct((m, n), F32),
        compiler_params=pltpu.CompilerParams(vmem_limit_bytes=VMEM_LIMIT_BYTES),
        name="inproj",
    )(x2, g, scale, shift, w)


def _qkprep_kernel(ak_ref, av_ref, ql_ref, kw_ref, gql_ref, wuq_ref, gq_ref, gk_ref, gki_ref,
                   qt_out, ak_out, qit_out, ki_out, wt_out, vt_out):
    tm = ak_ref.shape[0]
    ones64 = _group_ones(A_WIDTH, A_HEAD_DIM)
    ql = ql_ref[...]
    ql = ql * lax.rsqrt(jnp.mean(ql * ql, axis=-1, keepdims=True) + EPS) * gql_ref[...]
    up = _dot(ql.astype(BF16), wuq_ref[...])
    aq = up[:, :A_WIDTH]
    aq = aq * lax.rsqrt(_group_mean_sq(aq, ones64, A_HEAD_DIM) + EPS) * gq_ref[...]
    aq_t = (aq * (A_HEAD_DIM ** -0.5 * LOG2E)).T
    zero_half = jnp.zeros((A_HEAD_DIM, tm), BF16)
    for h in range(A_HEADS):
        e = h % 2
        qt_out[0, h, e * A_HEAD_DIM:(e + 1) * A_HEAD_DIM, :] = aq_t[h * A_HEAD_DIM:(h + 1) * A_HEAD_DIM].astype(BF16)
        qt_out[0, h, (1 - e) * A_HEAD_DIM:(2 - e) * A_HEAD_DIM, :] = zero_half
    qit_out[0] = up[:, A_WIDTH:].T.reshape(IDX_HEADS, IDX_DIM, tm).astype(BF16)
    ak = ak_ref[...]
    ak = ak * lax.rsqrt(_group_mean_sq(ak, ones64, A_HEAD_DIM) + EPS) * gk_ref[...]
    ak_out[...] = ak.astype(BF16)
    vt_out[0, 0] = av_ref[...].T.reshape(A_HEADS, A_HEAD_DIM, tm).astype(BF16)
    kw = kw_ref[...]
    lane = lax.broadcasted_iota(I32, kw.shape, 1)
    ksq = jnp.where(lane < IDX_DIM, kw * kw, 0.0)
    kms = jnp.sum(ksq, axis=-1, keepdims=True) * (1.0 / IDX_DIM)
    kn = kw * lax.rsqrt(kms + EPS) * gki_ref[...]
    ki_out[...] = kn[:, :IDX_DIM].astype(BF16)
    idx_scale = (IDX_DIM ** -0.5) * (IDX_HEADS ** -0.5)
    w_rows = jnp.where((lane >= IDX_DIM) & (lane < IDX_DIM + IDX_HEADS), kw * idx_scale, 0.0).T
    wt_out[0] = w_rows[IDX_DIM:IDX_DIM + 8]


def _qkprep(proj, gql, wuq, gq, gk, gki, bsz, seq, tm):
    m = proj.shape[0]
    tpb = seq // tm
    row = lambda i: (i, 0)
    const = lambda i: (0, 0)
    return pl.pallas_call(
        _qkprep_kernel,
        grid=(m // tm,),
        in_specs=[
            pl.BlockSpec((tm, A_WIDTH), lambda i: (i, COL_AK // A_WIDTH)),
            pl.BlockSpec((tm, A_WIDTH), lambda i: (i, COL_AV // A_WIDTH)),
            pl.BlockSpec((tm, Q_LORA_RANK), lambda i: (i, COL_QL // Q_LORA_RANK)),
            pl.BlockSpec((tm, LANES), lambda i: (i, COL_KW // LANES)),
            pl.BlockSpec((1, Q_LORA_RANK), const),
            pl.BlockSpec(wuq.shape, const),
            pl.BlockSpec((1, A_WIDTH), const),
            pl.BlockSpec((1, A_WIDTH), const),
            pl.BlockSpec((1, LANES), const),
        ],
        out_specs=[
            pl.BlockSpec((1, A_HEADS, LANES, tm), lambda i: (i // tpb, 0, 0, i % tpb)),
            pl.BlockSpec((tm, A_WIDTH), row),
            pl.BlockSpec((1, IDX_HEADS, IDX_DIM, tm), lambda i: (i // tpb, 0, 0, i % tpb)),
            pl.BlockSpec((tm, IDX_DIM), row),
            pl.BlockSpec((1, 8, tm), lambda i: (i // tpb, 0, i % tpb)),
            pl.BlockSpec((1, 1, A_HEADS, A_HEAD_DIM, tm), lambda i: (i // tpb, i % tpb, 0, 0, 0)),
        ],
        out_shape=[
            jax.ShapeDtypeStruct((bsz, A_HEADS, LANES, seq), BF16),
            jax.ShapeDtypeStruct((m, A_WIDTH), BF16),
            jax.ShapeDtypeStruct((bsz, IDX_HEADS, IDX_DIM, seq), BF16),
            jax.ShapeDtypeStruct((m, IDX_DIM), BF16),
            jax.ShapeDtypeStruct((bsz, 8, seq), F32),
            jax.ShapeDtypeStruct((bsz, tpb, A_HEADS, A_HEAD_DIM, tm), BF16),
        ],
        compiler_params=pltpu.CompilerParams(vmem_limit_bytes=VMEM_LIMIT_BYTES),
        name="qkprep",
    )(proj, proj, proj, proj, gql, wuq, gq, gk, gki)


def _dsa_kernel(qit_ref, wt_ref, ki_ref, qt_ref, k_ref, vt_ref, tril_ref, o_ref,
                sc_ref, acc_ref, p_ref, kn_ref, *, tq, ck, topk, nbis):
    i = pl.program_id(1)
    n_chunks = ((i + 1) * tq + ck - 1) // ck
    kf = float(topk)
    neg_inf = float("-inf")
    pos_inf = float("inf")
    ar = SELECT_ACC_ROWS
    full_acc = lambda val: jnp.full((ar, tq), val, F32)
    row_min = lambda x: jnp.min(x, axis=0, keepdims=True)
    row_max = lambda x: jnp.max(x, axis=0, keepdims=True)
    row_sum = lambda x: jnp.sum(x, axis=0, keepdims=True)
    chunk_rows = lambda jc: pl.ds(pl.multiple_of(jc * ck, ck), ck)

    @pl.when(i == 0)
    def _():
        head_of_lane = lax.broadcasted_iota(I32, (A_WIDTH, LANES), 0) >> _log2(A_HEAD_DIM)
        pick = jnp.where(head_of_lane == lax.broadcasted_iota(I32, (A_WIDTH, LANES), 1), 1.0, 0.0).astype(BF16)

        def kn_body(jc, mx):
            k = k_ref[0, chunk_rows(jc), :].astype(F32)
            return _fold_rows(_dot((k * k).astype(BF16), pick), jnp.maximum, mx)

        mx = lax.fori_loop(0, k_ref.shape[1] // ck, kn_body, jnp.zeros((8, LANES), F32))
        kn_ref[...] = jnp.broadcast_to(row_max(mx), (8, LANES))

    t_pos = i * tq + lax.broadcasted_iota(I32, (ck, tq), 1)
    s_iota = lax.broadcasted_iota(I32, (ck, tq), 0)

    def score_body(masked, jc, carry):
        mn, mx = carry
        kc = ki_ref[0, chunk_rows(jc), :]
        score = jnp.zeros((ck, tq), F32)
        for h in range(IDX_HEADS):
            score = score + wt_ref[0, h:h + 1, :] * jnp.maximum(_dot(kc, qit_ref[0, h]), 0.0)
        if masked:
            causal = (s_iota + jc * ck) <= t_pos
            sc_ref[jc] = jnp.where(causal, score, neg_inf)
            mn = _fold_rows(jnp.where(causal, score, pos_inf), jnp.minimum, mn)
            mx = _fold_rows(jnp.where(causal, score, neg_inf), jnp.maximum, mx)
        else:
            sc_ref[jc] = score
            mn = _fold_rows(score, jnp.minimum, mn)
            mx = _fold_rows(score, jnp.maximum, mx)
        return mn, mx

    n_full = (i * tq + 1) // ck
    carry = lax.fori_loop(0, n_full, functools.partial(score_body, False), (full_acc(pos_inf), full_acc(neg_inf)))
    mn_acc, mx_acc = lax.fori_loop(n_full, n_chunks, functools.partial(score_body, True), carry)
    rmin, rmax = row_min(mn_acc), row_max(mx_acc)

    def count_ge(cand):
        cand_b = jnp.broadcast_to(cand, (ar, tq))

        def body(jc, acc):
            return _fold_rows(sc_ref[jc], lambda a, x: a + jnp.where(x >= cand_b, 1.0, 0.0), acc)

        return row_sum(lax.fori_loop(0, n_chunks, body, full_acc(0.0)))

    def snap(lo, hi):
        lo_b = jnp.broadcast_to(lo, (ar, tq))
        hi_b = jnp.broadcast_to(hi, (ar, tq))

        def body(jc, carry):
            x = sc_ref[jc]
            a = _fold_rows(x, lambda a, x: jnp.minimum(a, jnp.where(x >= lo_b, x, pos_inf)), carry[0])
            b = _fold_rows(x, lambda b, x: jnp.maximum(b, jnp.where(x < hi_b, x, neg_inf)), carry[1])
            return a, b

        a, b = lax.fori_loop(0, n_chunks, body, (full_acc(pos_inf), full_acc(neg_inf)))
        return row_min(a), row_max(b)

    def probe(cand):
        a, b = snap(cand, cand)
        return count_ge(cand), a, b

    n_valid = (i * tq + 1 + lax.broadcasted_iota(I32, (1, tq), 1)).astype(F32)
    small = n_valid <= kf

    def interpolate(lo, hi, c_lo, c_hi, w_lo, w_hi):
        f_lo = (c_lo - kf + 0.5) * w_lo
        f_hi = (kf - c_hi - 0.5) * w_hi
        return lo + (hi - lo) * jnp.clip(f_lo / jnp.maximum(f_lo + f_hi, 0.5), 0.1, 0.9)

    def illinois(up, dn, w_lo, w_hi, last):
        w_hi = jnp.where(up, jnp.where(last > 0.0, w_hi * 0.5, 1.0), jnp.where(dn, 1.0, w_hi))
        w_lo = jnp.where(dn, jnp.where(last < 0.0, w_lo * 0.5, 1.0), jnp.where(up, 1.0, w_lo))
        return w_lo, w_hi, jnp.where(up, 1.0, jnp.where(dn, -1.0, last))

    def positive_stats():
        def body(jc, carry):
            c_pos, min_pos = carry
            blk = sc_ref[jc]
            for s in range(ck // ar):
                x = blk[s * ar:(s + 1) * ar, :]
                pos = x > 0.0
                c_pos = c_pos + jnp.where(pos, 1.0, 0.0)
                min_pos = jnp.minimum(min_pos, jnp.where(pos, x, pos_inf))
            return c_pos, min_pos

        c_pos, min_pos = lax.fori_loop(0, n_chunks, body, (full_acc(0.0), full_acc(pos_inf)))
        return row_sum(c_pos), row_min(min_pos)

    c_ge0 = count_ge(jnp.zeros((1, tq), F32))
    c_pos, min_pos = positive_stats()
    below_zero = c_ge0 < kf
    above_zero = c_pos >= kf
    at_zero = jnp.logical_not(below_zero | above_zero)
    hi0 = rmax + jnp.maximum(jnp.abs(rmax) * 1e-6, 1e-30)
    lo = jnp.where(above_zero, min_pos, jnp.where(at_zero, 0.0, rmin))
    c_lo = jnp.where(above_zero, c_pos, jnp.where(at_zero, c_ge0, n_valid))
    hi = jnp.where(above_zero, hi0, 0.0)
    c_hi = jnp.where(above_zero, 0.0, jnp.where(at_zero, c_pos, c_ge0))

    def search_body(_, st):
        lo, hi, c_lo, c_hi, w_lo, w_hi, last = st
        mid = interpolate(lo, hi, c_lo, c_hi, w_lo, w_hi)
        ok = (mid > lo) & (mid < hi)
        c = count_ge(mid)
        up = ok & (c >= kf)
        dn = ok & (c < kf)
        w_lo, w_hi, last = illinois(up, dn, w_lo, w_hi, last)
        return (jnp.where(up, mid, lo), jnp.where(dn, mid, hi), jnp.where(up, c, c_lo), jnp.where(dn, c, c_hi),
                w_lo, w_hi, last)

    zeros_q, ones_q = jnp.zeros((1, tq), F32), jnp.ones((1, tq), F32)
    lo, hi, c_lo, c_hi, _, _, _ = lax.fori_loop(0, nbis, search_body,
                                                (lo, hi, c_lo, c_hi, ones_q, ones_q, zeros_q))

    lo, hi = snap(lo, hi)
    lo, hi = jnp.where(at_zero, 0.0, lo), jnp.where(at_zero, 0.0, hi)

    def active_of(lo, hi, c_lo):
        return jnp.logical_not(small) & (lo < hi) & (c_lo != kf)

    def snap_body(st):
        lo, hi, c_lo, c_hi, w_lo, w_hi, last, _ = st
        act = active_of(lo, hi, c_lo)
        mid = interpolate(lo, hi, c_lo, c_hi, w_lo, w_hi)
        mid = jnp.where((c_hi == kf - 1.0) | (mid <= lo) | (mid > hi), hi, mid)
        c, a, b = probe(mid)
        up = act & (c >= kf)
        dn = act & (c < kf)
        w_lo, w_hi, last = illinois(up, dn, w_lo, w_hi, last)
        lo, c_lo = jnp.where(up, a, lo), jnp.where(up, c, c_lo)
        hi, c_hi = jnp.where(dn, b, hi), jnp.where(dn, c, c_hi)
        return lo, hi, c_lo, c_hi, w_lo, w_hi, last, jnp.max(jnp.where(active_of(lo, hi, c_lo), 1.0, 0.0))

    flag0 = jnp.max(jnp.where(active_of(lo, hi, c_lo), 1.0, 0.0))
    lo, hi, c_lo, c_hi = lax.while_loop(lambda st: st[7] > 0.0, snap_body,
                                        (lo, hi, c_lo, c_hi, ones_q, ones_q, zeros_q, flag0))[:4]

    tau = jnp.where(small, rmin, lo)
    excess = jnp.logical_not(small) & (lo == hi) & (c_lo > kf)
    need = jnp.where(excess, kf - c_hi, 4.0 * 65536.0 * 65536.0)

    @pl.when(jnp.max(jnp.where(excess, 1.0, 0.0)) > 0.0)
    def _():
        def tie_body(jc, carry):
            blk = sc_ref[jc]
            eq = blk == tau
            pc = _dot(tril_ref[...], jnp.where(eq, 1.0, 0.0).astype(BF16)) + carry
            sc_ref[jc] = jnp.where(eq & (pc > need), neg_inf, blk)
            return pc[ck - 1:ck, :]

        lax.fori_loop(0, n_chunks, tie_body, jnp.zeros((1, tq), F32))

    def logits_t(jc, h):
        kc = k_ref[0, chunk_rows(jc), (h // 2) * LANES:(h // 2 + 1) * LANES]
        return _dot(kc, qt_ref[0, h])

    m_bound = []
    bmax = jnp.zeros((1, 1), F32)
    for h in range(A_HEADS):
        qf = qt_ref[0, h].astype(F32)
        bound = jnp.sqrt(row_sum(qf * qf) * kn_ref[0:1, h:h + 1])
        m_bound.append(bound)
        bmax = jnp.maximum(bmax, jnp.max(bound, axis=1, keepdims=True))

    def exact_max(_):
        def max_body(jc, mx):
            sel = sc_ref[jc] >= tau
            return tuple(_fold_rows(jnp.where(sel, logits_t(jc, h), neg_inf), jnp.maximum, mx[h])
                         for h in range(A_HEADS))

        mx = lax.fori_loop(0, n_chunks, max_body, tuple(jnp.full((8, tq), neg_inf, F32) for _ in range(A_HEADS)))
        return tuple(row_max(m) for m in mx)

    m_ref_vals = lax.cond(bmax[0, 0] > LOGIT_BOUND_LIMIT, exact_max, lambda _: tuple(m_bound), 0)

    ones_rows = jnp.ones((16, ck), BF16)
    acc_ref[...] = jnp.zeros(acc_ref.shape, F32)

    def pv_stage(jc, h):
        lhs = jnp.concatenate([vt_ref[0, jc, h], ones_rows], axis=0)
        acc_ref[h] += _dot(lhs, p_ref[h])

    def qk_stage(jc, sel, h):
        p_ref[h] = jnp.where(sel, jnp.exp2(logits_t(jc, h) - m_ref_vals[h]), 0.0).astype(BF16)

    sel0 = sc_ref[0] >= tau
    for h in range(A_HEADS):
        qk_stage(0, sel0, h)

    def att_body(jc, carry):
        sel = sc_ref[jc] >= tau
        for h in range(A_HEADS):
            pv_stage(jc - 1, h)
            qk_stage(jc, sel, h)
        return carry

    lax.fori_loop(1, n_chunks, att_body, 0)
    for h in range(A_HEADS):
        pv_stage(n_chunks - 1, h)
    for h in range(A_HEADS):
        a = acc_ref[h]
        o_ref[0, h] = (a[:A_HEAD_DIM] * (1.0 / a[A_HEAD_DIM:A_HEAD_DIM + 1])).astype(o_ref.dtype)


def _dsa(qit, wt, ki, qt, k, vt, tril, tq, ck, topk):
    bsz, hi, di, seq = qit.shape
    nc = seq // ck
    kern = functools.partial(_dsa_kernel, tq=tq, ck=ck, topk=topk, nbis=BISECT_STEPS)
    return pl.pallas_call(
        kern,
        grid=(bsz, seq // tq),
        in_specs=[
            pl.BlockSpec((1, hi, di, tq), lambda b, i: (b, 0, 0, i)),
            pl.BlockSpec((1, 8, tq), lambda b, i: (b, 0, i)),
            pl.BlockSpec((1, seq, di), lambda b, i: (b, 0, 0)),
            pl.BlockSpec((1, A_HEADS, LANES, tq), lambda b, i: (b, 0, 0, i)),
            pl.BlockSpec((1, seq, A_WIDTH), lambda b, i: (b, 0, 0)),
            pl.BlockSpec((1, nc, A_HEADS, A_HEAD_DIM, ck), lambda b, i: (b, 0, 0, 0, 0)),
            pl.BlockSpec((ck, ck), lambda b, i: (0, 0)),
        ],
        out_specs=pl.BlockSpec((1, A_HEADS, A_HEAD_DIM, tq), lambda b, i: (b, 0, 0, i)),
        out_shape=jax.ShapeDtypeStruct((bsz, A_HEADS, A_HEAD_DIM, seq), BF16),
        scratch_shapes=[
            pltpu.VMEM((nc, ck, tq), F32),
            pltpu.VMEM((A_HEADS, A_HEAD_DIM + 16, tq), F32),
            pltpu.VMEM((A_HEADS, ck, tq), BF16),
            pltpu.VMEM((8, LANES), F32),
        ],
        compiler_params=pltpu.CompilerParams(
            dimension_semantics=("arbitrary", "arbitrary"), vmem_limit_bytes=VMEM_LIMIT_BYTES),
        name="dsa",
    )(qit, wt, ki, qt, k, vt, tril)


def _hgrn_kernel(bq_ref, bf_ref, bi_ref, bg_ref, lb_ref, go_ref, o_ref, st_ref, oi_ref, *, tl, layer):
    cs = HGRN_CHUNK
    nch = tl // cs

    @pl.when(pl.program_id(1) == 0)
    def _():
        st_ref[...] = jnp.zeros(st_ref.shape, F32)

    lbr = lb_ref[...]
    slots = [lbr[k:k + 1] for k in range(lbr.shape[0])]
    mx = functools.reduce(jnp.maximum, slots)
    es = [jnp.exp(s - mx) for s in slots]
    lb = functools.reduce(jnp.add, es[:layer + 1]) / functools.reduce(jnp.add, es)

    bq = bq_ref[...]
    q = bq * _sigmoid(bq)
    f = lb + (1.0 - lb) * _sigmoid(bf_ref[...])
    kk = 1.0 - f
    g = jnp.log(f)
    v = bi_ref[...]

    r_i = lax.broadcasted_iota(I32, (tl, tl), 0)
    c_i = lax.broadcasted_iota(I32, (tl, tl), 1)
    same = (r_i >> _log2(cs)) == (c_i >> _log2(cs))
    tri = jnp.where(same & (c_i <= r_i), 1.0, 0.0).astype(BF16)
    blk = jnp.where(same, 1.0, 0.0).astype(BF16)
    g1, g2, g3 = _split3(g)
    b = _dot(tri, g1) + _dot(tri, g2) + _dot(tri, g3)
    bend = _dot(blk, g1) + _dot(blk, g2) + _dot(blk, g3)

    bk = b - jnp.log(kk)
    pos = lax.broadcasted_iota(I32, (tl, B_WIDTH), 0) & (cs - 1)
    o = jnp.zeros((tl, B_WIDTH), F32)
    for r in range(cs):
        if r == 0:
            bk_s, v_s = bk, v
        else:
            bk_s = pltpu.roll(bk, r, axis=0)
            v_s = pltpu.roll(v, r, axis=0)
        e = q * jnp.exp(jnp.where(pos >= r, b - bk_s, -jnp.inf))
        parts = []
        for h in range(B_HEADS):
            sl = slice(h * B_HEAD_DIM, (h + 1) * B_HEAD_DIM)
            parts.append(jnp.sum(e[:, sl], axis=-1, keepdims=True) * v_s[:, sl])
        o = o + jnp.concatenate(parts, axis=1)

    qe = (q * jnp.exp(b)).astype(BF16)
    kd = (kk * jnp.exp(bend - b)).astype(BF16)
    vb = v.astype(BF16)
    dec = jnp.exp(bend)
    for c in range(nch):
        rs = slice(c * cs, (c + 1) * cs)
        for h in range(B_HEADS):
            sl = slice(h * B_HEAD_DIM, (h + 1) * B_HEAD_DIM)
            st = st_ref[h]
            oi_ref[rs, sl] = lax.dot_general(qe[rs, sl], st.astype(BF16), (((1,), (1,)), ((), ())),
                                             preferred_element_type=F32)
            upd = lax.dot_general(vb[rs, sl], kd[rs, sl], (((0,), (0,)), ((), ())),
                                  preferred_element_type=F32)
            st_ref[h] = st * dec[c * cs:c * cs + 1, sl] + upd
    o = o + oi_ref[...]

    parts = []
    for h in range(B_HEADS):
        sl = slice(h * B_HEAD_DIM, (h + 1) * B_HEAD_DIM)
        oh = o[:, sl]
        parts.append(oh * lax.rsqrt(jnp.mean(oh * oh, axis=-1, keepdims=True) + EPS))
    on = jnp.concatenate(parts, axis=1) * go_ref[...]
    bg = bg_ref[...]
    o_ref[...] = (on * (bg * _sigmoid(bg))).astype(o_ref.dtype)


def _hgrn(proj, lb_table, go, bsz, seq, tl, layer):
    m = proj.shape[0]
    tpb = seq // tl
    col = lambda c: (lambda b, t: (b * tpb + t, c // B_WIDTH))
    return pl.pallas_call(
        functools.partial(_hgrn_kernel, tl=tl, layer=layer),
        grid=(bsz, tpb),
        in_specs=[
            pl.BlockSpec((tl, B_WIDTH), col(COL_BQ)),
            pl.BlockSpec((tl, B_WIDTH), col(COL_BF)),
            pl.BlockSpec((tl, B_WIDTH), col(COL_BI)),
            pl.BlockSpec((tl, B_WIDTH), col(COL_BG)),
            pl.BlockSpec(lb_table.shape, lambda b, t: (0, 0)),
            pl.BlockSpec((1, B_WIDTH), lambda b, t: (0, 0)),
        ],
        out_specs=pl.BlockSpec((tl, B_WIDTH), lambda b, t: (b * tpb + t, 0)),
        out_shape=jax.ShapeDtypeStruct((m, B_WIDTH), BF16),
        scratch_shapes=[
            pltpu.VMEM((B_HEADS, B_HEAD_DIM, B_HEAD_DIM), F32),
            pltpu.VMEM((tl, B_WIDTH), F32),
        ],
        compiler_params=pltpu.CompilerParams(
            dimension_semantics=("arbitrary", "arbitrary"), vmem_limit_bytes=VMEM_LIMIT_BYTES),
        name="hgrn",
    )(proj, proj, proj, proj, lb_table, go)


def _tail_kernel(oat_ref, ob_ref, ga_ref, gb_ref, x_ref, g1_ref, wa_ref, wb_ref, wo_ref,
                 n2_ref, sc_ref, sh_ref, g2_ref, w1_ref, b1_ref, w2_ref, b2_ref, o_ref, *, tf):
    tm = x_ref.shape[0]
    oa = oat_ref[0].reshape(A_WIDTH, tm).astype(F32).T.astype(BF16)
    pa = _dot(oa, wa_ref[...])
    pb = _dot(ob_ref[...], wb_ref[...])
    merged = _sigmoid(ga_ref[...]) * pa + _sigmoid(gb_ref[...]) * pb
    x = x_ref[...] + g1_ref[0] * _dot(merged.astype(BF16), wo_ref[...])
    ms = jnp.mean(x * x, axis=-1, keepdims=True)
    h = x * lax.rsqrt(ms + EPS) * n2_ref[...]
    h = (h * (1.0 + sc_ref[0]) + sh_ref[0]).astype(BF16)
    dff = w1_ref.shape[1]
    y = jnp.zeros(x.shape, F32)
    for c in range(dff // tf):
        cs = slice(c * tf, (c + 1) * tf)
        a = jnp.maximum(_dot(h, w1_ref[:, cs]) + b1_ref[:, cs], 0.0)
        y = y + _dot((a * a).astype(BF16), w2_ref[cs, :])
    o_ref[...] = x + g2_ref[0] * (y + b2_ref[...])


def _tail(oat, ob, proj, x2, gate1, wa, wb, wo, n2, scale2, shift2, gate2, w1, b1, w2, b2, seq, tm, tf):
    m, d = x2.shape
    tpb = seq // tm
    row = lambda i: (i, 0)
    bat = lambda i: (i // tpb, 0, 0)
    resident = lambda a: pl.BlockSpec(a.shape, lambda i: (0, 0), pipeline_mode=pl.Buffered(1))
    return pl.pallas_call(
        functools.partial(_tail_kernel, tf=tf),
        grid=(m // tm,),
        in_specs=[
            pl.BlockSpec((1, A_HEADS, A_HEAD_DIM, tm), lambda i: (i // tpb, 0, 0, i % tpb)),
            pl.BlockSpec((tm, B_WIDTH), row),
            pl.BlockSpec((tm, d), lambda i: (i, COL_GA // d)),
            pl.BlockSpec((tm, d), lambda i: (i, COL_GB // d)),
            pl.BlockSpec((tm, d), row),
            pl.BlockSpec((1, 1, d), bat),
            resident(wa), resident(wb), resident(wo),
            resident(n2),
            pl.BlockSpec((1, 1, d), bat),
            pl.BlockSpec((1, 1, d), bat),
            pl.BlockSpec((1, 1, d), bat),
            resident(w1), resident(b1), resident(w2), resident(b2),
        ],
        out_specs=pl.BlockSpec((tm, d), row),
        out_shape=jax.ShapeDtypeStruct((m, d), F32),
        compiler_params=pltpu.CompilerParams(vmem_limit_bytes=VMEM_LIMIT_BYTES),
        name="tail",
    )(oat, ob, proj, proj, x2, gate1, wa, wb, wo, n2, scale2, shift2, gate2, w1, b1, w2, b2)


def _regroup_kernel(w_ref, o_ref, *, d):
    kv0 = Q_LORA_RANK
    kw0 = kv0 + 2 * A_WIDTH
    rest0 = kw0 + IDX_DIM + IDX_HEADS
    rest = 4 * B_WIDTH + 2 * d
    w = w_ref[0]
    o_ref[:, COL_AK:COL_AK + 2 * A_WIDTH] = w[:, kv0:kw0].astype(BF16)
    o_ref[:, COL_BQ:COL_BQ + rest] = w[:, rest0:rest0 + rest].astype(BF16)
    o_ref[:, COL_QL:COL_QL + Q_LORA_RANK] = w[:, :Q_LORA_RANK].astype(BF16)
    kw = w[:, kw0:kw0 + LANES]
    lane = lax.broadcasted_iota(I32, kw.shape, 1)
    o_ref[:, COL_KW:COL_KW + LANES] = jnp.where(lane < IDX_DIM + IDX_HEADS, kw, 0.0).astype(BF16)


def _regroup_w_in(w_all, layer):
    _, d, n = w_all.shape
    tr = 128
    assert COL_BQ + 4 * B_WIDTH + 2 * d == COL_QL and COL_QL + Q_LORA_RANK == COL_KW
    return pl.pallas_call(
        functools.partial(_regroup_kernel, d=d),
        grid=(d // tr,),
        in_specs=[pl.BlockSpec((1, tr, n), lambda i: (layer, i, 0))],
        out_specs=pl.BlockSpec((tr, PROJ_COLS), lambda i: (i, 0)),
        out_shape=jax.ShapeDtypeStruct((d, PROJ_COLS), BF16),
        compiler_params=pltpu.CompilerParams(vmem_limit_bytes=VMEM_LIMIT_BYTES),
        name="regroup",
    )(w_all)


def _layer(x, mod, l, p, tiles):
    bsz, seq, d = x.shape
    m = bsz * seq
    shift1, scale1, gate1, shift2, scale2, gate2 = (mod[:, k * d:(k + 1) * d].reshape(bsz, 1, d) for k in range(6))
    x2 = x.reshape(m, d)

    proj = _inproj(x2, p['norm1_g'][l][None], scale1, shift1, _regroup_w_in(p['w_in'], l), seq, tiles['tm_in'])

    tq, ck = tiles['tq'], tiles['ck']
    wuq = jnp.concatenate([p['w_uq'][l], p['w_uq_idx'][l]], axis=1).astype(BF16)
    gki = jnp.concatenate([p['k_idx_norm_g'][l], jnp.zeros((LANES - IDX_DIM,), F32)])[None]
    qt, ak, qit, ki, wt, vt = _qkprep(
        proj, p['q_lat_norm_g'][l][None], wuq,
        jnp.tile(p['q_norm_g'][l], A_HEADS)[None], jnp.tile(p['k_norm_g'][l], A_HEADS)[None], gki, bsz, seq, ck)

    topk = min(TOPK_MAX, seq // 4)
    tril = jnp.tril(jnp.ones((ck, ck), BF16))
    out_at = _dsa(qit, wt, ki.reshape(bsz, seq, IDX_DIM), qt, ak.reshape(bsz, seq, A_WIDTH), vt, tril, tq, ck, topk)

    out_b = _hgrn(proj, p['hgrn_lb'], p['hgrn_o_norm_g'][l][None], bsz, seq, tiles['tl'], l)

    out = _tail(out_at, out_b, proj, x2, gate1, p['w_proj_a'][l].astype(BF16), p['w_proj_b'][l].astype(BF16),
                p['w_out'][l].astype(BF16), p['norm2_g'][l][None], scale2, shift2, gate2,
                p['w_mlp1'][l].astype(BF16), p['b_mlp1'][l][None], p['w_mlp2'][l].astype(BF16), p['b_mlp2'][l][None],
                seq, tiles['tm'], tiles['tf'])
    return out.reshape(bsz, seq, d)


def _tiles(seq):
    pick = lambda want: min(want, seq)
    return dict(tm_in=pick(512), tm=pick(512), tq=pick(256), ck=pick(512), tl=pick(256), tf=1024)


def kernel(x, c, w_ada, b_ada, norm1_g, w_in, q_lat_norm_g, w_uq, w_uq_idx, q_norm_g, k_norm_g, k_idx_norm_g,
           hgrn_lb, hgrn_o_norm_g, w_proj_a, w_proj_b, w_out, norm2_g, w_mlp1, b_mlp1, w_mlp2, b_mlp2):
    p = dict(norm1_g=norm1_g, w_in=w_in, q_lat_norm_g=q_lat_norm_g, w_uq=w_uq, w_uq_idx=w_uq_idx,
             q_norm_g=q_norm_g, k_norm_g=k_norm_g, k_idx_norm_g=k_idx_norm_g, hgrn_lb=hgrn_lb,
             hgrn_o_norm_g=hgrn_o_norm_g, w_proj_a=w_proj_a, w_proj_b=w_proj_b, w_out=w_out, norm2_g=norm2_g,
             w_mlp1=w_mlp1, b_mlp1=b_mlp1, w_mlp2=w_mlp2, b_mlp2=b_mlp2)
    bsz, seq, d = x.shape
    depth = w_ada.shape[0]
    tiles = _tiles(seq)
    c_pad = jnp.zeros((8, d), F32).at[:bsz].set(c)
    for l in range(depth):
        mod = _adaln(c_pad, w_ada[l], b_ada[l][None])[:bsz]
        x = _layer(x, mod, l, p, tiles)
    return x
```

```python
import functools

import jax
import jax.numpy as jnp
from jax import lax
from jax.experimental import pallas as pl
from jax.experimental.pallas import tpu as pltpu

F32 = jnp.float32
BF16 = jnp.bfloat16
I32 = jnp.int32

EPS = 1e-6
A_HEADS = 8
A_HEAD_DIM = 64
A_WIDTH = A_HEADS * A_HEAD_DIM
Q_LORA_RANK = 256
IDX_HEADS = 4
IDX_DIM = 64
TOPK_MAX = 256
B_WIDTH = 512
B_HEADS = 4
B_HEAD_DIM = 128
HGRN_CHUNK = 16

LANES = 128
VMEM_LIMIT_BYTES = 56 * 1024 * 1024

COL_AK, COL_AV, COL_BQ, COL_BF, COL_BI, COL_BG = 0, 512, 1024, 1536, 2048, 2560
COL_GA, COL_GB, COL_QL, COL_KW = 3072, 4096, 5120, 5376
PROJ_COLS = 5504

LOG2E = 1.4426950408889634
BISECT_STEPS = 12
SELECT_ACC_ROWS = 32
LOGIT_BOUND_LIMIT = 60.0


def _sigmoid(x):
    return 0.5 * jnp.tanh(0.5 * x) + 0.5


def _split2(x):
    hi = x.astype(BF16)
    lo = (x - hi.astype(F32)).astype(BF16)
    return hi, lo


def _split3(x):
    a = x.astype(BF16)
    r = x - a.astype(F32)
    b = r.astype(BF16)
    c = (r - b.astype(F32)).astype(BF16)
    return a, b, c


def _dot(a, b):
    return jnp.dot(a, b, preferred_element_type=F32)


def _log2(n):
    assert n > 0 and n & (n - 1) == 0, n
    return n.bit_length() - 1


def _group_ones(n, group):
    r = lax.broadcasted_iota(I32, (n, n), 0) >> _log2(group)
    c = lax.broadcasted_iota(I32, (n, n), 1) >> _log2(group)
    return jnp.where(r == c, 1.0, 0.0).astype(BF16)


def _group_mean_sq(x, ones_bd, group):
    hi, lo = _split2(x * x)
    return (_dot(hi, ones_bd) + _dot(lo, ones_bd)) * (1.0 / group)


def _fold_rows(x, op, init):
    r = init.shape[0]
    for s in range(x.shape[0] // r):
        init = op(init, x[s * r:(s + 1) * r, :])
    return init


def _adaln_kernel(c_ref, w_ref, b_ref, o_ref):
    c = c_ref[...]
    a = c * _sigmoid(c)
    a1, a2, a3 = _split3(a)
    w1, w2, w3 = _split3(w_ref[...])
    acc = _dot(a1, w1) + (_dot(a1, w2) + _dot(a2, w1)) + (_dot(a2, w2) + _dot(a1, w3) + _dot(a3, w1))
    o_ref[...] = acc + b_ref[...]


def _adaln(c_pad, w, b):
    rows, d = c_pad.shape
    n = w.shape[1]
    tn = 1536
    return pl.pallas_call(
        _adaln_kernel,
        grid=(n // tn,),
        in_specs=[
            pl.BlockSpec((rows, d), lambda j: (0, 0)),
            pl.BlockSpec((d, tn), lambda j: (0, j)),
            pl.BlockSpec((1, tn), lambda j: (0, j)),
        ],
        out_specs=pl.BlockSpec((rows, tn), lambda j: (0, j)),
        out_shape=jax.ShapeDtypeStruct((rows, n), F32),
        compiler_params=pltpu.CompilerParams(vmem_limit_bytes=VMEM_LIMIT_BYTES),
        name="adaln",
    )(c_pad, w, b)


def _inproj_kernel(x_ref, g_ref, sc_ref, sh_ref, w_ref, o_ref):
    x = x_ref[...]
    ms = jnp.mean(x * x, axis=-1, keepdims=True)
    h = x * lax.rsqrt(ms + EPS) * g_ref[...]
    h = h * (1.0 + sc_ref[0]) + sh_ref[0]
    o_ref[...] = _dot(h.astype(BF16), w_ref[...])


def _inproj(x2, g, scale, shift, w, seq, tm):
    m, d = x2.shape
    n = w.shape[1]
    tpb = seq // tm
    return pl.pallas_call(
        _inproj_kernel,
        grid=(m // tm,),
        in_specs=[
            pl.BlockSpec((tm, d), lambda i: (i, 0)),
            pl.BlockSpec((1, d), lambda i: (0, 0)),
            pl.BlockSpec((1, 1, d), lambda i: (i // tpb, 0, 0)),
            pl.BlockSpec((1, 1, d), lambda i: (i // tpb, 0, 0)),
            pl.BlockSpec((d, n), lambda i: (0, 0), pipeline_mode=pl.Buffered(1)),
        ],
        out_specs=pl.BlockSpec((tm, n), lambda i: (i, 0)),
        out_shape=jax.ShapeDtypeStruct((m, n), F32),
        compiler_params=pltpu.CompilerParams(vmem_limit_bytes=VMEM_LIMIT_BYTES),
        name="inproj",
    )(x2, g, scale, shift, w)


def _qkprep_kernel(ak_ref, av_ref, ql_ref, kw_ref, gql_ref, wuq_ref, gq_ref, gk_ref, gki_ref,
                   qt_out, ak_out, qit_out, ki_out, wt_out, vt_out):
    tm = ak_ref.shape[0]
    ones64 = _group_ones(A_WIDTH, A_HEAD_DIM)
    ql = ql_ref[...]
    ql = ql * lax.rsqrt(jnp.mean(ql * ql, axis=-1, keepdims=True) + EPS) * gql_ref[...]
    up = _dot(ql.astype(BF16), wuq_ref[...])
    aq = up[:, :A_WIDTH]
    aq = aq * lax.rsqrt(_group_mean_sq(aq, ones64, A_HEAD_DIM) + EPS) * gq_ref[...]
    aq_t = (aq * (A_HEAD_DIM ** -0.5 * LOG2E)).T
    zero_half = jnp.zeros((A_HEAD_DIM, tm), BF16)
    for h in range(A_HEADS):
        e = h % 2
        qt_out[0, h, e * A_HEAD_DIM:(e + 1) * A_HEAD_DIM, :] = aq_t[h * A_HEAD_DIM:(h + 1) * A_HEAD_DIM].astype(BF16)
        qt_out[0, h, (1 - e) * A_HEAD_DIM:(2 - e) * A_HEAD_DIM, :] = zero_half
    qit_out[0] = up[:, A_WIDTH:].T.reshape(IDX_HEADS, IDX_DIM, tm).astype(BF16)
    ak = ak_ref[...]
    ak = ak * lax.rsqrt(_group_mean_sq(ak, ones64, A_HEAD_DIM) + EPS) * gk_ref[...]
    ak_out[...] = ak.astype(BF16)
    vt_out[0, 0] = av_ref[...].T.reshape(A_HEADS, A_HEAD_DIM, tm).astype(BF16)
    kw = kw_ref[...]
    lane = lax.broadcasted_iota(I32, kw.shape, 1)
    ksq = jnp.where(lane < IDX_DIM, kw * kw, 0.0)
    kms = jnp.sum(ksq, axis=-1, keepdims=True) * (1.0 / IDX_DIM)
    kn = kw * lax.rsqrt(kms + EPS) * gki_ref[...]
    ki_out[...] = kn[:, :IDX_DIM].astype(BF16)
    idx_scale = (IDX_DIM ** -0.5) * (IDX_HEADS ** -0.5)
    w_rows = jnp.where((lane >= IDX_DIM) & (lane < IDX_DIM + IDX_HEADS), kw * idx_scale, 0.0).T
    wt_out[0] = w_rows[IDX_DIM:IDX_DIM + 8]


def _qkprep(proj, gql, wuq, gq, gk, gki, bsz, seq, tm):
    m = proj.shape[0]
    tpb = seq // tm
    row = lambda i: (i, 0)
    const = lambda i: (0, 0)
    return pl.pallas_call(
        _qkprep_kernel,
        grid=(m // tm,),
        in_specs=[
            pl.BlockSpec((tm, A_WIDTH), lambda i: (i, COL_AK // A_WIDTH)),
            pl.BlockSpec((tm, A_WIDTH), lambda i: (i, COL_AV // A_WIDTH)),
            pl.BlockSpec((tm, Q_LORA_RANK), lambda i: (i, COL_QL // Q_LORA_RANK)),
            pl.BlockSpec((tm, LANES), lambda i: (i, COL_KW // LANES)),
            pl.BlockSpec((1, Q_LORA_RANK), const),
            pl.BlockSpec(wuq.shape, const),
            pl.BlockSpec((1, A_WIDTH), const),
            pl.BlockSpec((1, A_WIDTH), const),
            pl.BlockSpec((1, LANES), const),
        ],
        out_specs=[
            pl.BlockSpec((1, A_HEADS, LANES, tm), lambda i: (i // tpb, 0, 0, i % tpb)),
            pl.BlockSpec((tm, A_WIDTH), row),
            pl.BlockSpec((1, IDX_HEADS, IDX_DIM, tm), lambda i: (i // tpb, 0, 0, i % tpb)),
            pl.BlockSpec((tm, IDX_DIM), row),
            pl.BlockSpec((1, 8, tm), lambda i: (i // tpb, 0, i % tpb)),
            pl.BlockSpec((1, 1, A_HEADS, A_HEAD_DIM, tm), lambda i: (i // tpb, i % tpb, 0, 0, 0)),
        ],
        out_shape=[
            jax.ShapeDtypeStruct((bsz, A_HEADS, LANES, seq), BF16),
            jax.ShapeDtypeStruct((m, A_WIDTH), BF16),
            jax.ShapeDtypeStruct((bsz, IDX_HEADS, IDX_DIM, seq), BF16),
            jax.ShapeDtypeStruct((m, IDX_DIM), BF16),
            jax.ShapeDtypeStruct((bsz, 8, seq), F32),
            jax.ShapeDtypeStruct((bsz, tpb, A_HEADS, A_HEAD_DIM, tm), BF16),
        ],
        compiler_params=pltpu.CompilerParams(vmem_limit_bytes=VMEM_LIMIT_BYTES),
        name="qkprep",
    )(proj, proj, proj, proj, gql, wuq, gq, gk, gki)


def _dsa_kernel(qit_ref, wt_ref, ki_ref, qt_ref, k_ref, vt_ref, tril_ref, o_ref,
                sc_ref, acc_ref, p_ref, kn_ref, *, tq, ck, topk, nbis):
    i = pl.program_id(1)
    n_chunks = ((i + 1) * tq + ck - 1) // ck
    kf = float(topk)
    neg_inf = float("-inf")
    pos_inf = float("inf")
    ar = SELECT_ACC_ROWS
    full_acc = lambda val: jnp.full((ar, tq), val, F32)
    row_min = lambda x: jnp.min(x, axis=0, keepdims=True)
    row_max = lambda x: jnp.max(x, axis=0, keepdims=True)
    row_sum = lambda x: jnp.sum(x, axis=0, keepdims=True)
    chunk_rows = lambda jc: pl.ds(pl.multiple_of(jc * ck, ck), ck)

    @pl.when(i == 0)
    def _():
        head_of_lane = lax.broadcasted_iota(I32, (A_WIDTH, LANES), 0) >> _log2(A_HEAD_DIM)
        pick = jnp.where(head_of_lane == lax.broadcasted_iota(I32, (A_WIDTH, LANES), 1), 1.0, 0.0).astype(BF16)

        def kn_body(jc, mx):
            k = k_ref[0, chunk_rows(jc), :].astype(F32)
            return _fold_rows(_dot((k * k).astype(BF16), pick), jnp.maximum, mx)

        mx = lax.fori_loop(0, k_ref.shape[1] // ck, kn_body, jnp.zeros((8, LANES), F32))
        kn_ref[...] = jnp.broadcast_to(row_max(mx), (8, LANES))

    t_pos = i * tq + lax.broadcasted_iota(I32, (ck, tq), 1)
    s_iota = lax.broadcasted_iota(I32, (ck, tq), 0)

    def score_body(masked, jc, carry):
        mn, mx = carry
        kc = ki_ref[0, chunk_rows(jc), :]
        score = jnp.zeros((ck, tq), F32)
        for h in range(IDX_HEADS):
            score = score + wt_ref[0, h:h + 1, :] * jnp.maximum(_dot(kc, qit_ref[0, h]), 0.0)
        if masked:
            causal = (s_iota + jc * ck) <= t_pos
            sc_ref[jc] = jnp.where(causal, score, neg_inf)
            mn = _fold_rows(jnp.where(causal, score, pos_inf), jnp.minimum, mn)
            mx = _fold_rows(jnp.where(causal, score, neg_inf), jnp.maximum, mx)
        else:
            sc_ref[jc] = score
            mn = _fold_rows(score, jnp.minimum, mn)
            mx = _fold_rows(score, jnp.maximum, mx)
        return mn, mx

    n_full = (i * tq + 1) // ck
    carry = lax.fori_loop(0, n_full, functools.partial(score_body, False), (full_acc(pos_inf), full_acc(neg_inf)))
    mn_acc, mx_acc = lax.fori_loop(n_full, n_chunks, functools.partial(score_body, True), carry)
    rmin, rmax = row_min(mn_acc), row_max(mx_acc)

    def count_ge(cand):
        cand_b = jnp.broadcast_to(cand, (ar, tq))

        def body(jc, acc):
            return _fold_rows(sc_ref[jc], lambda a, x: a + jnp.where(x >= cand_b, 1.0, 0.0), acc)

        return row_sum(lax.fori_loop(0, n_chunks, body, full_acc(0.0)))

    def snap(lo, hi):
        lo_b = jnp.broadcast_to(lo, (ar, tq))
        hi_b = jnp.broadcast_to(hi, (ar, tq))

        def body(jc, carry):
            x = sc_ref[jc]
            a = _fold_rows(x, lambda a, x: jnp.minimum(a, jnp.where(x >= lo_b, x, pos_inf)), carry[0])
            b = _fold_rows(x, lambda b, x: jnp.maximum(b, jnp.where(x < hi_b, x, neg_inf)), carry[1])
            return a, b

        a, b = lax.fori_loop(0, n_chunks, body, (full_acc(pos_inf), full_acc(neg_inf)))
        return row_min(a), row_max(b)

    def probe(cand):
        cand_b = jnp.broadcast_to(cand, (ar, tq))

        def body(jc, carry):
            cnt, a, b = carry
            blk = sc_ref[jc]
            for s in range(ck // ar):
                x = blk[s * ar:(s + 1) * ar, :]
                ge = x >= cand_b
                cnt = cnt + jnp.where(ge, 1.0, 0.0)
                a = jnp.minimum(a, jnp.where(ge, x, pos_inf))
                b = jnp.maximum(b, jnp.where(ge, neg_inf, x))
            return cnt, a, b

        cnt, a, b = lax.fori_loop(0, n_chunks, body, (full_acc(0.0), full_acc(pos_inf), full_acc(neg_inf)))
        return row_sum(cnt), row_min(a), row_max(b)

    n_valid = (i * tq + 1 + lax.broadcasted_iota(I32, (1, tq), 1)).astype(F32)
    small = n_valid <= kf

    def interpolate(lo, hi, c_lo, c_hi, w_lo, w_hi):
        f_lo = (c_lo - kf + 0.5) * w_lo
        f_hi = (kf - c_hi - 0.5) * w_hi
        return lo + (hi - lo) * jnp.clip(f_lo / jnp.maximum(f_lo + f_hi, 0.5), 0.1, 0.9)

    def illinois(up, dn, w_lo, w_hi, last):
        w_hi = jnp.where(up, jnp.where(last > 0.0, w_hi * 0.5, 1.0), jnp.where(dn, 1.0, w_hi))
        w_lo = jnp.where(dn, jnp.where(last < 0.0, w_lo * 0.5, 1.0), jnp.where(up, 1.0, w_lo))
        return w_lo, w_hi, jnp.where(up, 1.0, jnp.where(dn, -1.0, last))

    def zero_stats():
        def body(jc, carry):
            c_ge0, c_pos, min_pos = carry
            blk = sc_ref[jc]
            for s in range(ck // ar):
                x = blk[s * ar:(s + 1) * ar, :]
                pos = x > 0.0
                c_ge0 = c_ge0 + jnp.where(x >= 0.0, 1.0, 0.0)
                c_pos = c_pos + jnp.where(pos, 1.0, 0.0)
                min_pos = jnp.minimum(min_pos, jnp.where(pos, x, pos_inf))
            return c_ge0, c_pos, min_pos

        c_ge0, c_pos, min_pos = lax.fori_loop(0, n_chunks, body, (full_acc(0.0), full_acc(0.0), full_acc(pos_inf)))
        return row_sum(c_ge0), row_sum(c_pos), row_min(min_pos)

    c_ge0, c_pos, min_pos = zero_stats()
    below_zero = c_ge0 < kf
    above_zero = c_pos >= kf
    at_zero = jnp.logical_not(below_zero | above_zero)
    hi0 = rmax + jnp.maximum(jnp.abs(rmax) * 1e-6, 1e-30)
    lo = jnp.where(above_zero, min_pos, jnp.where(at_zero, 0.0, rmin))
    c_lo = jnp.where(above_zero, c_pos, jnp.where(at_zero, c_ge0, n_valid))
    hi = jnp.where(above_zero, hi0, 0.0)
    c_hi = jnp.where(above_zero, 0.0, jnp.where(at_zero, c_pos, c_ge0))

    def search_body(_, st):
        lo, hi, c_lo, c_hi, w_lo, w_hi, last = st
        mid = interpolate(lo, hi, c_lo, c_hi, w_lo, w_hi)
        ok = (mid > lo) & (mid < hi)
        c = count_ge(mid)
        up = ok & (c >= kf)
        dn = ok & (c < kf)
        w_lo, w_hi, last = illinois(up, dn, w_lo, w_hi, last)
        return (jnp.where(up, mid, lo), jnp.where(dn, mid, hi), jnp.where(up, c, c_lo), jnp.where(dn, c, c_hi),
                w_lo, w_hi, last)

    zeros_q, ones_q = jnp.zeros((1, tq), F32), jnp.ones((1, tq), F32)
    lo, hi, c_lo, c_hi, _, _, _ = lax.fori_loop(0, nbis, search_body,
                                                (lo, hi, c_lo, c_hi, ones_q, ones_q, zeros_q))

    lo, hi = snap(lo, hi)
    lo, hi = jnp.where(at_zero, 0.0, lo), jnp.where(at_zero, 0.0, hi)

    def active_of(lo, hi, c_lo):
        return jnp.logical_not(small) & (lo < hi) & (c_lo != kf)

    def snap_body(st):
        lo, hi, c_lo, c_hi, w_lo, w_hi, last, _ = st
        act = active_of(lo, hi, c_lo)
        mid = interpolate(lo, hi, c_lo, c_hi, w_lo, w_hi)
        mid = jnp.where((c_hi == kf - 1.0) | (mid <= lo) | (mid > hi), hi, mid)
        c, a, b = probe(mid)
        up = act & (c >= kf)
        dn = act & (c < kf)
        w_lo, w_hi, last = illinois(up, dn, w_lo, w_hi, last)
        lo, c_lo = jnp.where(up, a, lo), jnp.where(up, c, c_lo)
        hi, c_hi = jnp.where(dn, b, hi), jnp.where(dn, c, c_hi)
        return lo, hi, c_lo, c_hi, w_lo, w_hi, last, jnp.max(jnp.where(active_of(lo, hi, c_lo), 1.0, 0.0))

    flag0 = jnp.max(jnp.where(active_of(lo, hi, c_lo), 1.0, 0.0))
    lo, hi, c_lo, c_hi = lax.while_loop(lambda st: st[7] > 0.0, snap_body,
                                        (lo, hi, c_lo, c_hi, ones_q, ones_q, zeros_q, flag0))[:4]

    tau = jnp.where(small, rmin, lo)
    excess = jnp.logical_not(small) & (lo == hi) & (c_lo > kf)
    need = jnp.where(excess, kf - c_hi, 4.0 * 65536.0 * 65536.0)

    @pl.when(jnp.max(jnp.where(excess, 1.0, 0.0)) > 0.0)
    def _():
        def tie_body(jc, carry):
            blk = sc_ref[jc]
            eq = blk == tau
            pc = _dot(tril_ref[...], jnp.where(eq, 1.0, 0.0).astype(BF16)) + carry
            sc_ref[jc] = jnp.where(eq & (pc > need), neg_inf, blk)
            return pc[ck - 1:ck, :]

        lax.fori_loop(0, n_chunks, tie_body, jnp.zeros((1, tq), F32))

    def logits_t(jc, h):
        kc = k_ref[0, chunk_rows(jc), (h // 2) * LANES:(h // 2 + 1) * LANES]
        return _dot(kc, qt_ref[0, h])

    m_bound = []
    bmax = jnp.zeros((1, 1), F32)
    for h in range(A_HEADS):
        qf = qt_ref[0, h].astype(F32)
        bound = jnp.sqrt(row_sum(qf * qf) * kn_ref[0:1, h:h + 1])
        m_bound.append(bound)
        bmax = jnp.maximum(bmax, jnp.max(bound, axis=1, keepdims=True))

    def exact_max(_):
        def max_body(jc, mx):
            sel = sc_ref[jc] >= tau
            return tuple(_fold_rows(jnp.where(sel, logits_t(jc, h), neg_inf), jnp.maximum, mx[h])
                         for h in range(A_HEADS))

        mx = lax.fori_loop(0, n_chunks, max_body, tuple(jnp.full((8, tq), neg_inf, F32) for _ in range(A_HEADS)))
        return tuple(row_max(m) for m in mx)

    m_ref_vals = lax.cond(bmax[0, 0] > LOGIT_BOUND_LIMIT, exact_max, lambda _: tuple(m_bound), 0)

    ones_rows = jnp.ones((16, ck), BF16)
    acc_ref[...] = jnp.zeros(acc_ref.shape, F32)

    def pv_stage(jc, h):
        lhs = jnp.concatenate([vt_ref[0, jc, h], ones_rows], axis=0)
        acc_ref[h] += _dot(lhs, p_ref[h])

    def qk_stage(jc, sel, h):
        p_ref[h] = jnp.where(sel, jnp.exp2(logits_t(jc, h) - m_ref_vals[h]), 0.0).astype(BF16)

    sel0 = sc_ref[0] >= tau
    for h in range(A_HEADS):
        qk_stage(0, sel0, h)

    def att_body(jc, carry):
        sel = sc_ref[jc] >= tau
        for h in range(A_HEADS):
            pv_stage(jc - 1, h)
            qk_stage(jc, sel, h)
        return carry

    lax.fori_loop(1, n_chunks, att_body, 0)
    for h in range(A_HEADS):
        pv_stage(n_chunks - 1, h)
    for h in range(A_HEADS):
        a = acc_ref[h]
        o_ref[0, h] = (a[:A_HEAD_DIM] * (1.0 / a[A_HEAD_DIM:A_HEAD_DIM + 1])).astype(o_ref.dtype)


def _dsa(qit, wt, ki, qt, k, vt, tril, tq, ck, topk):
    bsz, hi, di, seq = qit.shape
    nc = seq // ck
    kern = functools.partial(_dsa_kernel, tq=tq, ck=ck, topk=topk, nbis=BISECT_STEPS)
    return pl.pallas_call(
        kern,
        grid=(bsz, seq // tq),
        in_specs=[
            pl.BlockSpec((1, hi, di, tq), lambda b, i: (b, 0, 0, i)),
            pl.BlockSpec((1, 8, tq), lambda b, i: (b, 0, i)),
            pl.BlockSpec((1, seq, di), lambda b, i: (b, 0, 0)),
            pl.BlockSpec((1, A_HEADS, LANES, tq), lambda b, i: (b, 0, 0, i)),
            pl.BlockSpec((1, seq, A_WIDTH), lambda b, i: (b, 0, 0)),
            pl.BlockSpec((1, nc, A_HEADS, A_HEAD_DIM, ck), lambda b, i: (b, 0, 0, 0, 0)),
            pl.BlockSpec((ck, ck), lambda b, i: (0, 0)),
        ],
        out_specs=pl.BlockSpec((1, A_HEADS, A_HEAD_DIM, tq), lambda b, i: (b, 0, 0, i)),
        out_shape=jax.ShapeDtypeStruct((bsz, A_HEADS, A_HEAD_DIM, seq), BF16),
        scratch_shapes=[
            pltpu.VMEM((nc, ck, tq), F32),
            pltpu.VMEM((A_HEADS, A_HEAD_DIM + 16, tq), F32),
            pltpu.VMEM((A_HEADS, ck, tq), BF16),
            pltpu.VMEM((8, LANES), F32),
        ],
        compiler_params=pltpu.CompilerParams(
            dimension_semantics=("arbitrary", "arbitrary"), vmem_limit_bytes=VMEM_LIMIT_BYTES),
        name="dsa",
    )(qit, wt, ki, qt, k, vt, tril)


def _hgrn_kernel(bq_ref, bf_ref, bi_ref, bg_ref, lb_ref, go_ref, o_ref, st_ref, oi_ref, *, tl, layer):
    cs = HGRN_CHUNK
    nch = tl // cs

    @pl.when(pl.program_id(1) == 0)
    def _():
        st_ref[...] = jnp.zeros(st_ref.shape, F32)

    lbr = lb_ref[...]
    slots = [lbr[k:k + 1] for k in range(lbr.shape[0])]
    mx = functools.reduce(jnp.maximum, slots)
    es = [jnp.exp(s - mx) for s in slots]
    lb = functools.reduce(jnp.add, es[:layer + 1]) / functools.reduce(jnp.add, es)

    bq = bq_ref[...]
    q = bq * _sigmoid(bq)
    f = lb + (1.0 - lb) * _sigmoid(bf_ref[...])
    kk = 1.0 - f
    g = jnp.log(f)
    v = bi_ref[...]

    r_i = lax.broadcasted_iota(I32, (tl, tl), 0)
    c_i = lax.broadcasted_iota(I32, (tl, tl), 1)
    same = (r_i >> _log2(cs)) == (c_i >> _log2(cs))
    tri = jnp.where(same & (c_i <= r_i), 1.0, 0.0).astype(BF16)
    blk = jnp.where(same, 1.0, 0.0).astype(BF16)
    g1, g2, g3 = _split3(g)
    b = _dot(tri, g1) + _dot(tri, g2) + _dot(tri, g3)
    bend = _dot(blk, g1) + _dot(blk, g2) + _dot(blk, g3)

    bk = b - jnp.log(kk)
    pos = lax.broadcasted_iota(I32, (tl, B_WIDTH), 0) & (cs - 1)
    o = jnp.zeros((tl, B_WIDTH), F32)
    for r in range(cs):
        if r == 0:
            bk_s, v_s = bk, v
        else:
            bk_s = pltpu.roll(bk, r, axis=0)
            v_s = pltpu.roll(v, r, axis=0)
        e = q * jnp.exp(jnp.where(pos >= r, b - bk_s, -jnp.inf))
        parts = []
        for h in range(B_HEADS):
            sl = slice(h * B_HEAD_DIM, (h + 1) * B_HEAD_DIM)
            parts.append(jnp.sum(e[:, sl], axis=-1, keepdims=True) * v_s[:, sl])
        o = o + jnp.concatenate(parts, axis=1)

    qe = (q * jnp.exp(b)).astype(BF16)
    kd = (kk * jnp.exp(bend - b)).astype(BF16)
    vb = v.astype(BF16)
    dec = jnp.exp(bend)
    for c in range(nch):
        rs = slice(c * cs, (c + 1) * cs)
        for h in range(B_HEADS):
            sl = slice(h * B_HEAD_DIM, (h + 1) * B_HEAD_DIM)
            st = st_ref[h]
            oi_ref[rs, sl] = lax.dot_general(qe[rs, sl], st.astype(BF16), (((1,), (1,)), ((), ())),
                                             preferred_element_type=F32)
            upd = lax.dot_general(vb[rs, sl], kd[rs, sl], (((0,), (0,)), ((), ())),
                                  preferred_element_type=F32)
            st_ref[h] = st * dec[c * cs:c * cs + 1, sl] + upd
    o = o + oi_ref[...]

    parts = []
    for h in range(B_HEADS):
        sl = slice(h * B_HEAD_DIM, (h + 1) * B_HEAD_DIM)
        oh = o[:, sl]
        parts.append(oh * lax.rsqrt(jnp.mean(oh * oh, axis=-1, keepdims=True) + EPS))
    on = jnp.concatenate(parts, axis=1) * go_ref[...]
    bg = bg_ref[...]
    o_ref[...] = (on * (bg * _sigmoid(bg))).astype(o_ref.dtype)


def _hgrn(proj, lb_table, go, bsz, seq, tl, layer):
    m = proj.shape[0]
    tpb = seq // tl
    col = lambda c: (lambda b, t: (b * tpb + t, c // B_WIDTH))
    return pl.pallas_call(
        functools.partial(_hgrn_kernel, tl=tl, layer=layer),
        grid=(bsz, tpb),
        in_specs=[
            pl.BlockSpec((tl, B_WIDTH), col(COL_BQ)),
            pl.BlockSpec((tl, B_WIDTH), col(COL_BF)),
            pl.BlockSpec((tl, B_WIDTH), col(COL_BI)),
            pl.BlockSpec((tl, B_WIDTH), col(COL_BG)),
            pl.BlockSpec(lb_table.shape, lambda b, t: (0, 0)),
            pl.BlockSpec((1, B_WIDTH), lambda b, t: (0, 0)),
        ],
        out_specs=pl.BlockSpec((tl, B_WIDTH), lambda b, t: (b * tpb + t, 0)),
        out_shape=jax.ShapeDtypeStruct((m, B_WIDTH), BF16),
        scratch_shapes=[
            pltpu.VMEM((B_HEADS, B_HEAD_DIM, B_HEAD_DIM), F32),
            pltpu.VMEM((tl, B_WIDTH), F32),
        ],
        compiler_params=pltpu.CompilerParams(
            dimension_semantics=("arbitrary", "arbitrary"), vmem_limit_bytes=VMEM_LIMIT_BYTES),
        name="hgrn",
    )(proj, proj, proj, proj, lb_table, go)


def _tail_kernel(oat_ref, ob_ref, ga_ref, gb_ref, x_ref, g1_ref, wa_ref, wb_ref, wo_ref,
                 n2_ref, sc_ref, sh_ref, g2_ref, w1_ref, b1_ref, w2_ref, b2_ref, o_ref, *, tf):
    tm = x_ref.shape[0]
    oa = oat_ref[0].reshape(A_WIDTH, tm).astype(F32).T.astype(BF16)
    pa = _dot(oa, wa_ref[...])
    pb = _dot(ob_ref[...], wb_ref[...])
    merged = _sigmoid(ga_ref[...]) * pa + _sigmoid(gb_ref[...]) * pb
    x = x_ref[...] + g1_ref[0] * _dot(merged.astype(BF16), wo_ref[...])
    ms = jnp.mean(x * x, axis=-1, keepdims=True)
    h = x * lax.rsqrt(ms + EPS) * n2_ref[...]
    h = (h * (1.0 + sc_ref[0]) + sh_ref[0]).astype(BF16)
    dff = w1_ref.shape[1]
    y = jnp.zeros(x.shape, F32)
    for c in range(dff // tf):
        cs = slice(c * tf, (c + 1) * tf)
        a = jnp.maximum(_dot(h, w1_ref[:, cs]) + b1_ref[:, cs], 0.0)
        y = y + _dot((a * a).astype(BF16), w2_ref[cs, :])
    o_ref[...] = x + g2_ref[0] * (y + b2_ref[...])


def _tail(oat, ob, proj, x2, gate1, wa, wb, wo, n2, scale2, shift2, gate2, w1, b1, w2, b2, seq, tm, tf):
    m, d = x2.shape
    tpb = seq // tm
    row = lambda i: (i, 0)
    bat = lambda i: (i // tpb, 0, 0)
    resident = lambda a: pl.BlockSpec(a.shape, lambda i: (0, 0), pipeline_mode=pl.Buffered(1))
    return pl.pallas_call(
        functools.partial(_tail_kernel, tf=tf),
        grid=(m // tm,),
        in_specs=[
            pl.BlockSpec((1, A_HEADS, A_HEAD_DIM, tm), lambda i: (i // tpb, 0, 0, i % tpb)),
            pl.BlockSpec((tm, B_WIDTH), row),
            pl.BlockSpec((tm, d), lambda i: (i, COL_GA // d)),
            pl.BlockSpec((tm, d), lambda i: (i, COL_GB // d)),
            pl.BlockSpec((tm, d), row),
            pl.BlockSpec((1, 1, d), bat),
            resident(wa), resident(wb), resident(wo),
            resident(n2),
            pl.BlockSpec((1, 1, d), bat),
            pl.BlockSpec((1, 1, d), bat),
            pl.BlockSpec((1, 1, d), bat),
            resident(w1), resident(b1), resident(w2), resident(b2),
        ],
        out_specs=pl.BlockSpec((tm, d), row),
        out_shape=jax.ShapeDtypeStruct((m, d), F32),
        compiler_params=pltpu.CompilerParams(vmem_limit_bytes=VMEM_LIMIT_BYTES),
        name="tail",
    )(oat, ob, proj, proj, x2, gate1, wa, wb, wo, n2, scale2, shift2, gate2, w1, b1, w2, b2)


def _regroup_kernel(w_ref, o_ref, *, d):
    kv0 = Q_LORA_RANK
    kw0 = kv0 + 2 * A_WIDTH
    rest0 = kw0 + IDX_DIM + IDX_HEADS
    rest = 4 * B_WIDTH + 2 * d
    w = w_ref[0]
    o_ref[:, COL_AK:COL_AK + 2 * A_WIDTH] = w[:, kv0:kw0].astype(BF16)
    o_ref[:, COL_BQ:COL_BQ + rest] = w[:, rest0:rest0 + rest].astype(BF16)
    o_ref[:, COL_QL:COL_QL + Q_LORA_RANK] = w[:, :Q_LORA_RANK].astype(BF16)
    kw = w[:, kw0:kw0 + LANES]
    lane = lax.broadcasted_iota(I32, kw.shape, 1)
    o_ref[:, COL_KW:COL_KW + LANES] = jnp.where(lane < IDX_DIM + IDX_HEADS, kw, 0.0).astype(BF16)


def _regroup_w_in(w_all, layer):
    _, d, n = w_all.shape
    tr = 128
    assert COL_BQ + 4 * B_WIDTH + 2 * d == COL_QL and COL_QL + Q_LORA_RANK == COL_KW
    return pl.pallas_call(
        functools.partial(_regroup_kernel, d=d),
        grid=(d // tr,),
        in_specs=[pl.BlockSpec((1, tr, n), lambda i: (layer, i, 0))],
        out_specs=pl.BlockSpec((tr, PROJ_COLS), lambda i: (i, 0)),
        out_shape=jax.ShapeDtypeStruct((d, PROJ_COLS), BF16),
        compiler_params=pltpu.CompilerParams(vmem_limit_bytes=VMEM_LIMIT_BYTES),
        name="regroup",
    )(w_all)


def _layer(x, mod, l, p, tiles):
    bsz, seq, d = x.shape
    m = bsz * seq
    shift1, scale1, gate1, shift2, scale2, gate2 = (mod[:, k * d:(k + 1) * d].reshape(bsz, 1, d) for k in range(6))
    x2 = x.reshape(m, d)

    proj = _inproj(x2, p['norm1_g'][l][None], scale1, shift1, _regroup_w_in(p['w_in'], l), seq, tiles['tm_in'])

    tq, ck = tiles['tq'], tiles['ck']
    wuq = jnp.concatenate([p['w_uq'][l], p['w_uq_idx'][l]], axis=1).astype(BF16)
    gki = jnp.concatenate([p['k_idx_norm_g'][l], jnp.zeros((LANES - IDX_DIM,), F32)])[None]
    qt, ak, qit, ki, wt, vt = _qkprep(
        proj, p['q_lat_norm_g'][l][None], wuq,
        jnp.tile(p['q_norm_g'][l], A_HEADS)[None], jnp.tile(p['k_norm_g'][l], A_HEADS)[None], gki, bsz, seq, ck)

    topk = min(TOPK_MAX, seq // 4)
    tril = jnp.tril(jnp.ones((ck, ck), BF16))
    out_at = _dsa(qit, wt, ki.reshape(bsz, seq, IDX_DIM), qt, ak.reshape(bsz, seq, A_WIDTH), vt, tril, tq, ck, topk)

    out_b = _hgrn(proj, p['hgrn_lb'], p['hgrn_o_norm_g'][l][None], bsz, seq, tiles['tl'], l)

    out = _tail(out_at, out_b, proj, x2, gate1, p['w_proj_a'][l].astype(BF16), p['w_proj_b'][l].astype(BF16),
                p['w_out'][l].astype(BF16), p['norm2_g'][l][None], scale2, shift2, gate2,
                p['w_mlp1'][l].astype(BF16), p['b_mlp1'][l][None], p['w_mlp2'][l].astype(BF16), p['b_mlp2'][l][None],
                seq, tiles['tm'], tiles['tf'])
    return out.reshape(bsz, seq, d)


def _tiles(seq):
    pick = lambda want: min(want, seq)
    return dict(tm_in=pick(512), tm=pick(512), tq=pick(256), ck=pick(512), tl=pick(256), tf=1024)


def kernel(x, c, w_ada, b_ada, norm1_g, w_in, q_lat_norm_g, w_uq, w_uq_idx, q_norm_g, k_norm_g, k_idx_norm_g,
           hgrn_lb, hgrn_o_norm_g, w_proj_a, w_proj_b, w_out, norm2_g, w_mlp1, b_mlp1, w_mlp2, b_mlp2):
    p = dict(norm1_g=norm1_g, w_in=w_in, q_lat_norm_g=q_lat_norm_g, w_uq=w_uq, w_uq_idx=w_uq_idx,
             q_norm_g=q_norm_g, k_norm_g=k_norm_g, k_idx_norm_g=k_idx_norm_g, hgrn_lb=hgrn_lb,
             hgrn_o_norm_g=hgrn_o_norm_g, w_proj_a=w_proj_a, w_proj_b=w_proj_b, w_out=w_out, norm2_g=norm2_g,
             w_mlp1=w_mlp1, b_mlp1=b_mlp1, w_mlp2=w_mlp2, b_mlp2=b_mlp2)
    bsz, seq, d = x.shape
    depth = w_ada.shape[0]
    tiles = _tiles(seq)
    c_pad = jnp.zeros((8, d), F32).at[:bsz].set(c)
    for l in range(depth):
        mod = _adaln(c_pad, w_ada[l], b_ada[l][None])[:bsz]
        x = _layer(x, mod, l, p, tiles)
    return x
```

```python
import functools

import jax
import jax.numpy as jnp
from jax import lax
from jax.experimental import pallas as pl
from jax.experimental.pallas import tpu as pltpu

F32 = jnp.float32
BF16 = jnp.bfloat16
I32 = jnp.int32

EPS = 1e-6
A_HEADS = 8
A_HEAD_DIM = 64
A_WIDTH = A_HEADS * A_HEAD_DIM
Q_LORA_RANK = 256
IDX_HEADS = 4
IDX_DIM = 64
TOPK_MAX = 256
B_WIDTH = 512
B_HEADS = 4
B_HEAD_DIM = 128
HGRN_CHUNK = 16

LANES = 128
VMEM_LIMIT_BYTES = 56 * 1024 * 1024

COL_AK, COL_AV, COL_BQ, COL_BF, COL_BI, COL_BG = 0, 512, 1024, 1536, 2048, 2560
COL_GA, COL_GB, COL_QL, COL_KW = 3072, 4096, 5120, 5376
PROJ_COLS = 5504

LOG2E = 1.4426950408889634
COUNT_SEARCH_PASSES = 12
SELECT_ACC_ROWS = 32
LOGIT_BOUND_LIMIT = 60.0


def _sigmoid(x):
    return 0.5 * jnp.tanh(0.5 * x) + 0.5


def _split2(x):
    hi = x.astype(BF16)
    lo = (x - hi.astype(F32)).astype(BF16)
    return hi, lo


def _split3(x):
    a = x.astype(BF16)
    r = x - a.astype(F32)
    b = r.astype(BF16)
    c = (r - b.astype(F32)).astype(BF16)
    return a, b, c


def _dot(a, b):
    return jnp.dot(a, b, preferred_element_type=F32)


def _log2(n):
    assert n > 0 and n & (n - 1) == 0, n
    return n.bit_length() - 1


def _group_ones(n, group):
    r = lax.broadcasted_iota(I32, (n, n), 0) >> _log2(group)
    c = lax.broadcasted_iota(I32, (n, n), 1) >> _log2(group)
    return jnp.where(r == c, 1.0, 0.0).astype(BF16)


def _group_mean_sq(x, ones_bd, group):
    hi, lo = _split2(x * x)
    return (_dot(hi, ones_bd) + _dot(lo, ones_bd)) * (1.0 / group)


def _fold_rows(x, op, init):
    r = init.shape[0]
    for s in range(x.shape[0] // r):
        init = op(init, x[s * r:(s + 1) * r, :])
    return init


def _adaln_kernel(c_ref, w_ref, b_ref, o_ref):
    c = c_ref[...]
    a = c * _sigmoid(c)
    a1, a2, a3 = _split3(a)
    w1, w2, w3 = _split3(w_ref[...])
    acc = _dot(a1, w1) + (_dot(a1, w2) + _dot(a2, w1)) + (_dot(a2, w2) + _dot(a1, w3) + _dot(a3, w1))
    o_ref[...] = acc + b_ref[...]


def _adaln(c_pad, w, b):
    rows, d = c_pad.shape
    n = w.shape[1]
    tn = 1536
    return pl.pallas_call(
        _adaln_kernel,
        grid=(n // tn,),
        in_specs=[
            pl.BlockSpec((rows, d), lambda j: (0, 0)),
            pl.BlockSpec((d, tn), lambda j: (0, j)),
            pl.BlockSpec((1, tn), lambda j: (0, j)),
        ],
        out_specs=pl.BlockSpec((rows, tn), lambda j: (0, j)),
        out_shape=jax.ShapeDtypeStruct((rows, n), F32),
        compiler_params=pltpu.CompilerParams(vmem_limit_bytes=VMEM_LIMIT_BYTES),
        name="adaln",
    )(c_pad, w, b)


def _inproj_kernel(x_ref, g_ref, sc_ref, sh_ref, w_ref, o_ref):
    x = x_ref[...]
    ms = jnp.mean(x * x, axis=-1, keepdims=True)
    h = x * lax.rsqrt(ms + EPS) * g_ref[...]
    h = h * (1.0 + sc_ref[0]) + sh_ref[0]
    o_ref[...] = _dot(h.astype(BF16), w_ref[...])


def _inproj(x2, g, scale, shift, w, seq, tm):
    m, d = x2.shape
    n = w.shape[1]
    tpb = seq // tm
    return pl.pallas_call(
        _inproj_kernel,
        grid=(m // tm,),
        in_specs=[
            pl.BlockSpec((tm, d), lambda i: (i, 0)),
            pl.BlockSpec((1, d), lambda i: (0, 0)),
            pl.BlockSpec((1, 1, d), lambda i: (i // tpb, 0, 0)),
            pl.BlockSpec((1, 1, d), lambda i: (i // tpb, 0, 0)),
            pl.BlockSpec((d, n), lambda i: (0, 0), pipeline_mode=pl.Buffered(1)),
        ],
        out_specs=pl.BlockSpec((tm, n), lambda i: (i, 0)),
        out_shape=jax.ShapeDtypeStruct((m, n), F32),
        compiler_params=pltpu.CompilerParams(vmem_limit_bytes=VMEM_LIMIT_BYTES),
        name="inproj",
    )(x2, g, scale, shift, w)


def _qkprep_kernel(ak_ref, av_ref, ql_ref, kw_ref, gql_ref, wuq_ref, gq_ref, gk_ref, gki_ref,
                   qt_out, ak_out, qit_out, ki_out, wt_out, vt_out):
    tm = ak_ref.shape[0]
    ones64 = _group_ones(A_WIDTH, A_HEAD_DIM)
    ql = ql_ref[...]
    ql = ql * lax.rsqrt(jnp.mean(ql * ql, axis=-1, keepdims=True) + EPS) * gql_ref[...]
    up = _dot(ql.astype(BF16), wuq_ref[...])
    aq = up[:, :A_WIDTH]
    aq = aq * lax.rsqrt(_group_mean_sq(aq, ones64, A_HEAD_DIM) + EPS) * gq_ref[...]
    aq_t = (aq * (A_HEAD_DIM ** -0.5 * LOG2E)).T
    zero_half = jnp.zeros((A_HEAD_DIM, tm), BF16)
    for h in range(A_HEADS):
        e = h % 2
        qt_out[0, h, e * A_HEAD_DIM:(e + 1) * A_HEAD_DIM, :] = aq_t[h * A_HEAD_DIM:(h + 1) * A_HEAD_DIM].astype(BF16)
        qt_out[0, h, (1 - e) * A_HEAD_DIM:(2 - e) * A_HEAD_DIM, :] = zero_half
    qit_out[0] = up[:, A_WIDTH:].T.reshape(IDX_HEADS, IDX_DIM, tm).astype(BF16)
    ak = ak_ref[...]
    ak = ak * lax.rsqrt(_group_mean_sq(ak, ones64, A_HEAD_DIM) + EPS) * gk_ref[...]
    ak_out[...] = ak.astype(BF16)
    vt_out[0, 0] = av_ref[...].T.reshape(A_HEADS, A_HEAD_DIM, tm).astype(BF16)
    kw = kw_ref[...]
    lane = lax.broadcasted_iota(I32, kw.shape, 1)
    ksq = jnp.where(lane < IDX_DIM, kw * kw, 0.0)
    kms = jnp.sum(ksq, axis=-1, keepdims=True) * (1.0 / IDX_DIM)
    kn = kw * lax.rsqrt(kms + EPS) * gki_ref[...]
    ki_out[...] = kn[:, :IDX_DIM].astype(BF16)
    idx_scale = (IDX_DIM ** -0.5) * (IDX_HEADS ** -0.5)
    w_rows = jnp.where((lane >= IDX_DIM) & (lane < IDX_DIM + IDX_HEADS), kw * idx_scale, 0.0).T
    wt_out[0] = w_rows[IDX_DIM:IDX_DIM + 8]


def _qkprep(proj, gql, wuq, gq, gk, gki, bsz, seq, tm):
    m = proj.shape[0]
    tpb = seq // tm
    row = lambda i: (i, 0)
    const = lambda i: (0, 0)
    return pl.pallas_call(
        _qkprep_kernel,
        grid=(m // tm,),
        in_specs=[
            pl.BlockSpec((tm, A_WIDTH), lambda i: (i, COL_AK // A_WIDTH)),
            pl.BlockSpec((tm, A_WIDTH), lambda i: (i, COL_AV // A_WIDTH)),
            pl.BlockSpec((tm, Q_LORA_RANK), lambda i: (i, COL_QL // Q_LORA_RANK)),
            pl.BlockSpec((tm, LANES), lambda i: (i, COL_KW // LANES)),
            pl.BlockSpec((1, Q_LORA_RANK), const),
            pl.BlockSpec(wuq.shape, const),
            pl.BlockSpec((1, A_WIDTH), const),
            pl.BlockSpec((1, A_WIDTH), const),
            pl.BlockSpec((1, LANES), const),
        ],
        out_specs=[
            pl.BlockSpec((1, A_HEADS, LANES, tm), lambda i: (i // tpb, 0, 0, i % tpb)),
            pl.BlockSpec((tm, A_WIDTH), row),
            pl.BlockSpec((1, IDX_HEADS, IDX_DIM, tm), lambda i: (i // tpb, 0, 0, i % tpb)),
            pl.BlockSpec((tm, IDX_DIM), row),
            pl.BlockSpec((1, 8, tm), lambda i: (i // tpb, 0, i % tpb)),
            pl.BlockSpec((1, 1, A_HEADS, A_HEAD_DIM, tm), lambda i: (i // tpb, i % tpb, 0, 0, 0)),
        ],
        out_shape=[
            jax.ShapeDtypeStruct((bsz, A_HEADS, LANES, seq), BF16),
            jax.ShapeDtypeStruct((m, A_WIDTH), BF16),
            jax.ShapeDtypeStruct((bsz, IDX_HEADS, IDX_DIM, seq), BF16),
            jax.ShapeDtypeStruct((m, IDX_DIM), BF16),
            jax.ShapeDtypeStruct((bsz, 8, seq), F32),
            jax.ShapeDtypeStruct((bsz, tpb, A_HEADS, A_HEAD_DIM, tm), BF16),
        ],
        compiler_params=pltpu.CompilerParams(vmem_limit_bytes=VMEM_LIMIT_BYTES),
        name="qkprep",
    )(proj, proj, proj, proj, gql, wuq, gq, gk, gki)


def _dsa_kernel(qit_ref, wt_ref, ki_ref, qt_ref, k_ref, vt_ref, tril_ref, o_ref,
                sc_ref, acc_ref, p_ref, kn_ref, *, tq, ck, topk, n_count_passes):
    i = pl.program_id(1)
    n_chunks = ((i + 1) * tq + ck - 1) // ck
    kf = float(topk)
    neg_inf = float("-inf")
    pos_inf = float("inf")
    ar = SELECT_ACC_ROWS
    full_acc = lambda val: jnp.full((ar, tq), val, F32)
    row_min = lambda x: jnp.min(x, axis=0, keepdims=True)
    row_max = lambda x: jnp.max(x, axis=0, keepdims=True)
    row_sum = lambda x: jnp.sum(x, axis=0, keepdims=True)
    chunk_rows = lambda jc: pl.ds(pl.multiple_of(jc * ck, ck), ck)

    @pl.when(i == 0)
    def _():
        head_of_lane = lax.broadcasted_iota(I32, (A_WIDTH, LANES), 0) >> _log2(A_HEAD_DIM)
        pick = jnp.where(head_of_lane == lax.broadcasted_iota(I32, (A_WIDTH, LANES), 1), 1.0, 0.0).astype(BF16)

        def kn_body(jc, mx):
            k = k_ref[0, chunk_rows(jc), :].astype(F32)
            return _fold_rows(_dot((k * k).astype(BF16), pick), jnp.maximum, mx)

        mx = lax.fori_loop(0, k_ref.shape[1] // ck, kn_body, jnp.zeros((8, LANES), F32))
        kn_ref[...] = jnp.broadcast_to(row_max(mx), (8, LANES))

    t_pos = i * tq + lax.broadcasted_iota(I32, (ck, tq), 1)
    s_iota = lax.broadcasted_iota(I32, (ck, tq), 0)

    def score_body(masked, jc, carry):
        mn, mx = carry
        kc = ki_ref[0, chunk_rows(jc), :]
        score = jnp.zeros((ck, tq), F32)
        for h in range(IDX_HEADS):
            score = score + wt_ref[0, h:h + 1, :] * jnp.maximum(_dot(kc, qit_ref[0, h]), 0.0)
        if masked:
            causal = (s_iota + jc * ck) <= t_pos
            sc_ref[jc] = jnp.where(causal, score, neg_inf)
            mn = _fold_rows(jnp.where(causal, score, pos_inf), jnp.minimum, mn)
            mx = _fold_rows(jnp.where(causal, score, neg_inf), jnp.maximum, mx)
        else:
            sc_ref[jc] = score
            mn = _fold_rows(score, jnp.minimum, mn)
            mx = _fold_rows(score, jnp.maximum, mx)
        return mn, mx

    n_full = (i * tq + 1) // ck
    carry = lax.fori_loop(0, n_full, functools.partial(score_body, False), (full_acc(pos_inf), full_acc(neg_inf)))
    mn_acc, mx_acc = lax.fori_loop(n_full, n_chunks, functools.partial(score_body, True), carry)
    rmin, rmax = row_min(mn_acc), row_max(mx_acc)

    def count_ge(cand):
        cand_b = jnp.broadcast_to(cand, (ar, tq))

        def body(jc, acc):
            return _fold_rows(sc_ref[jc], lambda a, x: a + jnp.where(x >= cand_b, 1.0, 0.0), acc)

        return row_sum(lax.fori_loop(0, n_chunks, body, full_acc(0.0)))

    def snap(lo, hi):
        lo_b = jnp.broadcast_to(lo, (ar, tq))
        hi_b = jnp.broadcast_to(hi, (ar, tq))

        def body(jc, carry):
            x = sc_ref[jc]
            a = _fold_rows(x, lambda a, x: jnp.minimum(a, jnp.where(x >= lo_b, x, pos_inf)), carry[0])
            b = _fold_rows(x, lambda b, x: jnp.maximum(b, jnp.where(x < hi_b, x, neg_inf)), carry[1])
            return a, b

        a, b = lax.fori_loop(0, n_chunks, body, (full_acc(pos_inf), full_acc(neg_inf)))
        return row_min(a), row_max(b)

    def probe(cand):
        cand_b = jnp.broadcast_to(cand, (ar, tq))

        def body(jc, carry):
            cnt, a, b = carry
            blk = sc_ref[jc]
            for s in range(ck // ar):
                x = blk[s * ar:(s + 1) * ar, :]
                ge = x >= cand_b
                cnt = cnt + jnp.where(ge, 1.0, 0.0)
                a = jnp.minimum(a, jnp.where(ge, x, pos_inf))
                b = jnp.maximum(b, jnp.where(ge, neg_inf, x))
            return cnt, a, b

        cnt, a, b = lax.fori_loop(0, n_chunks, body, (full_acc(0.0), full_acc(pos_inf), full_acc(neg_inf)))
        return row_sum(cnt), row_min(a), row_max(b)

    n_valid = (i * tq + 1 + lax.broadcasted_iota(I32, (1, tq), 1)).astype(F32)
    small = n_valid <= kf

    def interpolate(lo, hi, c_lo, c_hi, w_lo, w_hi):
        f_lo = (c_lo - kf + 0.5) * w_lo
        f_hi = (kf - c_hi - 0.5) * w_hi
        return lo + (hi - lo) * jnp.clip(f_lo / jnp.maximum(f_lo + f_hi, 0.5), 0.1, 0.9)

    def illinois(up, dn, w_lo, w_hi, last):
        w_hi = jnp.where(up, jnp.where(last > 0.0, w_hi * 0.5, 1.0), jnp.where(dn, 1.0, w_hi))
        w_lo = jnp.where(dn, jnp.where(last < 0.0, w_lo * 0.5, 1.0), jnp.where(up, 1.0, w_lo))
        return w_lo, w_hi, jnp.where(up, 1.0, jnp.where(dn, -1.0, last))

    def zero_stats():
        def body(jc, carry):
            c_ge0, c_pos, min_pos = carry
            blk = sc_ref[jc]
            for s in range(ck // ar):
                x = blk[s * ar:(s + 1) * ar, :]
                pos = x > 0.0
                c_ge0 = c_ge0 + jnp.where(x >= 0.0, 1.0, 0.0)
                c_pos = c_pos + jnp.where(pos, 1.0, 0.0)
                min_pos = jnp.minimum(min_pos, jnp.where(pos, x, pos_inf))
            return c_ge0, c_pos, min_pos

        c_ge0, c_pos, min_pos = lax.fori_loop(0, n_chunks, body, (full_acc(0.0), full_acc(0.0), full_acc(pos_inf)))
        return row_sum(c_ge0), row_sum(c_pos), row_min(min_pos)

    c_ge0, c_pos, min_pos = zero_stats()
    below_zero = c_ge0 < kf
    above_zero = c_pos >= kf
    at_zero = jnp.logical_not(below_zero | above_zero)
    hi0 = rmax + jnp.maximum(jnp.abs(rmax) * 1e-6, 1e-30)
    lo = jnp.where(above_zero, min_pos, jnp.where(at_zero, 0.0, rmin))
    c_lo = jnp.where(above_zero, c_pos, jnp.where(at_zero, c_ge0, n_valid))
    hi = jnp.where(above_zero, hi0, 0.0)
    c_hi = jnp.where(above_zero, 0.0, jnp.where(at_zero, c_pos, c_ge0))

    def search_body(_, st):
        lo, hi, c_lo, c_hi, w_lo, w_hi, last = st
        mid = interpolate(lo, hi, c_lo, c_hi, w_lo, w_hi)
        ok = (mid > lo) & (mid < hi)
        c = count_ge(mid)
        up = ok & (c >= kf)
        dn = ok & (c < kf)
        w_lo, w_hi, last = illinois(up, dn, w_lo, w_hi, last)
        return (jnp.where(up, mid, lo), jnp.where(dn, mid, hi), jnp.where(up, c, c_lo), jnp.where(dn, c, c_hi),
                w_lo, w_hi, last)

    zeros_q, ones_q = jnp.zeros((1, tq), F32), jnp.ones((1, tq), F32)
    lo, hi, c_lo, c_hi, _, _, _ = lax.fori_loop(0, n_count_passes, search_body,
                                                (lo, hi, c_lo, c_hi, ones_q, ones_q, zeros_q))

    lo, hi = snap(lo, hi)
    lo, hi = jnp.where(at_zero, 0.0, lo), jnp.where(at_zero, 0.0, hi)

    def active_of(lo, hi, c_lo):
        return jnp.logical_not(small) & (lo < hi) & (c_lo != kf)

    def snap_body(st):
        lo, hi, c_lo, c_hi, w_lo, w_hi, last, _ = st
        act = active_of(lo, hi, c_lo)
        mid = interpolate(lo, hi, c_lo, c_hi, w_lo, w_hi)
        mid = jnp.where((c_hi == kf - 1.0) | (mid <= lo) | (mid > hi), hi, mid)
        c, a, b = probe(mid)
        up = act & (c >= kf)
        dn = act & (c < kf)
        w_lo, w_hi, last = illinois(up, dn, w_lo, w_hi, last)
        lo, c_lo = jnp.where(up, a, lo), jnp.where(up, c, c_lo)
        hi, c_hi = jnp.where(dn, b, hi), jnp.where(dn, c, c_hi)
        return lo, hi, c_lo, c_hi, w_lo, w_hi, last, jnp.max(jnp.where(active_of(lo, hi, c_lo), 1.0, 0.0))

    flag0 = jnp.max(jnp.where(active_of(lo, hi, c_lo), 1.0, 0.0))
    lo, hi, c_lo, c_hi = lax.while_loop(lambda st: st[7] > 0.0, snap_body,
                                        (lo, hi, c_lo, c_hi, ones_q, ones_q, zeros_q, flag0))[:4]

    tau = jnp.where(small, rmin, lo)
    excess = jnp.logical_not(small) & (lo == hi) & (c_lo > kf)
    need = jnp.where(excess, kf - c_hi, 4.0 * 65536.0 * 65536.0)

    @pl.when(jnp.max(jnp.where(excess, 1.0, 0.0)) > 0.0)
    def _():
        def tie_body(jc, carry):
            blk = sc_ref[jc]
            eq = blk == tau
            pc = _dot(tril_ref[...], jnp.where(eq, 1.0, 0.0).astype(BF16)) + carry
            sc_ref[jc] = jnp.where(eq & (pc > need), neg_inf, blk)
            return pc[ck - 1:ck, :]

        lax.fori_loop(0, n_chunks, tie_body, jnp.zeros((1, tq), F32))

    def logits_t(jc, h):
        kc = k_ref[0, chunk_rows(jc), (h // 2) * LANES:(h // 2 + 1) * LANES]
        return _dot(kc, qt_ref[0, h])

    m_bound = []
    bmax = jnp.zeros((1, 1), F32)
    for h in range(A_HEADS):
        qf = qt_ref[0, h].astype(F32)
        bound = jnp.sqrt(row_sum(qf * qf) * kn_ref[0:1, h:h + 1])
        m_bound.append(bound)
        bmax = jnp.maximum(bmax, jnp.max(bound, axis=1, keepdims=True))

    def exact_max(_):
        def max_body(jc, mx):
            sel = sc_ref[jc] >= tau
            return tuple(_fold_rows(jnp.where(sel, logits_t(jc, h), neg_inf), jnp.maximum, mx[h])
                         for h in range(A_HEADS))

        mx = lax.fori_loop(0, n_chunks, max_body, tuple(jnp.full((8, tq), neg_inf, F32) for _ in range(A_HEADS)))
        return tuple(row_max(m) for m in mx)

    m_ref_vals = lax.cond(bmax[0, 0] > LOGIT_BOUND_LIMIT, exact_max, lambda _: tuple(m_bound), 0)

    ones_rows = jnp.ones((16, ck), BF16)
    acc_ref[...] = jnp.zeros(acc_ref.shape, F32)

    def pv_stage(jc, h):
        lhs = jnp.concatenate([vt_ref[0, jc, h], ones_rows], axis=0)
        acc_ref[h] += _dot(lhs, p_ref[h])

    def qk_stage(jc, sel, h):
        p_ref[h] = jnp.where(sel, jnp.exp2(logits_t(jc, h) - m_ref_vals[h]), 0.0).astype(BF16)

    sel0 = sc_ref[0] >= tau
    for h in range(A_HEADS):
        qk_stage(0, sel0, h)

    def att_body(jc, carry):
        sel = sc_ref[jc] >= tau
        for h in range(A_HEADS):
            pv_stage(jc - 1, h)
            qk_stage(jc, sel, h)
        return carry

    lax.fori_loop(1, n_chunks, att_body, 0)
    for h in range(A_HEADS):
        pv_stage(n_chunks - 1, h)
    for h in range(A_HEADS):
        a = acc_ref[h]
        o_ref[0, h] = (a[:A_HEAD_DIM] * (1.0 / a[A_HEAD_DIM:A_HEAD_DIM + 1])).astype(o_ref.dtype)


def _dsa(qit, wt, ki, qt, k, vt, tril, tq, ck, topk):
    bsz, hi, di, seq = qit.shape
    nc = seq // ck
    kern = functools.partial(_dsa_kernel, tq=tq, ck=ck, topk=topk, n_count_passes=COUNT_SEARCH_PASSES)
    return pl.pallas_call(
        kern,
        grid=(bsz, seq // tq),
        in_specs=[
            pl.BlockSpec((1, hi, di, tq), lambda b, i: (b, 0, 0, i)),
            pl.BlockSpec((1, 8, tq), lambda b, i: (b, 0, i)),
            pl.BlockSpec((1, seq, di), lambda b, i: (b, 0, 0)),
            pl.BlockSpec((1, A_HEADS, LANES, tq), lambda b, i: (b, 0, 0, i)),
            pl.BlockSpec((1, seq, A_WIDTH), lambda b, i: (b, 0, 0)),
            pl.BlockSpec((1, nc, A_HEADS, A_HEAD_DIM, ck), lambda b, i: (b, 0, 0, 0, 0)),
            pl.BlockSpec((ck, ck), lambda b, i: (0, 0)),
        ],
        out_specs=pl.BlockSpec((1, A_HEADS, A_HEAD_DIM, tq), lambda b, i: (b, 0, 0, i)),
        out_shape=jax.ShapeDtypeStruct((bsz, A_HEADS, A_HEAD_DIM, seq), BF16),
        scratch_shapes=[
            pltpu.VMEM((nc, ck, tq), F32),
            pltpu.VMEM((A_HEADS, A_HEAD_DIM + 16, tq), F32),
            pltpu.VMEM((A_HEADS, ck, tq), BF16),
            pltpu.VMEM((8, LANES), F32),
        ],
        compiler_params=pltpu.CompilerParams(
            dimension_semantics=("arbitrary", "arbitrary"), vmem_limit_bytes=VMEM_LIMIT_BYTES),
        name="dsa",
    )(qit, wt, ki, qt, k, vt, tril)


def _hgrn_kernel(bq_ref, bf_ref, bi_ref, bg_ref, lb_ref, go_ref, o_ref, st_ref, oi_ref, *, tl, layer):
    cs = HGRN_CHUNK
    nch = tl // cs

    @pl.when(pl.program_id(1) == 0)
    def _():
        st_ref[...] = jnp.zeros(st_ref.shape, F32)

    lbr = lb_ref[...]
    slots = [lbr[k:k + 1] for k in range(lbr.shape[0])]
    mx = functools.reduce(jnp.maximum, slots)
    es = [jnp.exp(s - mx) for s in slots]
    lb = functools.reduce(jnp.add, es[:layer + 1]) / functools.reduce(jnp.add, es)

    bq = bq_ref[...]
    q = bq * _sigmoid(bq)
    f = lb + (1.0 - lb) * _sigmoid(bf_ref[...])
    kk = 1.0 - f
    g = jnp.log(f)
    v = bi_ref[...]

    r_i = lax.broadcasted_iota(I32, (tl, tl), 0)
    c_i = lax.broadcasted_iota(I32, (tl, tl), 1)
    same = (r_i >> _log2(cs)) == (c_i >> _log2(cs))
    tri = jnp.where(same & (c_i <= r_i), 1.0, 0.0).astype(BF16)
    blk = jnp.where(same, 1.0, 0.0).astype(BF16)
    g1, g2, g3 = _split3(g)
    b = _dot(tri, g1) + _dot(tri, g2) + _dot(tri, g3)
    bend = _dot(blk, g1) + _dot(blk, g2) + _dot(blk, g3)

    bk = b - jnp.log(kk)
    pos = lax.broadcasted_iota(I32, (tl, B_WIDTH), 0) & (cs - 1)
    o = jnp.zeros((tl, B_WIDTH), F32)
    for r in range(cs):
        if r == 0:
            bk_s, v_s = bk, v
        else:
            bk_s = pltpu.roll(bk, r, axis=0)
            v_s = pltpu.roll(v, r, axis=0)
        e = q * jnp.exp(jnp.where(pos >= r, b - bk_s, -jnp.inf))
        parts = []
        for h in range(B_HEADS):
            sl = slice(h * B_HEAD_DIM, (h + 1) * B_HEAD_DIM)
            parts.append(jnp.sum(e[:, sl], axis=-1, keepdims=True) * v_s[:, sl])
        o = o + jnp.concatenate(parts, axis=1)

    qe = (q * jnp.exp(b)).astype(BF16)
    kd = (kk * jnp.exp(bend - b)).astype(BF16)
    vb = v.astype(BF16)
    dec = jnp.exp(bend)
    for c in range(nch):
        rs = slice(c * cs, (c + 1) * cs)
        for h in range(B_HEADS):
            sl = slice(h * B_HEAD_DIM, (h + 1) * B_HEAD_DIM)
            st = st_ref[h]
            oi_ref[rs, sl] = lax.dot_general(qe[rs, sl], st.astype(BF16), (((1,), (1,)), ((), ())),
                                             preferred_element_type=F32)
            upd = lax.dot_general(vb[rs, sl], kd[rs, sl], (((0,), (0,)), ((), ())),
                                  preferred_element_type=F32)
            st_ref[h] = st * dec[c * cs:c * cs + 1, sl] + upd
    o = o + oi_ref[...]

    parts = []
    for h in range(B_HEADS):
        sl = slice(h * B_HEAD_DIM, (h + 1) * B_HEAD_DIM)
        oh = o[:, sl]
        parts.append(oh * lax.rsqrt(jnp.mean(oh * oh, axis=-1, keepdims=True) + EPS))
    on = jnp.concatenate(parts, axis=1) * go_ref[...]
    bg = bg_ref[...]
    o_ref[...] = (on * (bg * _sigmoid(bg))).astype(o_ref.dtype)


def _hgrn(proj, lb_table, go, bsz, seq, tl, layer):
    m = proj.shape[0]
    tpb = seq // tl
    col = lambda c: (lambda b, t: (b * tpb + t, c // B_WIDTH))
    return pl.pallas_call(
        functools.partial(_hgrn_kernel, tl=tl, layer=layer),
        grid=(bsz, tpb),
        in_specs=[
            pl.BlockSpec((tl, B_WIDTH), col(COL_BQ)),
            pl.BlockSpec((tl, B_WIDTH), col(COL_BF)),
            pl.BlockSpec((tl, B_WIDTH), col(COL_BI)),
            pl.BlockSpec((tl, B_WIDTH), col(COL_BG)),
            pl.BlockSpec(lb_table.shape, lambda b, t: (0, 0)),
            pl.BlockSpec((1, B_WIDTH), lambda b, t: (0, 0)),
        ],
        out_specs=pl.BlockSpec((tl, B_WIDTH), lambda b, t: (b * tpb + t, 0)),
        out_shape=jax.ShapeDtypeStruct((m, B_WIDTH), BF16),
        scratch_shapes=[
            pltpu.VMEM((B_HEADS, B_HEAD_DIM, B_HEAD_DIM), F32),
            pltpu.VMEM((tl, B_WIDTH), F32),
        ],
        compiler_params=pltpu.CompilerParams(
            dimension_semantics=("arbitrary", "arbitrary"), vmem_limit_bytes=VMEM_LIMIT_BYTES),
        name="hgrn",
    )(proj, proj, proj, proj, lb_table, go)


def _tail_kernel(oat_ref, ob_ref, ga_ref, gb_ref, x_ref, g1_ref, wa_ref, wb_ref, wo_ref,
                 n2_ref, sc_ref, sh_ref, g2_ref, w1_ref, b1_ref, w2_ref, b2_ref, o_ref, *, tf):
    tm = x_ref.shape[0]
    oa = oat_ref[0].reshape(A_WIDTH, tm).astype(F32).T.astype(BF16)
    pa = _dot(oa, wa_ref[...])
    pb = _dot(ob_ref[...], wb_ref[...])
    merged = _sigmoid(ga_ref[...]) * pa + _sigmoid(gb_ref[...]) * pb
    x = x_ref[...] + g1_ref[0] * _dot(merged.astype(BF16), wo_ref[...])
    ms = jnp.mean(x * x, axis=-1, keepdims=True)
    h = x * lax.rsqrt(ms + EPS) * n2_ref[...]
    h = (h * (1.0 + sc_ref[0]) + sh_ref[0]).astype(BF16)
    dff = w1_ref.shape[1]
    y = jnp.zeros(x.shape, F32)
    for c in range(dff // tf):
        cs = slice(c * tf, (c + 1) * tf)
        a = jnp.maximum(_dot(h, w1_ref[:, cs]) + b1_ref[:, cs], 0.0)
        y = y + _dot((a * a).astype(BF16), w2_ref[cs, :])
    o_ref[...] = x + g2_ref[0] * (y + b2_ref[...])


def _tail(oat, ob, proj, x2, gate1, wa, wb, wo, n2, scale2, shift2, gate2, w1, b1, w2, b2, seq, tm, tf):
    m, d = x2.shape
    tpb = seq // tm
    row = lambda i: (i, 0)
    bat = lambda i: (i // tpb, 0, 0)
    resident = lambda a: pl.BlockSpec(a.shape, lambda i: (0, 0), pipeline_mode=pl.Buffered(1))
    return pl.pallas_call(
        functools.partial(_tail_kernel, tf=tf),
        grid=(m // tm,),
        in_specs=[
            pl.BlockSpec((1, A_HEADS, A_HEAD_DIM, tm), lambda i: (i // tpb, 0, 0, i % tpb)),
            pl.BlockSpec((tm, B_WIDTH), row),
            pl.BlockSpec((tm, d), lambda i: (i, COL_GA // d)),
            pl.BlockSpec((tm, d), lambda i: (i, COL_GB // d)),
            pl.BlockSpec((tm, d), row),
            pl.BlockSpec((1, 1, d), bat),
            resident(wa), resident(wb), resident(wo),
            resident(n2),
            pl.BlockSpec((1, 1, d), bat),
            pl.BlockSpec((1, 1, d), bat),
            pl.BlockSpec((1, 1, d), bat),
            resident(w1), resident(b1), resident(w2), resident(b2),
        ],
        out_specs=pl.BlockSpec((tm, d), row),
        out_shape=jax.ShapeDtypeStruct((m, d), F32),
        compiler_params=pltpu.CompilerParams(vmem_limit_bytes=VMEM_LIMIT_BYTES),
        name="tail",
    )(oat, ob, proj, proj, x2, gate1, wa, wb, wo, n2, scale2, shift2, gate2, w1, b1, w2, b2)


def _regroup_kernel(w_ref, o_ref, *, d):
    kv0 = Q_LORA_RANK
    kw0 = kv0 + 2 * A_WIDTH
    rest0 = kw0 + IDX_DIM + IDX_HEADS
    rest = 4 * B_WIDTH + 2 * d
    w = w_ref[0]
    o_ref[:, COL_AK:COL_AK + 2 * A_WIDTH] = w[:, kv0:kw0].astype(BF16)
    o_ref[:, COL_BQ:COL_BQ + rest] = w[:, rest0:rest0 + rest].astype(BF16)
    o_ref[:, COL_QL:COL_QL + Q_LORA_RANK] = w[:, :Q_LORA_RANK].astype(BF16)
    kw = w[:, kw0:kw0 + LANES]
    lane = lax.broadcasted_iota(I32, kw.shape, 1)
    o_ref[:, COL_KW:COL_KW + LANES] = jnp.where(lane < IDX_DIM + IDX_HEADS, kw, 0.0).astype(BF16)


def _regroup_w_in(w_all, layer):
    _, d, n = w_all.shape
    tr = 128
    assert COL_BQ + 4 * B_WIDTH + 2 * d == COL_QL and COL_QL + Q_LORA_RANK == COL_KW
    return pl.pallas_call(
        functools.partial(_regroup_kernel, d=d),
        grid=(d // tr,),
        in_specs=[pl.BlockSpec((1, tr, n), lambda i: (layer, i, 0))],
        out_specs=pl.BlockSpec((tr, PROJ_COLS), lambda i: (i, 0)),
        out_shape=jax.ShapeDtypeStruct((d, PROJ_COLS), BF16),
        compiler_params=pltpu.CompilerParams(vmem_limit_bytes=VMEM_LIMIT_BYTES),
        name="regroup",
    )(w_all)


def _layer(x, mod, l, p, tiles):
    bsz, seq, d = x.shape
    m = bsz * seq
    shift1, scale1, gate1, shift2, scale2, gate2 = (mod[:, k * d:(k + 1) * d].reshape(bsz, 1, d) for k in range(6))
    x2 = x.reshape(m, d)

    proj = _inproj(x2, p['norm1_g'][l][None], scale1, shift1, _regroup_w_in(p['w_in'], l), seq, tiles['tm_in'])

    tq, ck = tiles['tq'], tiles['ck']
    wuq = jnp.concatenate([p['w_uq'][l], p['w_uq_idx'][l]], axis=1).astype(BF16)
    gki = jnp.concatenate([p['k_idx_norm_g'][l], jnp.zeros((LANES - IDX_DIM,), F32)])[None]
    qt, ak, qit, ki, wt, vt = _qkprep(
        proj, p['q_lat_norm_g'][l][None], wuq,
        jnp.tile(p['q_norm_g'][l], A_HEADS)[None], jnp.tile(p['k_norm_g'][l], A_HEADS)[None], gki, bsz, seq, ck)

    topk = min(TOPK_MAX, seq // 4)
    tril = jnp.tril(jnp.ones((ck, ck), BF16))
    out_at = _dsa(qit, wt, ki.reshape(bsz, seq, IDX_DIM), qt, ak.reshape(bsz, seq, A_WIDTH), vt, tril, tq, ck, topk)

    out_b = _hgrn(proj, p['hgrn_lb'], p['hgrn_o_norm_g'][l][None], bsz, seq, tiles['tl'], l)

    out = _tail(out_at, out_b, proj, x2, gate1, p['w_proj_a'][l].astype(BF16), p['w_proj_b'][l].astype(BF16),
                p['w_out'][l].astype(BF16), p['norm2_g'][l][None], scale2, shift2, gate2,
                p['w_mlp1'][l].astype(BF16), p['b_mlp1'][l][None], p['w_mlp2'][l].astype(BF16), p['b_mlp2'][l][None],
                seq, tiles['tm'], tiles['tf'])
    return out.reshape(bsz, seq, d)


def _tiles(seq):
    pick = lambda want: min(want, seq)
    return dict(tm_in=pick(512), tm=pick(512), tq=pick(256), ck=pick(512), tl=pick(256), tf=1024)


def kernel(x, c, w_ada, b_ada, norm1_g, w_in, q_lat_norm_g, w_uq, w_uq_idx, q_norm_g, k_norm_g, k_idx_norm_g,
           hgrn_lb, hgrn_o_norm_g, w_proj_a, w_proj_b, w_out, norm2_g, w_mlp1, b_mlp1, w_mlp2, b_mlp2):
    p = dict(norm1_g=norm1_g, w_in=w_in, q_lat_norm_g=q_lat_norm_g, w_uq=w_uq, w_uq_idx=w_uq_idx,
             q_norm_g=q_norm_g, k_norm_g=k_norm_g, k_idx_norm_g=k_idx_norm_g, hgrn_lb=hgrn_lb,
             hgrn_o_norm_g=hgrn_o_norm_g, w_proj_a=w_proj_a, w_proj_b=w_proj_b, w_out=w_out, norm2_g=norm2_g,
             w_mlp1=w_mlp1, b_mlp1=b_mlp1, w_mlp2=w_mlp2, b_mlp2=b_mlp2)
    bsz, seq, d = x.shape
    depth = w_ada.shape[0]
    tiles = _tiles(seq)
    c_pad = jnp.zeros((8, d), F32).at[:bsz].set(c)
    for l in range(depth):
        mod = _adaln(c_pad, w_ada[l], b_ada[l][None])[:bsz]
        x = _layer(x, mod, l, p, tiles)
    return x
```

```python
import functools

import jax
import jax.numpy as jnp
from jax import lax
from jax.experimental import pallas as pl
from jax.experimental.pallas import tpu as pltpu

F32 = jnp.float32
BF16 = jnp.bfloat16
I32 = jnp.int32

EPS = 1e-6
A_HEADS = 8
A_HEAD_DIM = 64
A_WIDTH = A_HEADS * A_HEAD_DIM
Q_LORA_RANK = 256
IDX_HEADS = 4
IDX_DIM = 64
TOPK_MAX = 256
B_WIDTH = 512
B_HEADS = 4
B_HEAD_DIM = 128
HGRN_CHUNK = 16

LANES = 128
VMEM_LIMIT_BYTES = 56 * 1024 * 1024

COL_AK, COL_AV, COL_BQ, COL_BF, COL_BI, COL_BG = 0, 512, 1024, 1536, 2048, 2560
COL_GA, COL_GB, COL_QL, COL_KW = 3072, 4096, 5120, 5376
PROJ_COLS = 5504

LOG2E = 1.4426950408889634
COUNT_SEARCH_PASSES = 12
SELECT_ACC_ROWS = 32
LOGIT_BOUND_LIMIT = 60.0


def _sigmoid(x):
    return 0.5 * jnp.tanh(0.5 * x) + 0.5


def _split2(x):
    hi = x.astype(BF16)
    lo = (x - hi.astype(F32)).astype(BF16)
    return hi, lo


def _split3(x):
    a = x.astype(BF16)
    r = x - a.astype(F32)
    b = r.astype(BF16)
    c = (r - b.astype(F32)).astype(BF16)
    return a, b, c


def _dot(a, b):
    return jnp.dot(a, b, preferred_element_type=F32)


def _log2(n):
    assert n > 0 and n & (n - 1) == 0, n
    return n.bit_length() - 1


def _group_ones(n, group):
    r = lax.broadcasted_iota(I32, (n, n), 0) >> _log2(group)
    c = lax.broadcasted_iota(I32, (n, n), 1) >> _log2(group)
    return jnp.where(r == c, 1.0, 0.0).astype(BF16)


def _group_mean_sq(x, ones_bd, group):
    hi, lo = _split2(x * x)
    return (_dot(hi, ones_bd) + _dot(lo, ones_bd)) * (1.0 / group)


def _fold_rows(x, op, init):
    r = init.shape[0]
    for s in range(x.shape[0] // r):
        init = op(init, x[s * r:(s + 1) * r, :])
    return init


def _adaln_kernel(c_ref, w_ref, b_ref, o_ref):
    c = c_ref[...]
    a = c * _sigmoid(c)
    a1, a2, a3 = _split3(a)
    w1, w2, w3 = _split3(w_ref[...])
    acc = _dot(a1, w1) + (_dot(a1, w2) + _dot(a2, w1)) + (_dot(a2, w2) + _dot(a1, w3) + _dot(a3, w1))
    o_ref[...] = acc + b_ref[...]


def _adaln(c_pad, w, b):
    rows, d = c_pad.shape
    n = w.shape[1]
    tn = 1536
    return pl.pallas_call(
        _adaln_kernel,
        grid=(n // tn,),
        in_specs=[
            pl.BlockSpec((rows, d), lambda j: (0, 0)),
            pl.BlockSpec((d, tn), lambda j: (0, j)),
            pl.BlockSpec((1, tn), lambda j: (0, j)),
        ],
        out_specs=pl.BlockSpec((rows, tn), lambda j: (0, j)),
        out_shape=jax.ShapeDtypeStruct((rows, n), F32),
        compiler_params=pltpu.CompilerParams(vmem_limit_bytes=VMEM_LIMIT_BYTES),
        name="adaln",
    )(c_pad, w, b)


def _inproj_kernel(x_ref, g_ref, sc_ref, sh_ref, w_ref, o_ref):
    x = x_ref[...]
    ms = jnp.mean(x * x, axis=-1, keepdims=True)
    h = x * lax.rsqrt(ms + EPS) * g_ref[...]
    h = h * (1.0 + sc_ref[0]) + sh_ref[0]
    o_ref[...] = _dot(h.astype(BF16), w_ref[...])


def _inproj(x2, g, scale, shift, w, seq, tm):
    m, d = x2.shape
    n = w.shape[1]
    tpb = seq // tm
    return pl.pallas_call(
        _inproj_kernel,
        grid=(m // tm,),
        in_specs=[
            pl.BlockSpec((tm, d), lambda i: (i, 0)),
            pl.BlockSpec((1, d), lambda i: (0, 0)),
            pl.BlockSpec((1, 1, d), lambda i: (i // tpb, 0, 0)),
            pl.BlockSpec((1, 1, d), lambda i: (i // tpb, 0, 0)),
            pl.BlockSpec((d, n), lambda i: (0, 0), pipeline_mode=pl.Buffered(1)),
        ],
        out_specs=pl.BlockSpec((tm, n), lambda i: (i, 0)),
        out_shape=jax.ShapeDtypeStruct((m, n), F32),
        compiler_params=pltpu.CompilerParams(vmem_limit_bytes=VMEM_LIMIT_BYTES),
        name="inproj",
    )(x2, g, scale, shift, w)


def _qkprep_kernel(ak_ref, av_ref, ql_ref, kw_ref, gql_ref, wuq_ref, gq_ref, gk_ref, gki_ref,
                   qt_out, ak_out, qit_out, ki_out, wt_out, vt_out):
    tm = ak_ref.shape[0]
    ones64 = _group_ones(A_WIDTH, A_HEAD_DIM)
    ql = ql_ref[...]
    ql = ql * lax.rsqrt(jnp.mean(ql * ql, axis=-1, keepdims=True) + EPS) * gql_ref[...]
    up = _dot(ql.astype(BF16), wuq_ref[...])
    aq = up[:, :A_WIDTH]
    aq = aq * lax.rsqrt(_group_mean_sq(aq, ones64, A_HEAD_DIM) + EPS) * gq_ref[...]
    aq_t = (aq * (A_HEAD_DIM ** -0.5 * LOG2E)).T
    zero_half = jnp.zeros((A_HEAD_DIM, tm), BF16)
    for h in range(A_HEADS):
        e = h % 2
        qt_out[0, h, e * A_HEAD_DIM:(e + 1) * A_HEAD_DIM, :] = aq_t[h * A_HEAD_DIM:(h + 1) * A_HEAD_DIM].astype(BF16)
        qt_out[0, h, (1 - e) * A_HEAD_DIM:(2 - e) * A_HEAD_DIM, :] = zero_half
    qit_out[0] = up[:, A_WIDTH:].T.reshape(IDX_HEADS, IDX_DIM, tm).astype(BF16)
    ak = ak_ref[...]
    ak = ak * lax.rsqrt(_group_mean_sq(ak, ones64, A_HEAD_DIM) + EPS) * gk_ref[...]
    ak_out[...] = ak.astype(BF16)
    vt_out[0, 0] = av_ref[...].T.reshape(A_HEADS, A_HEAD_DIM, tm).astype(BF16)
    kw = kw_ref[...]
    lane = lax.broadcasted_iota(I32, kw.shape, 1)
    ksq = jnp.where(lane < IDX_DIM, kw * kw, 0.0)
    kms = jnp.sum(ksq, axis=-1, keepdims=True) * (1.0 / IDX_DIM)
    kn = kw * lax.rsqrt(kms + EPS) * gki_ref[...]
    ki_out[...] = kn[:, :IDX_DIM].astype(BF16)
    idx_scale = (IDX_DIM ** -0.5) * (IDX_HEADS ** -0.5)
    w_rows = jnp.where((lane >= IDX_DIM) & (lane < IDX_DIM + IDX_HEADS), kw * idx_scale, 0.0).T
    wt_out[0] = w_rows[IDX_DIM:IDX_DIM + 8]


def _qkprep(proj, gql, wuq, gq, gk, gki, bsz, seq, tm):
    m = proj.shape[0]
    tpb = seq // tm
    row = lambda i: (i, 0)
    const = lambda i: (0, 0)
    return pl.pallas_call(
        _qkprep_kernel,
        grid=(m // tm,),
        in_specs=[
            pl.BlockSpec((tm, A_WIDTH), lambda i: (i, COL_AK // A_WIDTH)),
            pl.BlockSpec((tm, A_WIDTH), lambda i: (i, COL_AV // A_WIDTH)),
            pl.BlockSpec((tm, Q_LORA_RANK), lambda i: (i, COL_QL // Q_LORA_RANK)),
            pl.BlockSpec((tm, LANES), lambda i: (i, COL_KW // LANES)),
            pl.BlockSpec((1, Q_LORA_RANK), const),
            pl.BlockSpec(wuq.shape, const),
            pl.BlockSpec((1, A_WIDTH), const),
            pl.BlockSpec((1, A_WIDTH), const),
            pl.BlockSpec((1, LANES), const),
        ],
        out_specs=[
            pl.BlockSpec((1, A_HEADS, LANES, tm), lambda i: (i // tpb, 0, 0, i % tpb)),
            pl.BlockSpec((tm, A_WIDTH), row),
            pl.BlockSpec((1, IDX_HEADS, IDX_DIM, tm), lambda i: (i // tpb, 0, 0, i % tpb)),
            pl.BlockSpec((tm, IDX_DIM), row),
            pl.BlockSpec((1, 8, tm), lambda i: (i // tpb, 0, i % tpb)),
            pl.BlockSpec((1, 1, A_HEADS, A_HEAD_DIM, tm), lambda i: (i // tpb, i % tpb, 0, 0, 0)),
        ],
        out_shape=[
            jax.ShapeDtypeStruct((bsz, A_HEADS, LANES, seq), BF16),
            jax.ShapeDtypeStruct((m, A_WIDTH), BF16),
            jax.ShapeDtypeStruct((bsz, IDX_HEADS, IDX_DIM, seq), BF16),
            jax.ShapeDtypeStruct((m, IDX_DIM), BF16),
            jax.ShapeDtypeStruct((bsz, 8, seq), F32),
            jax.ShapeDtypeStruct((bsz, tpb, A_HEADS, A_HEAD_DIM, tm), BF16),
        ],
        compiler_params=pltpu.CompilerParams(vmem_limit_bytes=VMEM_LIMIT_BYTES),
        name="qkprep",
    )(proj, proj, proj, proj, gql, wuq, gq, gk, gki)


def _dsa_kernel(qit_ref, wt_ref, ki_ref, qt_ref, k_ref, vt_ref, tril_ref, o_ref,
                sc_ref, acc_ref, p_ref, kn_ref, *, tq, ck, topk, n_count_passes):
    i = pl.program_id(1)
    n_chunks = ((i + 1) * tq + ck - 1) // ck
    kf = float(topk)
    neg_inf = float("-inf")
    pos_inf = float("inf")
    ar = SELECT_ACC_ROWS
    full_acc = lambda val: jnp.full((ar, tq), val, F32)
    row_min = lambda x: jnp.min(x, axis=0, keepdims=True)
    row_max = lambda x: jnp.max(x, axis=0, keepdims=True)
    row_sum = lambda x: jnp.sum(x, axis=0, keepdims=True)
    chunk_rows = lambda jc: pl.ds(pl.multiple_of(jc * ck, ck), ck)

    @pl.when(i == 0)
    def _():
        head_of_lane = lax.broadcasted_iota(I32, (A_WIDTH, LANES), 0) >> _log2(A_HEAD_DIM)
        pick = jnp.where(head_of_lane == lax.broadcasted_iota(I32, (A_WIDTH, LANES), 1), 1.0, 0.0).astype(BF16)

        def kn_body(jc, mx):
            k = k_ref[0, chunk_rows(jc), :].astype(F32)
            return _fold_rows(_dot((k * k).astype(BF16), pick), jnp.maximum, mx)

        mx = lax.fori_loop(0, k_ref.shape[1] // ck, kn_body, jnp.zeros((8, LANES), F32))
        kn_ref[...] = jnp.broadcast_to(row_max(mx), (8, LANES))

    t_pos = i * tq + lax.broadcasted_iota(I32, (ck, tq), 1)
    s_iota = lax.broadcasted_iota(I32, (ck, tq), 0)

    def score_body(masked, jc, carry):
        mn, mx = carry
        kc = ki_ref[0, chunk_rows(jc), :]
        score = jnp.zeros((ck, tq), F32)
        for h in range(IDX_HEADS):
            score = score + wt_ref[0, h:h + 1, :] * jnp.maximum(_dot(kc, qit_ref[0, h]), 0.0)
        if masked:
            causal = (s_iota + jc * ck) <= t_pos
            sc_ref[jc] = jnp.where(causal, score, neg_inf)
            mn = _fold_rows(jnp.where(causal, score, pos_inf), jnp.minimum, mn)
            mx = _fold_rows(jnp.where(causal, score, neg_inf), jnp.maximum, mx)
        else:
            sc_ref[jc] = score
            mn = _fold_rows(score, jnp.minimum, mn)
            mx = _fold_rows(score, jnp.maximum, mx)
        return mn, mx

    n_full = (i * tq + 1) // ck
    carry = lax.fori_loop(0, n_full, functools.partial(score_body, False), (full_acc(pos_inf), full_acc(neg_inf)))
    mn_acc, mx_acc = lax.fori_loop(n_full, n_chunks, functools.partial(score_body, True), carry)
    rmin, rmax = row_min(mn_acc), row_max(mx_acc)

    def count_ge(cand):
        cand_b = jnp.broadcast_to(cand, (ar, tq))

        def body(jc, acc):
            return _fold_rows(sc_ref[jc], lambda a, x: a + jnp.where(x >= cand_b, 1.0, 0.0), acc)

        return row_sum(lax.fori_loop(0, n_chunks, body, full_acc(0.0)))

    def snap(lo, hi):
        lo_b = jnp.broadcast_to(lo, (ar, tq))
        hi_b = jnp.broadcast_to(hi, (ar, tq))

        def body(jc, carry):
            x = sc_ref[jc]
            a = _fold_rows(x, lambda a, x: jnp.minimum(a, jnp.where(x >= lo_b, x, pos_inf)), carry[0])
            b = _fold_rows(x, lambda b, x: jnp.maximum(b, jnp.where(x < hi_b, x, neg_inf)), carry[1])
            return a, b

        a, b = lax.fori_loop(0, n_chunks, body, (full_acc(pos_inf), full_acc(neg_inf)))
        return row_min(a), row_max(b)

    def probe(cand):
        cand_b = jnp.broadcast_to(cand, (ar, tq))

        def body(jc, carry):
            cnt, a, b = carry
            blk = sc_ref[jc]
            for s in range(ck // ar):
                x = blk[s * ar:(s + 1) * ar, :]
                ge = x >= cand_b
                cnt = cnt + jnp.where(ge, 1.0, 0.0)
                a = jnp.minimum(a, jnp.where(ge, x, pos_inf))
                b = jnp.maximum(b, jnp.where(ge, neg_inf, x))
            return cnt, a, b

        cnt, a, b = lax.fori_loop(0, n_chunks, body, (full_acc(0.0), full_acc(pos_inf), full_acc(neg_inf)))
        return row_sum(cnt), row_min(a), row_max(b)

    n_valid = (i * tq + 1 + lax.broadcasted_iota(I32, (1, tq), 1)).astype(F32)
    small = n_valid <= kf

    def interpolate(lo, hi, c_lo, c_hi, w_lo, w_hi):
        f_lo = (c_lo - kf + 0.5) * w_lo
        f_hi = (kf - c_hi - 0.5) * w_hi
        return lo + (hi - lo) * jnp.clip(f_lo / jnp.maximum(f_lo + f_hi, 0.5), 0.1, 0.9)

    def illinois(up, dn, w_lo, w_hi, last):
        w_hi = jnp.where(up, jnp.where(last > 0.0, w_hi * 0.5, 1.0), jnp.where(dn, 1.0, w_hi))
        w_lo = jnp.where(dn, jnp.where(last < 0.0, w_lo * 0.5, 1.0), jnp.where(up, 1.0, w_lo))
        return w_lo, w_hi, jnp.where(up, 1.0, jnp.where(dn, -1.0, last))

    def zero_stats():
        def body(jc, carry):
            c_ge0, c_pos, min_pos = carry
            blk = sc_ref[jc]
            for s in range(ck // ar):
                x = blk[s * ar:(s + 1) * ar, :]
                pos = x > 0.0
                c_ge0 = c_ge0 + jnp.where(x >= 0.0, 1.0, 0.0)
                c_pos = c_pos + jnp.where(pos, 1.0, 0.0)
                min_pos = jnp.minimum(min_pos, jnp.where(pos, x, pos_inf))
            return c_ge0, c_pos, min_pos

        c_ge0, c_pos, min_pos = lax.fori_loop(0, n_chunks, body, (full_acc(0.0), full_acc(0.0), full_acc(pos_inf)))
        return row_sum(c_ge0), row_sum(c_pos), row_min(min_pos)

    c_ge0, c_pos, min_pos = zero_stats()
    below_zero = c_ge0 < kf
    above_zero = c_pos >= kf
    at_zero = jnp.logical_not(below_zero | above_zero)
    hi0 = rmax + jnp.maximum(jnp.abs(rmax) * 1e-6, 1e-30)
    lo = jnp.where(above_zero, min_pos, jnp.where(at_zero, 0.0, rmin))
    c_lo = jnp.where(above_zero, c_pos, jnp.where(at_zero, c_ge0, n_valid))
    hi = jnp.where(above_zero, hi0, 0.0)
    c_hi = jnp.where(above_zero, 0.0, jnp.where(at_zero, c_pos, c_ge0))

    def search_body(_, st):
        lo, hi, c_lo, c_hi, w_lo, w_hi, last = st
        mid = interpolate(lo, hi, c_lo, c_hi, w_lo, w_hi)
        ok = (mid > lo) & (mid < hi)
        c = count_ge(mid)
        up = ok & (c >= kf)
        dn = ok & (c < kf)
        w_lo, w_hi, last = illinois(up, dn, w_lo, w_hi, last)
        return (jnp.where(up, mid, lo), jnp.where(dn, mid, hi), jnp.where(up, c, c_lo), jnp.where(dn, c, c_hi),
                w_lo, w_hi, last)

    zeros_q, ones_q = jnp.zeros((1, tq), F32), jnp.ones((1, tq), F32)
    lo, hi, c_lo, c_hi, _, _, _ = lax.fori_loop(0, n_count_passes, search_body,
                                                (lo, hi, c_lo, c_hi, ones_q, ones_q, zeros_q))

    lo, hi = snap(lo, hi)
    lo, hi = jnp.where(at_zero, 0.0, lo), jnp.where(at_zero, 0.0, hi)

    def active_of(lo, hi, c_lo):
        return jnp.logical_not(small) & (lo < hi) & (c_lo != kf)

    def snap_body(st):
        lo, hi, c_lo, c_hi, w_lo, w_hi, last, _ = st
        act = active_of(lo, hi, c_lo)
        mid = interpolate(lo, hi, c_lo, c_hi, w_lo, w_hi)
        mid = jnp.where((c_hi == kf - 1.0) | (mid <= lo) | (mid > hi), hi, mid)
        c, a, b = probe(mid)
        up = act & (c >= kf)
        dn = act & (c < kf)
        w_lo, w_hi, last = illinois(up, dn, w_lo, w_hi, last)
        lo, c_lo = jnp.where(up, a, lo), jnp.where(up, c, c_lo)
        hi, c_hi = jnp.where(dn, b, hi), jnp.where(dn, c, c_hi)
        return lo, hi, c_lo, c_hi, w_lo, w_hi, last, jnp.max(jnp.where(active_of(lo, hi, c_lo), 1.0, 0.0))

    flag0 = jnp.max(jnp.where(active_of(lo, hi, c_lo), 1.0, 0.0))
    lo, hi, c_lo, c_hi = lax.while_loop(lambda st: st[7] > 0.0, snap_body,
                                        (lo, hi, c_lo, c_hi, ones_q, ones_q, zeros_q, flag0))[:4]

    tau = jnp.where(small, rmin, lo)
    excess = jnp.logical_not(small) & (lo == hi) & (c_lo > kf)
    need = jnp.where(excess, kf - c_hi, 4.0 * 65536.0 * 65536.0)

    @pl.when(jnp.max(jnp.where(excess, 1.0, 0.0)) > 0.0)
    def _():
        def tie_body(jc, carry):
            blk = sc_ref[jc]
            eq = blk == tau
            pc = _dot(tril_ref[...], jnp.where(eq, 1.0, 0.0).astype(BF16)) + carry
            sc_ref[jc] = jnp.where(eq & (pc > need), neg_inf, blk)
            return pc[ck - 1:ck, :]

        lax.fori_loop(0, n_chunks, tie_body, jnp.zeros((1, tq), F32))

    def logits_t(jc, h):
        kc = k_ref[0, chunk_rows(jc), (h // 2) * LANES:(h // 2 + 1) * LANES]
        return _dot(kc, qt_ref[0, h])

    m_bound = []
    bmax = jnp.zeros((1, 1), F32)
    for h in range(A_HEADS):
        qf = qt_ref[0, h].astype(F32)
        bound = jnp.sqrt(row_sum(qf * qf) * kn_ref[0:1, h:h + 1])
        m_bound.append(bound)
        bmax = jnp.maximum(bmax, jnp.max(bound, axis=1, keepdims=True))

    def exact_max(_):
        def max_body(jc, mx):
            sel = sc_ref[jc] >= tau
            return tuple(_fold_rows(jnp.where(sel, logits_t(jc, h), neg_inf), jnp.maximum, mx[h])
                         for h in range(A_HEADS))

        mx = lax.fori_loop(0, n_chunks, max_body, tuple(jnp.full((8, tq), neg_inf, F32) for _ in range(A_HEADS)))
        return tuple(row_max(m) for m in mx)

    m_ref_vals = lax.cond(bmax[0, 0] > LOGIT_BOUND_LIMIT, exact_max, lambda _: tuple(m_bound), 0)

    ones_rows = jnp.ones((16, ck), BF16)
    acc_ref[...] = jnp.zeros(acc_ref.shape, F32)

    def pv_stage(jc, h):
        lhs = jnp.concatenate([vt_ref[0, jc, h], ones_rows], axis=0)
        acc_ref[h] += _dot(lhs, p_ref[h])

    def qk_stage(jc, sel, h):
        p_ref[h] = jnp.where(sel, jnp.exp2(logits_t(jc, h) - m_ref_vals[h]), 0.0).astype(BF16)

    sel0 = sc_ref[0] >= tau
    for h in range(A_HEADS):
        qk_stage(0, sel0, h)

    def att_body(jc, carry):
        sel = sc_ref[jc] >= tau
        for h in range(A_HEADS):
            pv_stage(jc - 1, h)
            qk_stage(jc, sel, h)
        return carry

    lax.fori_loop(1, n_chunks, att_body, 0)
    for h in range(A_HEADS):
        pv_stage(n_chunks - 1, h)
    for h in range(A_HEADS):
        a = acc_ref[h]
        o_ref[0, h] = (a[:A_HEAD_DIM] * (1.0 / a[A_HEAD_DIM:A_HEAD_DIM + 1])).astype(o_ref.dtype)


def _dsa(qit, wt, ki, qt, k, vt, tril, tq, ck, topk):
    bsz, hi, di, seq = qit.shape
    nc = seq // ck
    kern = functools.partial(_dsa_kernel, tq=tq, ck=ck, topk=topk, n_count_passes=COUNT_SEARCH_PASSES)
    return pl.pallas_call(
        kern,
        grid=(bsz, seq // tq),
        in_specs=[
            pl.BlockSpec((1, hi, di, tq), lambda b, i: (b, 0, 0, i)),
            pl.BlockSpec((1, 8, tq), lambda b, i: (b, 0, i)),
            pl.BlockSpec((1, seq, di), lambda b, i: (b, 0, 0), pipeline_mode=pl.Buffered(1)),
            pl.BlockSpec((1, A_HEADS, LANES, tq), lambda b, i: (b, 0, 0, i)),
            pl.BlockSpec((1, seq, A_WIDTH), lambda b, i: (b, 0, 0), pipeline_mode=pl.Buffered(1)),
            pl.BlockSpec((1, nc, A_HEADS, A_HEAD_DIM, ck), lambda b, i: (b, 0, 0, 0, 0), pipeline_mode=pl.Buffered(1)),
            pl.BlockSpec((ck, ck), lambda b, i: (0, 0)),
        ],
        out_specs=pl.BlockSpec((1, A_HEADS, A_HEAD_DIM, tq), lambda b, i: (b, 0, 0, i)),
        out_shape=jax.ShapeDtypeStruct((bsz, A_HEADS, A_HEAD_DIM, seq), BF16),
        scratch_shapes=[
            pltpu.VMEM((nc, ck, tq), F32),
            pltpu.VMEM((A_HEADS, A_HEAD_DIM + 16, tq), F32),
            pltpu.VMEM((A_HEADS, ck, tq), BF16),
            pltpu.VMEM((8, LANES), F32),
        ],
        compiler_params=pltpu.CompilerParams(
            dimension_semantics=("arbitrary", "arbitrary"), vmem_limit_bytes=VMEM_LIMIT_BYTES),
        name="dsa",
    )(qit, wt, ki, qt, k, vt, tril)


def _hgrn_kernel(bq_ref, bf_ref, bi_ref, bg_ref, lb_ref, go_ref, o_ref, st_ref, oi_ref, *, tl, layer):
    cs = HGRN_CHUNK
    nch = tl // cs

    @pl.when(pl.program_id(1) == 0)
    def _():
        st_ref[...] = jnp.zeros(st_ref.shape, F32)

    lbr = lb_ref[...]
    slots = [lbr[k:k + 1] for k in range(lbr.shape[0])]
    mx = functools.reduce(jnp.maximum, slots)
    es = [jnp.exp(s - mx) for s in slots]
    lb = functools.reduce(jnp.add, es[:layer + 1]) / functools.reduce(jnp.add, es)

    bq = bq_ref[...]
    q = bq * _sigmoid(bq)
    f = lb + (1.0 - lb) * _sigmoid(bf_ref[...])
    kk = 1.0 - f
    g = jnp.log(f)
    v = bi_ref[...]

    r_i = lax.broadcasted_iota(I32, (tl, tl), 0)
    c_i = lax.broadcasted_iota(I32, (tl, tl), 1)
    same = (r_i >> _log2(cs)) == (c_i >> _log2(cs))
    tri = jnp.where(same & (c_i <= r_i), 1.0, 0.0).astype(BF16)
    blk = jnp.where(same, 1.0, 0.0).astype(BF16)
    g1, g2, g3 = _split3(g)
    b = _dot(tri, g1) + _dot(tri, g2) + _dot(tri, g3)
    bend = _dot(blk, g1) + _dot(blk, g2) + _dot(blk, g3)

    bk = b - jnp.log(kk)
    pos = lax.broadcasted_iota(I32, (tl, B_WIDTH), 0) & (cs - 1)
    o = jnp.zeros((tl, B_WIDTH), F32)
    for r in range(cs):
        if r == 0:
            bk_s, v_s = bk, v
        else:
            bk_s = pltpu.roll(bk, r, axis=0)
            v_s = pltpu.roll(v, r, axis=0)
        e = q * jnp.exp(jnp.where(pos >= r, b - bk_s, -jnp.inf))
        parts = []
        for h in range(B_HEADS):
            sl = slice(h * B_HEAD_DIM, (h + 1) * B_HEAD_DIM)
            parts.append(jnp.sum(e[:, sl], axis=-1, keepdims=True) * v_s[:, sl])
        o = o + jnp.concatenate(parts, axis=1)

    qe = (q * jnp.exp(b)).astype(BF16)
    kd = (kk * jnp.exp(bend - b)).astype(BF16)
    vb = v.astype(BF16)
    dec = jnp.exp(bend)
    for c in range(nch):
        rs = slice(c * cs, (c + 1) * cs)
        for h in range(B_HEADS):
            sl = slice(h * B_HEAD_DIM, (h + 1) * B_HEAD_DIM)
            st = st_ref[h]
            oi_ref[rs, sl] = lax.dot_general(qe[rs, sl], st.astype(BF16), (((1,), (1,)), ((), ())),
                                             preferred_element_type=F32)
            upd = lax.dot_general(vb[rs, sl], kd[rs, sl], (((0,), (0,)), ((), ())),
                                  preferred_element_type=F32)
            st_ref[h] = st * dec[c * cs:c * cs + 1, sl] + upd
    o = o + oi_ref[...]

    parts = []
    for h in range(B_HEADS):
        sl = slice(h * B_HEAD_DIM, (h + 1) * B_HEAD_DIM)
        oh = o[:, sl]
        parts.append(oh * lax.rsqrt(jnp.mean(oh * oh, axis=-1, keepdims=True) + EPS))
    on = jnp.concatenate(parts, axis=1) * go_ref[...]
    bg = bg_ref[...]
    o_ref[...] = (on * (bg * _sigmoid(bg))).astype(o_ref.dtype)


def _hgrn(proj, lb_table, go, bsz, seq, tl, layer):
    m = proj.shape[0]
    tpb = seq // tl
    col = lambda c: (lambda b, t: (b * tpb + t, c // B_WIDTH))
    return pl.pallas_call(
        functools.partial(_hgrn_kernel, tl=tl, layer=layer),
        grid=(bsz, tpb),
        in_specs=[
            pl.BlockSpec((tl, B_WIDTH), col(COL_BQ)),
            pl.BlockSpec((tl, B_WIDTH), col(COL_BF)),
            pl.BlockSpec((tl, B_WIDTH), col(COL_BI)),
            pl.BlockSpec((tl, B_WIDTH), col(COL_BG)),
            pl.BlockSpec(lb_table.shape, lambda b, t: (0, 0)),
            pl.BlockSpec((1, B_WIDTH), lambda b, t: (0, 0)),
        ],
        out_specs=pl.BlockSpec((tl, B_WIDTH), lambda b, t: (b * tpb + t, 0)),
        out_shape=jax.ShapeDtypeStruct((m, B_WIDTH), BF16),
        scratch_shapes=[
            pltpu.VMEM((B_HEADS, B_HEAD_DIM, B_HEAD_DIM), F32),
            pltpu.VMEM((tl, B_WIDTH), F32),
        ],
        compiler_params=pltpu.CompilerParams(
            dimension_semantics=("arbitrary", "arbitrary"), vmem_limit_bytes=VMEM_LIMIT_BYTES),
        name="hgrn",
    )(proj, proj, proj, proj, lb_table, go)


def _tail_kernel(oat_ref, ob_ref, ga_ref, gb_ref, x_ref, g1_ref, wa_ref, wb_ref, wo_ref,
                 n2_ref, sc_ref, sh_ref, g2_ref, w1_ref, b1_ref, w2_ref, b2_ref, o_ref, *, tf):
    tm = x_ref.shape[0]
    oa = oat_ref[0].reshape(A_WIDTH, tm).astype(F32).T.astype(BF16)
    pa = _dot(oa, wa_ref[...])
    pb = _dot(ob_ref[...], wb_ref[...])
    merged = _sigmoid(ga_ref[...]) * pa + _sigmoid(gb_ref[...]) * pb
    x = x_ref[...] + g1_ref[0] * _dot(merged.astype(BF16), wo_ref[...])
    ms = jnp.mean(x * x, axis=-1, keepdims=True)
    h = x * lax.rsqrt(ms + EPS) * n2_ref[...]
    h = (h * (1.0 + sc_ref[0]) + sh_ref[0]).astype(BF16)
    dff = w1_ref.shape[1]
    y = jnp.zeros(x.shape, F32)
    for c in range(dff // tf):
        cs = slice(c * tf, (c + 1) * tf)
        a = jnp.maximum(_dot(h, w1_ref[:, cs]) + b1_ref[:, cs], 0.0)
        y = y + _dot((a * a).astype(BF16), w2_ref[cs, :])
    o_ref[...] = x + g2_ref[0] * (y + b2_ref[...])


def _tail(oat, ob, proj, x2, gate1, wa, wb, wo, n2, scale2, shift2, gate2, w1, b1, w2, b2, seq, tm, tf):
    m, d = x2.shape
    tpb = seq // tm
    row = lambda i: (i, 0)
    bat = lambda i: (i // tpb, 0, 0)
    resident = lambda a: pl.BlockSpec(a.shape, lambda i: (0, 0), pipeline_mode=pl.Buffered(1))
    return pl.pallas_call(
        functools.partial(_tail_kernel, tf=tf),
        grid=(m // tm,),
        in_specs=[
            pl.BlockSpec((1, A_HEADS, A_HEAD_DIM, tm), lambda i: (i // tpb, 0, 0, i % tpb)),
            pl.BlockSpec((tm, B_WIDTH), row),
            pl.BlockSpec((tm, d), lambda i: (i, COL_GA // d)),
            pl.BlockSpec((tm, d), lambda i: (i, COL_GB // d)),
            pl.BlockSpec((tm, d), row),
            pl.BlockSpec((1, 1, d), bat),
            resident(wa), resident(wb), resident(wo),
            resident(n2),
            pl.BlockSpec((1, 1, d), bat),
            pl.BlockSpec((1, 1, d), bat),
            pl.BlockSpec((1, 1, d), bat),
            resident(w1), resident(b1), resident(w2), resident(b2),
        ],
        out_specs=pl.BlockSpec((tm, d), row),
        out_shape=jax.ShapeDtypeStruct((m, d), F32),
        compiler_params=pltpu.CompilerParams(vmem_limit_bytes=VMEM_LIMIT_BYTES),
        name="tail",
    )(oat, ob, proj, proj, x2, gate1, wa, wb, wo, n2, scale2, shift2, gate2, w1, b1, w2, b2)


def _regroup_kernel(w_ref, o_ref, *, d):
    kv0 = Q_LORA_RANK
    kw0 = kv0 + 2 * A_WIDTH
    rest0 = kw0 + IDX_DIM + IDX_HEADS
    rest = 4 * B_WIDTH + 2 * d
    w = w_ref[0]
    o_ref[:, COL_AK:COL_AK + 2 * A_WIDTH] = w[:, kv0:kw0].astype(BF16)
    o_ref[:, COL_BQ:COL_BQ + rest] = w[:, rest0:rest0 + rest].astype(BF16)
    o_ref[:, COL_QL:COL_QL + Q_LORA_RANK] = w[:, :Q_LORA_RANK].astype(BF16)
    kw = w[:, kw0:kw0 + LANES]
    lane = lax.broadcasted_iota(I32, kw.shape, 1)
    o_ref[:, COL_KW:COL_KW + LANES] = jnp.where(lane < IDX_DIM + IDX_HEADS, kw, 0.0).astype(BF16)


def _regroup_w_in(w_all, layer):
    _, d, n = w_all.shape
    tr = 128
    assert COL_BQ + 4 * B_WIDTH + 2 * d == COL_QL and COL_QL + Q_LORA_RANK == COL_KW
    return pl.pallas_call(
        functools.partial(_regroup_kernel, d=d),
        grid=(d // tr,),
        in_specs=[pl.BlockSpec((1, tr, n), lambda i: (layer, i, 0))],
        out_specs=pl.BlockSpec((tr, PROJ_COLS), lambda i: (i, 0)),
        out_shape=jax.ShapeDtypeStruct((d, PROJ_COLS), BF16),
        compiler_params=pltpu.CompilerParams(vmem_limit_bytes=VMEM_LIMIT_BYTES),
        name="regroup",
    )(w_all)


def _layer(x, mod, l, p, tiles):
    bsz, seq, d = x.shape
    m = bsz * seq
    shift1, scale1, gate1, shift2, scale2, gate2 = (mod[:, k * d:(k + 1) * d].reshape(bsz, 1, d) for k in range(6))
    x2 = x.reshape(m, d)

    proj = _inproj(x2, p['norm1_g'][l][None], scale1, shift1, _regroup_w_in(p['w_in'], l), seq, tiles['tm_in'])

    tq, ck = tiles['tq'], tiles['ck']
    wuq = jnp.concatenate([p['w_uq'][l], p['w_uq_idx'][l]], axis=1).astype(BF16)
    gki = jnp.concatenate([p['k_idx_norm_g'][l], jnp.zeros((LANES - IDX_DIM,), F32)])[None]
    qt, ak, qit, ki, wt, vt = _qkprep(
        proj, p['q_lat_norm_g'][l][None], wuq,
        jnp.tile(p['q_norm_g'][l], A_HEADS)[None], jnp.tile(p['k_norm_g'][l], A_HEADS)[None], gki, bsz, seq, ck)

    topk = min(TOPK_MAX, seq // 4)
    tril = jnp.tril(jnp.ones((ck, ck), BF16))
    out_at = _dsa(qit, wt, ki.reshape(bsz, seq, IDX_DIM), qt, ak.reshape(bsz, seq, A_WIDTH), vt, tril, tq, ck, topk)

    out_b = _hgrn(proj, p['hgrn_lb'], p['hgrn_o_norm_g'][l][None], bsz, seq, tiles['tl'], l)

    out = _tail(out_at, out_b, proj, x2, gate1, p['w_proj_a'][l].astype(BF16), p['w_proj_b'][l].astype(BF16),
                p['w_out'][l].astype(BF16), p['norm2_g'][l][None], scale2, shift2, gate2,
                p['w_mlp1'][l].astype(BF16), p['b_mlp1'][l][None], p['w_mlp2'][l].astype(BF16), p['b_mlp2'][l][None],
                seq, tiles['tm'], tiles['tf'])
    return out.reshape(bsz, seq, d)


def _tiles(seq):
    pick = lambda want: min(want, seq)
    return dict(tm_in=pick(512), tm=pick(512), tq=pick(512), ck=pick(512), tl=pick(256), tf=1024)


def kernel(x, c, w_ada, b_ada, norm1_g, w_in, q_lat_norm_g, w_uq, w_uq_idx, q_norm_g, k_norm_g, k_idx_norm_g,
           hgrn_lb, hgrn_o_norm_g, w_proj_a, w_proj_b, w_out, norm2_g, w_mlp1, b_mlp1, w_mlp2, b_mlp2):
    p = dict(norm1_g=norm1_g, w_in=w_in, q_lat_norm_g=q_lat_norm_g, w_uq=w_uq, w_uq_idx=w_uq_idx,
             q_norm_g=q_norm_g, k_norm_g=k_norm_g, k_idx_norm_g=k_idx_norm_g, hgrn_lb=hgrn_lb,
             hgrn_o_norm_g=hgrn_o_norm_g, w_proj_a=w_proj_a, w_proj_b=w_proj_b, w_out=w_out, norm2_g=norm2_g,
             w_mlp1=w_mlp1, b_mlp1=b_mlp1, w_mlp2=w_mlp2, b_mlp2=b_mlp2)
    bsz, seq, d = x.shape
    depth = w_ada.shape[0]
    tiles = _tiles(seq)
    c_pad = jnp.zeros((8, d), F32).at[:bsz].set(c)
    for l in range(depth):
        mod = _adaln(c_pad, w_ada[l], b_ada[l][None])[:bsz]
        x = _layer(x, mod, l, p, tiles)
    return x
```
